```python
import math
import jax, jax.numpy as jnp
from jax import lax
import numpy as np

D_MODEL = 1024
BATCH = 8
SEQ = 4096
DEPTH = 2

CTX_LEN = 256
GRID_W = 64
N_MOD = 6
D_FF = 4 * D_MODEL
CHUNK = 64
NORM_EPS = 1e-6

HY_C = D_MODEL // 4
HY_ORDER = 2
HY_SHORT = 3
HY_EMB = 33
HY_FH = 64
HY_DECAY_TARGET = 1e-2
HY_FAST_PCT = 0.3
HY_SLOW_PCT = 1.5

GLA_H = 4
GLA_VW = 3 * D_MODEL // 8
GLA_KW = GLA_VW // 2
GLA_DK = GLA_KW // GLA_H
GLA_DV = GLA_VW // GLA_H
GLA_RANK = 16
GLA_TAU = 16.0

GDN_H = 4
GDN_W = 3 * D_MODEL // 8
GDN_D = GDN_W // GDN_H
GDN_SHORT = 7

D_MIX = HY_C + GLA_VW + GDN_W
HY_IN = 3 * HY_C
GLA_IN = 2 * GLA_KW + 2 * GLA_VW + 2 * GLA_RANK
GDN_IN = 4 * GDN_W + 4 * GDN_H
IN_SIZES = (HY_IN, GLA_IN, GDN_IN)
N_IN = HY_IN + GLA_IN + GDN_IN

kernel_name = 'hybrid_hyena_gla_gdn_prefix_dit'

F32 = jnp.float32


def split_last(t, sizes):
    return jnp.split(t, np.cumsum(sizes)[:-1].tolist(), axis=-1)


def rmsnorm(x, g):
    xf = x.astype(F32)
    y = xf * lax.rsqrt(jnp.mean(xf * xf, axis=-1, keepdims=True) + NORM_EPS)
    return (y * g.astype(F32)).astype(x.dtype)


def modulate(x, g, shift, scale):
    return rmsnorm(x, g) * (1 + scale) + shift


def to_heads(t, n_heads):
    b, l, _ = t.shape
    return t.reshape(b, l, n_heads, -1).transpose(0, 2, 1, 3)


def from_heads(t):
    b, h, l, d = t.shape
    return t.transpose(0, 2, 1, 3).reshape(b, l, h * d)


def l2norm(t):
    return t * lax.rsqrt(jnp.sum(t * t, axis=-1, keepdims=True) + NORM_EPS)


def short_conv(u, w, n_seg, seg_len):
    b, l, ch = u.shape
    k = w.shape[0]
    us = u.reshape(b * n_seg, seg_len, ch)
    out = lax.conv_general_dilated(us, w[:, None, :].astype(u.dtype), window_strides=(1,),
                                   padding=[(k // 2, k // 2)], dimension_numbers=('NWC', 'WIO', 'NWC'),
                                   feature_group_count=ch)
    return out.reshape(b, l, ch)


def hyena_filters(length, w1, b1, w2, b2, w3, sin_freq):
    t = jnp.linspace(0.0, 1.0, length, dtype=F32)[:, None]
    bands = (HY_EMB - 1) // 2
    f = jnp.linspace(1e-4, bands - 1, bands, dtype=F32)
    w = 2 * math.pi * jnp.arange(length, dtype=F32) / length
    ang = w[:, None] * f[None, :]
    z = jnp.concatenate([t, jnp.cos(ang), -jnp.sin(ang)], axis=-1)
    h = jnp.sin(sin_freq[0].astype(F32) * (z @ w1.astype(F32) + b1.astype(F32)))
    h = jnp.sin(sin_freq[1].astype(F32) * (h @ w2.astype(F32) + b2.astype(F32)))
    h = (h @ w3.astype(F32)).reshape(length, HY_ORDER, 2, HY_C)
    deltas = jnp.abs(jnp.linspace(math.log(HY_DECAY_TARGET) / HY_SLOW_PCT,
                                  math.log(HY_DECAY_TARGET) / HY_FAST_PCT, HY_C, dtype=F32))
    h = h * jnp.exp(-t * deltas[None, :])[:, None, None, :]
    h_fwd, h_bwd = h[:, :, 0], h[:, :, 1]
    return jnp.concatenate([h_fwd, jnp.zeros_like(h_fwd[:1]), h_bwd[:0:-1]], axis=0)


def hyena_mix(z, n_seg, seg_len, conv_w, conv_b, filt, d_skip):
    u = (short_conv(z, conv_w, n_seg, seg_len) + conv_b).astype(F32)
    v, x1, x2 = jnp.split(u, 3, axis=-1)
    l = u.shape[1]
    filt_f = jnp.fft.rfft(filt, axis=0)
    y = v
    for o, gate in enumerate((x1, x2)):
        conv = jnp.fft.irfft(jnp.fft.rfft(y, n=2 * l, axis=1) * filt_f[None, :, o], n=2 * l, axis=1)[:, :l]
        y = gate * (conv + d_skip[o].astype(F32) * y)
    return y.astype(z.dtype)


def to_chunks(t, n):
    return jnp.moveaxis(t.reshape(t.shape[:2] + (n, CHUNK) + t.shape[3:]), 2, 0)


def from_chunks(t):
    t = jnp.moveaxis(t, 0, 2)
    return t.reshape(t.shape[:2] + (-1,) + t.shape[4:])


def gla_chunk_scan(q, k, v, log_a, s0):
    n = q.shape[2] // CHUNK
    mask = jnp.tril(jnp.ones((CHUNK, CHUNK), bool))[:, :, None]

    def step(s, xs):
        qc, kc, vc, ac = xs
        b = jnp.cumsum(ac, axis=2)
        diff = b[:, :, :, None, :] - b[:, :, None, :, :]
        decay = jnp.exp(jnp.where(mask, diff, -jnp.inf))
        attn = jnp.einsum('bhid,bhjd,bhijd->bhij', qc, kc, decay)
        o = jnp.einsum('bhid,bhde->bhie', qc * jnp.exp(b), s) + jnp.einsum('bhij,bhje->bhie', attn, vc)
        b_last = b[:, :, -1:, :]
        s_new = jnp.exp(b_last[:, :, 0, :])[..., None] * s + jnp.einsum(
            'bhjd,bhje->bhde', kc * jnp.exp(b_last - b), vc)
        return s_new, o

    s_fin, o = lax.scan(step, s0, (to_chunks(q, n), to_chunks(k, n), to_chunks(v, n), to_chunks(log_a, n)))
    return from_chunks(o), s_fin


def gdn_chunk_scan(q, k, v, g, beta, s0):
    n = q.shape[2] // CHUNK
    dk = q.shape[-1]
    strict = jnp.tril(jnp.ones((CHUNK, CHUNK), bool), -1)
    incl = jnp.tril(jnp.ones((CHUNK, CHUNK), bool))
    eye = jnp.eye(CHUNK, dtype=F32)

    def step(s, xs):
        qc, kc, vc, gc, bc = xs
        gcum = jnp.cumsum(gc, axis=-1)
        diff = gcum[..., :, None] - gcum[..., None, :]
        kb = kc * bc[..., None]
        a = -jnp.einsum('bhid,bhjd->bhij', kb, kc) * jnp.exp(jnp.where(strict, diff, -jnp.inf))
        rhs = jnp.concatenate([kb * jnp.exp(gcum)[..., None], vc * bc[..., None]], axis=-1)
        sol = lax.linalg.triangular_solve(eye - a, rhs, left_side=True, lower=True, unit_diagonal=True)
        w, u = sol[..., :dk], sol[..., dk:]
        v_new = u - w @ s
        attn = jnp.einsum('bhid,bhjd->bhij', qc, kc) * jnp.exp(jnp.where(incl, diff, -jnp.inf))
        o = (qc * jnp.exp(gcum)[..., None]) @ s + attn @ v_new
        g_last = gcum[..., -1:]
        s_new = jnp.exp(g_last)[..., None] * s + jnp.einsum(
            'bhjd,bhje->bhde', kc * jnp.exp(g_last - gcum)[..., None], v_new)
        return s_new, o

    s_fin, o = lax.scan(step, s0, (to_chunks(q, n), to_chunks(k, n), to_chunks(v, n),
                                   to_chunks(g, n), to_chunks(beta, n)))
    return from_chunks(o), s_fin


def bidir_scan(scan_fn, shared, fwd, bwd, s0_f, s0_b):
    o_f, s_f = scan_fn(*shared, *fwd, s0_f)
    flip = lambda a: jnp.flip(a, axis=2)
    o_b, s_b = scan_fn(*[flip(a) for a in shared], *[flip(a) for a in bwd], s0_b)
    return o_f + flip(o_b), s_f, s_b


def gated_head_norm(o, gate, g_norm):
    y = o * lax.rsqrt(jnp.mean(o * o, axis=-1, keepdims=True) + NORM_EPS) * g_norm.astype(F32)
    return from_heads(y) * jax.nn.silu(gate.astype(F32))


def gla_prep(z, w_a2, b_a):
    zf = z.astype(F32)
    q, k, v, gate, a_f, a_b = split_last(zf, (GLA_KW, GLA_KW, GLA_VW, GLA_VW, GLA_RANK, GLA_RANK))
    log_a = [to_heads(jax.nn.log_sigmoid(a @ w_a2[d].astype(F32) + b_a[d].astype(F32)) / GLA_TAU, GLA_H)
             for d, a in enumerate((a_f, a_b))]
    shared = (to_heads(q, GLA_H) * GLA_DK ** -0.5, to_heads(k, GLA_H), to_heads(v, GLA_H))
    return shared, (log_a[0],), (log_a[1],), gate


def gla_branch(z_lat, z_ctx, with_ctx_out, w_a2, b_a, norm_g):
    sh_c, fw_c, bw_c, gate_c = gla_prep(z_ctx, w_a2, b_a)
    s0 = jnp.zeros((z_ctx.shape[0], GLA_H, GLA_DK, GLA_DV), F32)
    o_c, s_f, s_b = bidir_scan(gla_chunk_scan, sh_c, fw_c, bw_c, s0, s0)
    sh_l, fw_l, bw_l, gate_l = gla_prep(z_lat, w_a2, b_a)
    o_l, _, _ = bidir_scan(gla_chunk_scan, sh_l, fw_l, bw_l, s_f, s_b)
    out_lat = gated_head_norm(o_l, gate_l, norm_g).astype(z_lat.dtype)
    out_ctx = gated_head_norm(o_c, gate_c, norm_g).astype(z_ctx.dtype) if with_ctx_out else None
    return out_lat, out_ctx


def gdn_prep(z, conv_w, a_log, dt_bias, n_seg, seg_len):
    qkv, gate, a_f, a_b, b_f, b_b = split_last(z, (3 * GDN_W, GDN_W, GDN_H, GDN_H, GDN_H, GDN_H))
    qkv = jax.nn.silu(short_conv(qkv, conv_w, n_seg, seg_len)).astype(F32)
    q, k, v = jnp.split(qkv, 3, axis=-1)
    shared = (l2norm(to_heads(q, GDN_H)) * GDN_D ** -0.5, l2norm(to_heads(k, GDN_H)), to_heads(v, GDN_H))

    def log_decay(a, d):
        return (-jnp.exp(a_log[d].astype(F32)) *
                jax.nn.softplus(a.astype(F32) + dt_bias[d].astype(F32))).transpose(0, 2, 1)

    def write_gate(b):
        return jax.nn.sigmoid(b.astype(F32)).transpose(0, 2, 1)

    return shared, (log_decay(a_f, 0), write_gate(b_f)), (log_decay(a_b, 1), write_gate(b_b)), gate


def gdn_branch(z_lat, z_ctx, rows, with_ctx_out, conv_w, a_log, dt_bias, norm_g):
    sh_c, fw_c, bw_c, gate_c = gdn_prep(z_ctx, conv_w, a_log, dt_bias, 1, z_ctx.shape[1])
    s0 = jnp.zeros((z_ctx.shape[0], GDN_H, GDN_D, GDN_D), F32)
    o_c, s_f, s_b = bidir_scan(gdn_chunk_scan, sh_c, fw_c, bw_c, s0, s0)
    sh_l, fw_l, bw_l, gate_l = gdn_prep(z_lat, conv_w, a_log, dt_bias, rows, GRID_W)
    o_l, _, _ = bidir_scan(gdn_chunk_scan, sh_l, fw_l, bw_l, s_f, s_b)
    out_lat = gated_head_norm(o_l, gate_l, norm_g).astype(z_lat.dtype)
    out_ctx = gated_head_norm(o_c, gate_c, norm_g).astype(z_ctx.dtype) if with_ctx_out else None
    return out_lat, out_ctx


def mixer(h, hc, rows, with_ctx_out, w_in, w_out, hy_conv_w, hy_conv_b, hy_filter, hy_d,
          gla_w_a2, gla_b_a, gla_norm_g, gdn_conv_w, gdn_a_log, gdn_dt_bias, gdn_norm_g):
    zl_hy, zl_gla, zl_gdn = split_last(h @ w_in, IN_SIZES)
    zc_hy, zc_gla, zc_gdn = split_last(hc @ w_in, IN_SIZES)
    hy_lat = hyena_mix(zl_hy, rows, GRID_W, hy_conv_w, hy_conv_b, hyena_filters(h.shape[1], *hy_filter), hy_d)
    gla_lat, gla_ctx = gla_branch(zl_gla, zc_gla, with_ctx_out, gla_w_a2, gla_b_a, gla_norm_g)
    gdn_lat, gdn_ctx = gdn_branch(zl_gdn, zc_gdn, rows, with_ctx_out, gdn_conv_w, gdn_a_log, gdn_dt_bias, gdn_norm_g)
    y_lat = jnp.concatenate([hy_lat, gla_lat, gdn_lat], axis=-1) @ w_out
    if not with_ctx_out:
        return y_lat, None
    hy_ctx = hyena_mix(zc_hy, 1, hc.shape[1], hy_conv_w, hy_conv_b, hyena_filters(hc.shape[1], *hy_filter), hy_d)
    y_ctx = jnp.concatenate([hy_ctx, gla_ctx, gdn_ctx], axis=-1) @ w_out
    return y_lat, y_ctx


def sq_relu_mlp(h, w1, w2):
    return jnp.square(jax.nn.relu(h @ w1)) @ w2


def setup_inputs(seed: int = 0) -> dict:
    key = jax.random.key(seed)
    ks = iter(jax.random.split(key, 32))
    nrm = lambda shape, std: std * jax.random.normal(next(ks), shape, F32)
    dt = jnp.exp(jax.random.uniform(next(ks), (DEPTH, 2, GDN_H), F32, math.log(1e-3), math.log(1e-1)))
    return {
        'x': nrm((BATCH, SEQ, D_MODEL), 1.0),
        'c': nrm((BATCH, D_MODEL), 1.0),
        'ctx': nrm((BATCH, CTX_LEN, D_MODEL), 1.0),
        'c_ctx': nrm((D_MODEL,), 1.0),
        'norm1_g': 1.0 + nrm((DEPTH, D_MODEL), 0.05),
        'norm2_g': 1.0 + nrm((DEPTH, D_MODEL), 0.05),
        'w_mod': nrm((DEPTH, D_MODEL, N_MOD * D_MODEL), D_MODEL ** -0.5),
        'b_mod': nrm((DEPTH, N_MOD * D_MODEL), 0.02),
        'w_in': nrm((DEPTH, D_MODEL, N_IN), D_MODEL ** -0.5),
        'w_out': nrm((DEPTH, D_MIX, D_MODEL), D_MIX ** -0.5),
        'hy_conv_w': nrm((DEPTH, HY_SHORT, HY_IN), HY_SHORT ** -0.5),
        'hy_conv_b': nrm((DEPTH, HY_IN), 0.02),
        'hy_f_w1': nrm((DEPTH, HY_EMB, HY_FH), HY_EMB ** -0.5),
        'hy_f_b1': nrm((DEPTH, HY_FH), 0.02),
        'hy_f_w2': nrm((DEPTH, HY_FH, HY_FH), HY_FH ** -0.5),
        'hy_f_b2': nrm((DEPTH, HY_FH), 0.02),
        'hy_f_w3': nrm((DEPTH, HY_FH, HY_ORDER * 2 * HY_C), 0.05 * HY_FH ** -0.5),
        'hy_sin_freq': 1.0 + nrm((DEPTH, 2, HY_FH), 0.05),
        'hy_d': nrm((DEPTH, HY_ORDER, HY_C), 1.0),
        'gla_w_a2': nrm((DEPTH, 2, GLA_RANK, GLA_KW), GLA_RANK ** -0.5),
        'gla_b_a': nrm((DEPTH, 2, GLA_KW), 0.02),
        'gla_norm_g': 1.0 + nrm((DEPTH, GLA_DV), 0.05),
        'gdn_conv_w': nrm((DEPTH, GDN_SHORT, 3 * GDN_W), GDN_SHORT ** -0.5),
        'gdn_a_log': jnp.log(jax.random.uniform(next(ks), (DEPTH, 2, GDN_H), F32, 1.0, 16.0)),
        'gdn_dt_bias': dt + jnp.log(-jnp.expm1(-dt)),
        'gdn_norm_g': 1.0 + nrm((DEPTH, GDN_D), 0.05),
        'w_mlp1': nrm((DEPTH, D_MODEL, D_FF), D_MODEL ** -0.5),
        'w_mlp2': nrm((DEPTH, D_FF, D_MODEL), D_FF ** -0.5),
        'final_norm_g': 1.0 + nrm((D_MODEL,), 0.05),
    }


def reference(x, c, ctx, c_ctx, norm1_g, norm2_g, w_mod, b_mod, w_in, w_out, hy_conv_w, hy_conv_b,
              hy_f_w1, hy_f_b1, hy_f_w2, hy_f_b2, hy_f_w3, hy_sin_freq, hy_d, gla_w_a2, gla_b_a, gla_norm_g,
              gdn_conv_w, gdn_a_log, gdn_dt_bias, gdn_norm_g, w_mlp1, w_mlp2, final_norm_g):
    rows = x.shape[1] // GRID_W
    silu_c = jax.nn.silu(c)
    silu_cc = jax.nn.silu(c_ctx)
    for l in range(DEPTH):
        with_ctx_out = l < DEPTH - 1
        sh1, sc1, g1, sh2, sc2, g2 = jnp.split((silu_c @ w_mod[l] + b_mod[l])[:, None, :], N_MOD, axis=-1)
        csh1, csc1, cg1, csh2, csc2, cg2 = jnp.split(silu_cc @ w_mod[l] + b_mod[l], N_MOD, axis=-1)
        h = modulate(x, norm1_g[l], sh1, sc1)
        hc = modulate(ctx, norm1_g[l], csh1, csc1)
        hy_filter = (hy_f_w1[l], hy_f_b1[l], hy_f_w2[l], hy_f_b2[l], hy_f_w3[l], hy_sin_freq[l])
        y, yc = mixer(h, hc, rows, with_ctx_out, w_in[l], w_out[l], hy_conv_w[l], hy_conv_b[l], hy_filter,
                      hy_d[l], gla_w_a2[l], gla_b_a[l], gla_norm_g[l], gdn_conv_w[l], gdn_a_log[l],
                      gdn_dt_bias[l], gdn_norm_g[l])
        x = x + g1 * y
        x = x + g2 * sq_relu_mlp(modulate(x, norm2_g[l], sh2, sc2), w_mlp1[l], w_mlp2[l])
        if with_ctx_out:
            ctx = ctx + cg1 * yc
            ctx = ctx + cg2 * sq_relu_mlp(modulate(ctx, norm2_g[l], csh2, csc2), w_mlp1[l], w_mlp2[l])
    return rmsnorm(x, final_norm_g)
```

```python
import functools
import math

import numpy as np
import jax
import jax.numpy as jnp
from jax import lax
from jax.experimental import pallas as pl
from jax.experimental.pallas import tpu as pltpu

F32 = jnp.float32
BF16 = jnp.bfloat16
HIGHEST = lax.Precision.HIGHEST

NORM_EPS = 1e-6
N_MOD = 6
GRID_W = 64
CHUNK = 64
LANES = 128

HY_ORDER = 2
HY_SHORT = 3
HY_EMB = 33
HY_DECAY_TARGET = 1e-2
HY_FAST_PCT = 0.3
HY_SLOW_PCT = 1.5

GLA_H = 4
GLA_RANK = 16
GLA_TAU = 16.0
GDN_H = 4
GDN_DP = LANES

VMEM_LIMIT = 56 * 1024 * 1024


def _cparams(n_grid):
    return pltpu.CompilerParams(dimension_semantics=("arbitrary",) * n_grid, vmem_limit_bytes=VMEM_LIMIT)


def _bdot(a, b):
    return jnp.dot(a.astype(BF16), b.astype(BF16), preferred_element_type=F32)


def _bdot_nt(a, b):
    return lax.dot_general(a.astype(BF16), b.astype(BF16), (((1,), (1,)), ((), ())), preferred_element_type=F32)


def _bdot_tn(a, b):
    return lax.dot_general(a.astype(BF16), b.astype(BF16), (((0,), (0,)), ((), ())), preferred_element_type=F32)


def _const_spec(shape):
    return pl.BlockSpec(shape, lambda *_: (0,) * len(shape))


def _mod_kernel(c_ref, w_ref, b_ref, o_ref):
    c = c_ref[...]
    o_ref[0] = _bdot(c * jax.nn.sigmoid(c), w_ref[0]) + b_ref[0]


def _modulation(cc, w_mod, b_mod):
    depth, d, n = w_mod.shape
    tn = n // 4
    return pl.pallas_call(
        _mod_kernel,
        grid=(depth, n // tn),
        in_specs=[pl.BlockSpec(cc.shape, lambda l, j: (0, 0)),
                  pl.BlockSpec((1, d, tn), lambda l, j: (l, 0, j)),
                  pl.BlockSpec((1, 1, tn), lambda l, j: (l, 0, j))],
        out_specs=pl.BlockSpec((1, cc.shape[0], tn), lambda l, j: (l, 0, j)),
        out_shape=jax.ShapeDtypeStruct((depth, cc.shape[0], n), F32),
        compiler_params=_cparams(2),
        name="modulation",
    )(cc, w_mod.astype(BF16), b_mod.reshape(depth, 1, n))


def _rms_modulate(x, g, shift, scale):
    y = x * lax.rsqrt(jnp.mean(x * x, axis=-1, keepdims=True) + NORM_EPS) * g
    return y * (1.0 + scale) + shift


def _inproj_kernel(x_ref, sh_ref, sc_ref, g_ref, *refs):
    n_out = len(refs) // 2
    hb = _rms_modulate(x_ref[0], g_ref[...], sh_ref[0], sc_ref[0]).astype(BF16)
    for w_ref, o_ref in zip(refs[:n_out], refs[n_out:]):
        o_ref[0] = jnp.dot(hb, w_ref[...], preferred_element_type=F32)


def _inproj(x, shift, scale, g, weights, tm):
    b, l, d = x.shape
    vec = pl.BlockSpec((1, 1, d), lambda i, j: (i, 0, 0))
    return pl.pallas_call(
        _inproj_kernel,
        grid=(b, l // tm),
        in_specs=[pl.BlockSpec((1, tm, d), lambda i, j: (i, j, 0)), vec, vec, _const_spec((1, d))]
        + [_const_spec(w.shape) for w in weights],
        out_specs=[pl.BlockSpec((1, tm, w.shape[1]), lambda i, j: (i, j, 0)) for w in weights],
        out_shape=[jax.ShapeDtypeStruct((b, l, w.shape[1]), F32) for w in weights],
        compiler_params=_cparams(2),
        name="inproj",
    )(x, shift, scale, g.reshape(1, d), *weights)


def _seg_conv(z, w, seg_len):
    rows, width = z.shape
    taps = w.shape[0]
    half = taps // 2
    pos = lax.broadcasted_iota(jnp.int32, (rows, width), 0) % seg_len
    acc = z * w[half:half + 1, :]
    for k in range(taps):
        d = k - half
        if d == 0:
            continue
        shifted = pltpu.roll(z, (-d) % rows, 0)
        valid = (pos >= -d) if d < 0 else (pos < seg_len - d)
        acc = acc + jnp.where(valid, shifted, 0.0) * w[k:k + 1, :]
    return acc


def _hy_conv_kernel(z_ref, w_ref, b_ref, o_ref, *, seg_len):
    o_ref[0] = _seg_conv(z_ref[0], w_ref[...], seg_len) + b_ref[...]


def _hy_short_conv(z, w, bias, seg_len, tl):
    b, l, width = z.shape
    return pl.pallas_call(
        functools.partial(_hy_conv_kernel, seg_len=seg_len),
        grid=(b, l // tl),
        in_specs=[pl.BlockSpec((1, tl, width), lambda i, j: (i, j, 0)), _const_spec(w.shape),
                  _const_spec((1, width))],
        out_specs=pl.BlockSpec((1, tl, width), lambda i, j: (i, j, 0)),
        out_shape=jax.ShapeDtypeStruct(z.shape, F32),
        compiler_params=_cparams(2),
        name="hy_short_conv",
    )(z, w, bias.reshape(1, width))


def _fft_tables(n1, n2):
    n = n1 * n2
    s1 = np.arange(n1 // 2)[None, :, None]
    s2 = np.arange(n2)[:, None, None]
    k1 = np.arange(n1)[None, None, :]
    theta = 2.0 * np.pi * ((k1 * (n2 * s1 + s2)) % n) / n
    tt = np.concatenate([np.cos(theta), -np.sin(theta)], axis=-1)
    phi = 2.0 * np.pi * ((np.arange(n2)[:, None] * np.arange(n2)[None, :]) % n2) / n2
    c2, s2m = np.cos(phi), np.sin(phi)
    fwd = np.block([[c2, s2m], [-s2m, c2]])
    inv = np.block([[c2, -s2m], [s2m, c2]])
    return jnp.asarray(tt, BF16), jnp.asarray(fwd, BF16), jnp.asarray(inv, BF16)


def _fftconv_kernel(u_ref, gate_ref, hr_ref, hi_ref, d_ref, tt_ref, fwd_ref, inv_ref, o_ref, spec_ref, *, n1, n2):
    n1h = n1 // 2

    def stage_a(s2, carry):
        slab = u_ref[0, pl.ds(s2, n1h, stride=n2), :]
        res = _bdot_tn(tt_ref[s2], slab)
        spec_ref[pl.ds(s2, n1, stride=2 * n2), :] = res[:n1]
        spec_ref[pl.ds(n2 + s2, n1, stride=2 * n2), :] = res[n1:]
        return carry

    lax.fori_loop(0, n2, stage_a, 0)

    def stage_b(k1, carry):
        rows = pl.ds(pl.multiple_of(k1 * 2 * n2, 2 * n2), 2 * n2)
        spec = jnp.dot(fwd_ref[...], spec_ref[rows, :].astype(BF16), preferred_element_type=F32)
        br, bi = spec[:n2], spec[n2:]
        hrows = pl.ds(pl.multiple_of(k1 * n2, n2), n2)
        hr, hi = hr_ref[hrows, :], hi_ref[hrows, :]
        y = jnp.concatenate([br * hr - bi * hi, br * hi + bi * hr], axis=0)
        spec_ref[rows, :] = jnp.dot(inv_ref[...], y.astype(BF16), preferred_element_type=F32)
        return carry

    lax.fori_loop(0, n1, stage_b, 0)

    def stage_a_inv(s2, carry):
        g = jnp.concatenate([spec_ref[pl.ds(s2, n1, stride=2 * n2), :],
                             spec_ref[pl.ds(n2 + s2, n1, stride=2 * n2), :]], axis=0)
        conv = jnp.dot(tt_ref[s2], g.astype(BF16), preferred_element_type=F32)
        rows = pl.ds(s2, n1h, stride=n2)
        u = u_ref[0, rows, :]
        o_ref[0, rows, :] = gate_ref[0, rows, :] * (conv + d_ref[...] * u)
        return carry

    lax.fori_loop(0, n2, stage_a_inv, 0)


def _fftconv_gate(u_arr, u_blk, gate_arr, gate_blk, hr, hi, d, tables, n1, n2):
    tt, fwd, inv = tables
    b, l, _ = u_arr.shape
    n, c = hr.shape
    ncb = c // LANES
    seq = lambda off: pl.BlockSpec((1, l, LANES), lambda j, i: (i, 0, off + j))
    chan = lambda rows: pl.BlockSpec((rows, LANES), lambda j, i: (0, j))
    return pl.pallas_call(
        functools.partial(_fftconv_kernel, n1=n1, n2=n2),
        grid=(ncb, b),
        in_specs=[seq(u_blk), seq(gate_blk), chan(n), chan(n), chan(1),
                  _const_spec(tt.shape), _const_spec(fwd.shape), _const_spec(inv.shape)],
        out_specs=pl.BlockSpec((1, l, LANES), lambda j, i: (i, 0, j)),
        out_shape=jax.ShapeDtypeStruct((b, l, c), F32),
        scratch_shapes=[pltpu.VMEM((2 * n, LANES), F32)],
        compiler_params=_cparams(2),
        name="hy_fftconv",
    )(u_arr, gate_arr, hr, hi, d.reshape(1, c), tt, fwd, inv)


def _dft_tables(l):
    n = 2 * l
    ang = 2.0 * np.pi * ((np.arange(l)[:, None] * np.arange(l)[None, :]) % n) / n
    cf = np.cos(ang)
    sf = -np.sin(ang)
    sf[0, :] = (-1.0) ** np.arange(l)
    fwd = np.concatenate([cf, sf], axis=0)
    return jnp.asarray(fwd, BF16), jnp.asarray(fwd.T, BF16)


def _dftconv_kernel(u_ref, gate_ref, p_ref, q_ref, r_ref, d_ref, fwd_ref, inv_ref, o_ref):
    u = u_ref[0]
    l = u.shape[0]
    spec = jnp.dot(fwd_ref[...], u.astype(BF16), preferred_element_type=F32)
    xr, xi = spec[:l], spec[l:]
    q = q_ref[...]
    y = jnp.concatenate([xr * p_ref[...] - xi * q, xr * q + xi * r_ref[...]], axis=0)
    conv = jnp.dot(inv_ref[...], y.astype(BF16), preferred_element_type=F32)
    o_ref[0] = gate_ref[0] * (conv + d_ref[...] * u)


def _dftconv_gate(u_arr, u_blk, gate_arr, gate_blk, p, q, r, d, tables):
    fwd, inv = tables
    b, l, _ = u_arr.shape
    c = p.shape[1]
    seq = lambda off: pl.BlockSpec((1, l, c), lambda i: (i, 0, off))
    return pl.pallas_call(
        _dftconv_kernel,
        grid=(b,),
        in_specs=[seq(u_blk), seq(gate_blk), _const_spec(p.shape), _const_spec(q.shape), _const_spec(r.shape),
                  _const_spec((1, c)), _const_spec(fwd.shape), _const_spec(inv.shape)],
        out_specs=pl.BlockSpec((1, l, c), lambda i: (i, 0, 0)),
        out_shape=jax.ShapeDtypeStruct((b, l, c), F32),
        compiler_params=_cparams(1),
        name="hy_dftconv",
    )(u_arr, gate_arr, p, q, r, d.reshape(1, c), fwd, inv)


def _chunk_cumsum_mats(tl):
    r = np.arange(tl)
    same = (r[:, None] // CHUNK) == (r[None, :] // CHUNK)
    lower = same & (r[None, :] <= r[:, None])
    upper = same & (r[None, :] >= r[:, None])
    return jnp.asarray(lower, F32), jnp.asarray(upper, F32)


def _gla_prep_kernel(a_ref, w_ref, b_ref, lo_ref, up_ref, bf_ref, bb_ref):
    a = a_ref[0]
    for d, (tri_ref, o_ref) in enumerate(((lo_ref, bf_ref), (up_ref, bb_ref))):
        pre = jnp.dot(a[:, d * GLA_RANK:(d + 1) * GLA_RANK], w_ref[d], precision=HIGHEST,
                      preferred_element_type=F32) + b_ref[d]
        log_a = jax.nn.log_sigmoid(pre) * (1.0 / GLA_TAU)
        o_ref[0] = jnp.dot(tri_ref[...], log_a, precision=HIGHEST, preferred_element_type=F32)


def _gla_prep(small, w_a2, b_a, tl):
    b, l, sw = small.shape
    kw = w_a2.shape[-1]
    lower, upper = _chunk_cumsum_mats(tl)
    out = pl.BlockSpec((1, tl, kw), lambda i, j: (i, j, 0))
    return pl.pallas_call(
        _gla_prep_kernel,
        grid=(b, l // tl),
        in_specs=[pl.BlockSpec((1, tl, sw), lambda i, j: (i, j, 0)), _const_spec(w_a2.shape),
                  _const_spec((2, 1, kw)), _const_spec((tl, tl)), _const_spec((tl, tl))],
        out_specs=[out, out],
        out_shape=[jax.ShapeDtypeStruct((b, l, kw), F32)] * 2,
        compiler_params=_cparams(2),
        name="gla_prep",
    )(small, w_a2, b_a.reshape(2, 1, kw), lower, upper)


def _gla_scan_kernel(qkf_ref, vf_ref, bf_ref, qkb_ref, vb_ref, bb_ref, s0f_ref, s0b_ref, hm_ref, vm_ref, bd_ref,
                     of_ref, ob_ref, sff_ref, sfb_ref, stf_ref, stb_ref, *, n_chunks, kw, q_scale):
    j = pl.program_id(1)

    @pl.when(j == 0)
    def _():
        stf_ref[...] = s0f_ref[0]
        stb_ref[...] = s0b_ref[0]

    ri = lax.broadcasted_iota(jnp.int32, (CHUNK, CHUNK), 0)
    ci = lax.broadcasted_iota(jnp.int32, (CHUNK, CHUNK), 1)
    hm = hm_ref[...]
    vm = vm_ref[...]
    bd = bd_ref[...]

    def one_chunk(qk_ref, v_ref, b_ref, o_ref, st_ref, c, keep, last, mid):
        rows = pl.ds(pl.multiple_of(c * CHUNK, CHUNK), CHUNK)
        qk = qk_ref[0, rows, :]
        q, k = qk[:, :kw] * q_scale, qk[:, kw:]
        v = v_ref[0, rows, :]
        bc = b_ref[0, rows, :]
        st = st_ref[...]
        b_mid = bc[mid:mid + 1, :]
        b_last = bc[last:last + 1, :]
        qt = q * jnp.exp(bc - b_mid)
        kt = (k * jnp.exp(b_mid - bc)).astype(BF16)
        o = _bdot_nt(q * jnp.exp(bc), st)
        for h in range(GLA_H):
            attn = jnp.where(keep, _bdot_nt(qt * hm[h:h + 1, :], kt), 0.0)
            o = o + _bdot(attn, v * vm[h:h + 1, :])
        o_ref[0, rows, :] = o
        kd = k * jnp.exp(b_last - bc)
        st_ref[...] = st * jnp.exp(b_last) + _bdot_tn(v, kd) * bd

    def body(c, carry):
        one_chunk(qkf_ref, vf_ref, bf_ref, of_ref, stf_ref, c, ri >= ci, CHUNK - 1, CHUNK // 2 - 1)
        one_chunk(qkb_ref, vb_ref, bb_ref, ob_ref, stb_ref, n_chunks - 1 - c, ci >= ri, 0, CHUNK // 2)
        return carry

    lax.fori_loop(0, n_chunks, body, 0)

    @pl.when(j == pl.num_programs(1) - 1)
    def _():
        sff_ref[0] = stf_ref[...]
        sfb_ref[0] = stb_ref[...]


def _gla_scan(qk, v, b_f, b_b, s0_f, s0_b, tl):
    b, l, kw2 = qk.shape
    kw, vw = kw2 // 2, v.shape[-1]
    dk, dv = kw // GLA_H, vw // GLA_H
    nblk = l // tl
    heads_k = np.arange(kw) // dk
    heads_v = np.arange(vw) // dv
    hm = jnp.asarray(np.arange(8)[:, None] == heads_k[None, :], F32)
    vm = jnp.asarray(np.arange(8)[:, None] == heads_v[None, :], F32)
    bd = jnp.asarray(heads_v[:, None] == heads_k[None, :], F32)
    fwd = lambda w: pl.BlockSpec((1, tl, w), lambda i, j: (i, j, 0))
    bwd = lambda w: pl.BlockSpec((1, tl, w), lambda i, j: (i, nblk - 1 - j, 0))
    state = pl.BlockSpec((1, vw, kw), lambda i, j: (i, 0, 0))
    return pl.pallas_call(
        functools.partial(_gla_scan_kernel, n_chunks=tl // CHUNK, kw=kw, q_scale=dk ** -0.5),
        grid=(b, nblk),
        in_specs=[fwd(kw2), fwd(vw), fwd(kw), bwd(kw2), bwd(vw), bwd(kw), state, state,
                  _const_spec(hm.shape), _const_spec(vm.shape), _const_spec(bd.shape)],
        out_specs=[fwd(vw), bwd(vw), state, state],
        out_shape=[jax.ShapeDtypeStruct((b, l, vw), F32)] * 2 + [jax.ShapeDtypeStruct((b, vw, kw), F32)] * 2,
        scratch_shapes=[pltpu.VMEM((vw, kw), F32)] * 2,
        compiler_params=_cparams(2),
        name="gla_scan",
    )(qk, v, b_f, qk, v, b_b, s0_f, s0_b, hm, vm, bd)


GDN_GF, GDN_GB, GDN_BF, GDN_BB = 32, 36, 40, 44

def _gdn_prep_kernel(z_ref, w_ref, small_ref, nea_ref, dtb_ref, lo_ref, up_ref,
                     q_ref, k_ref, v_ref, gc_ref, gct_ref, *, seg_len, q_scale):
    width = q_ref.shape[-1]
    y = _seg_conv(z_ref[0], w_ref[...], seg_len)
    y = y * jax.nn.sigmoid(y)
    for part, (o_ref, scale) in enumerate(((q_ref, q_scale), (k_ref, 1.0))):
        for h in range(GDN_H):
            cols = slice(part * width + h * GDN_DP, part * width + (h + 1) * GDN_DP)
            t = y[:, cols]
            o_ref[0, :, h * GDN_DP:(h + 1) * GDN_DP] = t * (
                lax.rsqrt(jnp.sum(t * t, axis=-1, keepdims=True) + NORM_EPS) * scale)
    v_ref[0] = y[:, 2 * width:]
    small = small_ref[0]
    log_decay = nea_ref[...] * jax.nn.softplus(small + dtb_ref[...])
    cum_f = jnp.dot(lo_ref[...], log_decay, precision=HIGHEST, preferred_element_type=F32)
    cum_b = jnp.dot(up_ref[...], log_decay, precision=HIGHEST, preferred_element_type=F32)
    lane = lax.broadcasted_iota(jnp.int32, small.shape, 1)
    gc = jnp.where(lane < GDN_GB, cum_f, jnp.where(lane < GDN_BF, cum_b, jax.nn.sigmoid(small)))
    gc_ref[0] = gc
    for c in range(small.shape[0] // CHUNK):
        gct_ref[0, c] = gc[c * CHUNK:(c + 1) * CHUNK, :].T


def _gdn_prep(z, conv_w, small, a_log, dt_bias, head_dim, seg_len, tl):
    b, l, w3 = z.shape
    width = w3 // 3
    nea = jnp.zeros((1, LANES), F32).at[0, GDN_GF:GDN_BF].set(-jnp.exp(a_log.reshape(-1)))
    dtb = jnp.zeros((1, LANES), F32).at[0, GDN_GF:GDN_BF].set(dt_bias.reshape(-1))
    lower, upper = _chunk_cumsum_mats(tl)
    tok = lambda w: pl.BlockSpec((1, tl, w), lambda i, j: (i, j, 0))
    nck = tl // CHUNK
    return pl.pallas_call(
        functools.partial(_gdn_prep_kernel, seg_len=seg_len, q_scale=head_dim ** -0.5),
        grid=(b, l // tl),
        in_specs=[tok(w3), _const_spec(conv_w.shape), tok(LANES), _const_spec((1, LANES)), _const_spec((1, LANES)),
                  _const_spec((tl, tl)), _const_spec((tl, tl))],
        out_specs=[tok(width), tok(width), tok(width), tok(LANES),
                   pl.BlockSpec((1, nck, LANES, CHUNK), lambda i, j: (i, j, 0, 0))],
        out_shape=[jax.ShapeDtypeStruct((b, l, width), F32)] * 3
        + [jax.ShapeDtypeStruct((b, l, LANES), F32), jax.ShapeDtypeStruct((b, l // CHUNK, LANES, CHUNK), F32)],
        compiler_params=_cparams(2),
        name="gdn_prep",
    )(z, conv_w, small, nea, dtb, lower, upper)


def _unit_tri_inverse(a, eye, m16, m32, m64):
    a_d = a * m16
    t = eye + a_d
    p = a_d
    for _ in range(3):
        p = _bdot(p, p)
        t = t + _bdot(t, p)
    t = t + _bdot(_bdot(t, a * m32), t)
    return t + _bdot(_bdot(t, a * m64), t)


def _gdn_scan_kernel(qf_ref, kf_ref, vf_ref, gcf_ref, gctf_ref, qb_ref, kb_ref, vb_ref, gcb_ref, gctb_ref,
                     s0f_ref, s0b_ref, of_ref, ob_ref, sff_ref, sfb_ref, sf_ref, sb_ref, *, n_chunks):
    j = pl.program_id(1)

    @pl.when(j == 0)
    def _():
        sf_ref[...] = s0f_ref[0]
        sb_ref[...] = s0b_ref[0]

    ri = lax.broadcasted_iota(jnp.int32, (CHUNK, CHUNK), 0)
    ci = lax.broadcasted_iota(jnp.int32, (CHUNK, CHUNK), 1)
    eye = (ri == ci).astype(F32)
    blk = lambda n: ((ri // n) == (ci // n))
    m16 = blk(16).astype(F32)
    m32 = (blk(32) & ~blk(16)).astype(F32)
    m64 = (~blk(32)).astype(F32)

    def one_chunk(q_ref, k_ref, v_ref, gc_ref, gct_ref, o_ref, s_ref, c, strict, incl, last, g_col, b_col):
        rows = pl.ds(pl.multiple_of(c * CHUNK, CHUNK), CHUNK)
        gc = gc_ref[0, rows, :]
        gct = gct_ref[0, c]
        for h in range(GDN_H):
            cols = slice(h * GDN_DP, (h + 1) * GDN_DP)
            q, k, v = q_ref[0, rows, cols], k_ref[0, rows, cols], v_ref[0, rows, cols]
            s = s_ref[h]
            gcum = gc[:, g_col + h:g_col + h + 1]
            beta = gc[:, b_col + h:b_col + h + 1]
            decay = jnp.exp(jnp.minimum(gcum - gct[g_col + h:g_col + h + 1, :], 0.0))
            e_col = jnp.exp(gcum)
            kb = k * beta
            a = -_bdot_nt(kb, k) * jnp.where(strict, decay, 0.0)
            t = _unit_tri_inverse(a, eye, m16, m32, m64)
            wu = _bdot(t, jnp.concatenate([kb * e_col, v * beta], axis=1))
            v_new = wu[:, GDN_DP:] - _bdot(wu[:, :GDN_DP], s)
            attn = _bdot_nt(q, k) * jnp.where(incl, decay, 0.0)
            o_ref[0, rows, cols] = _bdot(q * e_col, s) + _bdot(attn, v_new)
            g_last = gcum[last:last + 1, :]
            s_ref[h] = jnp.exp(g_last) * s + _bdot_tn(k * jnp.exp(g_last - gcum), v_new)

    def body(c, carry):
        one_chunk(qf_ref, kf_ref, vf_ref, gcf_ref, gctf_ref, of_ref, sf_ref, c,
                  ri > ci, ri >= ci, CHUNK - 1, GDN_GF, GDN_BF)
        one_chunk(qb_ref, kb_ref, vb_ref, gcb_ref, gctb_ref, ob_ref, sb_ref, n_chunks - 1 - c,
                  ci > ri, ci >= ri, 0, GDN_GB, GDN_BB)
        return carry

    lax.fori_loop(0, n_chunks, body, 0)

    @pl.when(j == pl.num_programs(1) - 1)
    def _():
        sff_ref[0] = sf_ref[...]
        sfb_ref[0] = sb_ref[...]


def _gdn_scan(q, k, v, gc, gct, s0_f, s0_b, tl):
    b, l, width = q.shape
    nblk = l // tl
    nck = tl // CHUNK
    fwd = lambda w: pl.BlockSpec((1, tl, w), lambda i, j: (i, j, 0))
    bwd = lambda w: pl.BlockSpec((1, tl, w), lambda i, j: (i, nblk - 1 - j, 0))
    gct_f = pl.BlockSpec((1, nck, LANES, CHUNK), lambda i, j: (i, j, 0, 0))
    gct_b = pl.BlockSpec((1, nck, LANES, CHUNK), lambda i, j: (i, nblk - 1 - j, 0, 0))
    state = pl.BlockSpec((1, GDN_H, GDN_DP, GDN_DP), lambda i, j: (i, 0, 0, 0))
    st_shape = jax.ShapeDtypeStruct((b, GDN_H, GDN_DP, GDN_DP), F32)
    return pl.pallas_call(
        functools.partial(_gdn_scan_kernel, n_chunks=nck),
        grid=(b, nblk),
        in_specs=[fwd(width), fwd(width), fwd(width), fwd(LANES), gct_f,
                  bwd(width), bwd(width), bwd(width), bwd(LANES), gct_b, state, state],
        out_specs=[fwd(width), bwd(width), state, state],
        out_shape=[jax.ShapeDtypeStruct((b, l, width), F32)] * 2 + [st_shape] * 2,
        scratch_shapes=[pltpu.VMEM((GDN_H, GDN_DP, GDN_DP), F32)] * 2,
        compiler_params=_cparams(2),
        name="gdn_scan",
    )(q, k, v, gc, gct, q, k, v, gc, gct, s0_f, s0_b)


def _gated_head_norm(o, gate, g_norm, ones_bd, inv_d):
    sq = o * o
    hi = sq.astype(BF16)
    lo = (sq - hi.astype(F32)).astype(BF16)
    ms = (jnp.dot(hi, ones_bd, preferred_element_type=F32) + jnp.dot(lo, ones_bd, preferred_element_type=F32)) * inv_d
    return o * lax.rsqrt(ms + NORM_EPS) * g_norm * (gate * jax.nn.sigmoid(gate))


def _outproj_kernel(x_ref, g1_ref, hy_ref, glf_ref, glb_ref, glg_ref, gdf_ref, gdb_ref, gdg_ref,
                    gln_ref, gdn_ref, glm_ref, gdm_ref, why_ref, wgl_ref, wgd_ref, o_ref, *, gla_dv, gdn_d):
    y_gla = _gated_head_norm(glf_ref[0] + glb_ref[0], glg_ref[0], gln_ref[...], glm_ref[...], 1.0 / gla_dv)
    y_gdn = _gated_head_norm(gdf_ref[0] + gdb_ref[0], gdg_ref[0], gdn_ref[...], gdm_ref[...], 1.0 / gdn_d)
    acc = _bdot(hy_ref[0], why_ref[...]) + _bdot(y_gla, wgl_ref[...]) + _bdot(y_gdn, wgd_ref[...])
    o_ref[0] = x_ref[0] + g1_ref[0] * acc


def _outproj(x, g1, hy, gla_f, gla_b, gla_gate, gdn_f, gdn_b, gdn_gate, gla_g, gdn_g, w_hy, w_gla, w_gdn,
             gla_dv, gdn_d, tm):
    b, l, d = x.shape
    hyw, glw, gdw = hy.shape[-1], gla_f.shape[-1], gdn_f.shape[-1]
    gl_heads = np.arange(glw) // gla_dv
    gd_heads = np.arange(gdw) // GDN_DP
    gl_m = jnp.asarray(gl_heads[:, None] == gl_heads[None, :], BF16)
    gd_m = jnp.asarray(gd_heads[:, None] == gd_heads[None, :], BF16)
    tok = lambda w: pl.BlockSpec((1, tm, w), lambda i, j: (i, j, 0))
    return pl.pallas_call(
        functools.partial(_outproj_kernel, gla_dv=gla_dv, gdn_d=gdn_d),
        grid=(b, l // tm),
        in_specs=[tok(d), pl.BlockSpec((1, 1, d), lambda i, j: (i, 0, 0)), tok(hyw), tok(glw), tok(glw), tok(glw),
                  tok(gdw), tok(gdw), tok(gdw), _const_spec((1, glw)), _const_spec((1, gdw)),
                  _const_spec(gl_m.shape), _const_spec(gd_m.shape),
                  _const_spec(w_hy.shape), _const_spec(w_gla.shape), _const_spec(w_gdn.shape)],
        out_specs=tok(d),
        out_shape=jax.ShapeDtypeStruct(x.shape, F32),
        compiler_params=_cparams(2),
        name="outproj",
    )(x, g1, hy, gla_f, gla_b, gla_gate, gdn_f, gdn_b, gdn_gate, gla_g, gdn_g, gl_m, gd_m, w_hy, w_gla, w_gdn)


def _mlp_kernel(x_ref, sh_ref, sc_ref, g2_ref, ng_ref, w1_ref, w2_ref, fg_ref, o_ref, hn_ref, acc_ref, *, final_norm):
    kk = pl.program_id(2)

    @pl.when(kk == 0)
    def _():
        hn_ref[...] = _rms_modulate(x_ref[0], ng_ref[...], sh_ref[0], sc_ref[0]).astype(BF16)
        acc_ref[...] = jnp.zeros_like(acc_ref)

    hid = jnp.maximum(jnp.dot(hn_ref[...], w1_ref[...], preferred_element_type=F32), 0.0)
    acc_ref[...] += _bdot(hid * hid, w2_ref[...])

    @pl.when(kk == pl.num_programs(2) - 1)
    def _():
        y = x_ref[0] + g2_ref[0] * acc_ref[...]
        if final_norm:
            y = y * lax.rsqrt(jnp.mean(y * y, axis=-1, keepdims=True) + NORM_EPS) * fg_ref[...]
        o_ref[0] = y


def _mlp(x, shift, scale, gate, norm_g, w1, w2, final_g, final_norm, tm, th):
    b, l, d = x.shape
    dff = w1.shape[1]
    vec = pl.BlockSpec((1, 1, d), lambda i, j, k: (i, 0, 0))
    row = pl.BlockSpec((1, d), lambda i, j, k: (0, 0))
    return pl.pallas_call(
        functools.partial(_mlp_kernel, final_norm=final_norm),
        grid=(b, l // tm, dff // th),
        in_specs=[pl.BlockSpec((1, tm, d), lambda i, j, k: (i, j, 0)), vec, vec, vec, row,
                  pl.BlockSpec((d, th), lambda i, j, k: (0, k)), pl.BlockSpec((th, d), lambda i, j, k: (k, 0)), row],
        out_specs=pl.BlockSpec((1, tm, d), lambda i, j, k: (i, j, 0)),
        out_shape=jax.ShapeDtypeStruct(x.shape, F32),
        scratch_shapes=[pltpu.VMEM((tm, d), BF16), pltpu.VMEM((tm, d), F32)],
        compiler_params=_cparams(3),
        name="mlp",
    )(x, shift, scale, gate, norm_g.reshape(1, d), w1, w2, final_g.reshape(1, d))


def _hyena_filter_spectrum(length, w1, b1, w2, b2, w3, sin_freq, hy_c):
    t = jnp.linspace(0.0, 1.0, length, dtype=F32)[:, None]
    bands = (HY_EMB - 1) // 2
    f = jnp.linspace(1e-4, bands - 1, bands, dtype=F32)
    w = 2 * math.pi * jnp.arange(length, dtype=F32) / length
    ang = w[:, None] * f[None, :]
    z = jnp.concatenate([t, jnp.cos(ang), -jnp.sin(ang)], axis=-1)
    h = jnp.sin(sin_freq[0] * (z @ w1 + b1))
    h = jnp.sin(sin_freq[1] * (h @ w2 + b2))
    h = (h @ w3).reshape(length, HY_ORDER, 2, hy_c)
    deltas = jnp.abs(jnp.linspace(math.log(HY_DECAY_TARGET) / HY_SLOW_PCT,
                                  math.log(HY_DECAY_TARGET) / HY_FAST_PCT, hy_c, dtype=F32))
    h = h * jnp.exp(-t * deltas[None, :])[:, None, None, :]
    h_fwd, h_bwd = h[:, :, 0], h[:, :, 1]
    filt = jnp.concatenate([h_fwd, jnp.zeros_like(h_fwd[:1]), h_bwd[:0:-1]], axis=0)
    return jnp.fft.rfft(filt, axis=0)


def _spectrum_two_stage(spec, n1, n2):
    n = n1 * n2
    full = jnp.concatenate([spec, jnp.conj(spec[-2:0:-1])], axis=0) / n
    full = full.reshape(n2, n1, *full.shape[1:]).swapaxes(0, 1).reshape(n, *full.shape[1:])
    return [(jnp.real(full[:, o]).astype(F32), jnp.imag(full[:, o]).astype(F32)) for o in range(HY_ORDER)]


def _spectrum_dense(spec):
    l = spec.shape[0] - 1
    n = 2 * l
    re, im = jnp.real(spec).astype(F32), jnp.imag(spec).astype(F32)
    out = []
    for o in range(HY_ORDER):
        p = (2.0 / n) * re[:l, o]
        q = (2.0 / n) * im[:l, o]
        r = p
        p = p.at[0].set(re[0, o] / n)
        q = q.at[0].set(0.0)
        r = r.at[0].set(re[l, o] / n)
        out.append((p, q, r))
    return out


def _pad_heads(w, n_heads, axis):
    shape = w.shape
    d = shape[axis] // n_heads
    w = w.reshape(shape[:axis] + (n_heads, d) + shape[axis + 1:])
    pad = [(0, 0)] * w.ndim
    pad[axis + 1] = (0, GDN_DP - d)
    w = jnp.pad(w, pad)
    return w.reshape(shape[:axis] + (n_heads * GDN_DP,) + shape[axis + 1:])


def kernel(x, c, ctx, c_ctx, norm1_g, norm2_g, w_mod, b_mod, w_in, w_out, hy_conv_w, hy_conv_b, hy_f_w1, hy_f_b1, hy_f_w2, hy_f_b2, hy_f_w3, hy_sin_freq, hy_d, gla_w_a2, gla_b_a, gla_norm_g, gdn_conv_w, gdn_a_log, gdn_dt_bias, gdn_norm_g, w_mlp1, w_mlp2, final_norm_g):
    batch, seq, d_model = x.shape
    ctx_len = ctx.shape[1]
    depth = w_mod.shape[0]
    hy_c = hy_d.shape[-1]
    gla_kw = gla_w_a2.shape[-1]
    gla_dv = gla_norm_g.shape[-1]
    gla_vw = GLA_H * gla_dv
    gdn_d = gdn_norm_g.shape[-1]
    gdn_w = GDN_H * gdn_d
    gdn_wp = GDN_H * GDN_DP

    n2 = LANES
    n1 = 2 * seq // n2
    fft_tabs = _fft_tables(n1, n2)
    dft_tabs = _dft_tables(ctx_len)

    pad_rows = -(batch + 1) % 8
    cc = jnp.concatenate([c, c_ctx[None, :], jnp.zeros((pad_rows, d_model), F32)], axis=0)
    mod = _modulation(cc, w_mod, b_mod)

    def mod_vecs(l, i):
        v = mod[l, :, i * d_model:(i + 1) * d_model]
        lat = v[:batch, None, :]
        cx = jnp.broadcast_to(v[batch][None, None, :], (batch, 1, d_model))
        return lat, cx

    zeros_gla = jnp.zeros((batch, gla_vw, gla_kw), F32)
    zeros_gdn = jnp.zeros((batch, GDN_H, GDN_DP, GDN_DP), F32)
    gla_bd_gain = jnp.tile(gla_norm_g, (1, GLA_H))

    for l in range(depth):
        with_ctx_out = l < depth - 1
        (sh1, csh1), (sc1, csc1), (g1, cg1), (sh2, csh2), (sc2, csc2), (g2, cg2) = [mod_vecs(l, i) for i in range(N_MOD)]

        wl = w_in[l]
        o_gla = 3 * hy_c
        o_gdn = o_gla + 2 * gla_kw + 2 * gla_vw + 2 * GLA_RANK
        o_gdn_gate = o_gdn + 3 * gdn_w
        o_gdn_small = o_gdn_gate + gdn_w
        w_small = jnp.concatenate([wl[:, o_gdn - 2 * GLA_RANK:o_gdn], wl[:, o_gdn_small:]], axis=1)
        w_small = jnp.pad(w_small, ((0, 0), (0, LANES - w_small.shape[1])))
        in_weights = [
            wl[:, :o_gla],
            wl[:, o_gla:o_gla + 2 * gla_kw],
            wl[:, o_gla + 2 * gla_kw:o_gla + 2 * gla_kw + gla_vw],
            wl[:, o_gla + 2 * gla_kw + gla_vw:o_gla + 2 * gla_kw + 2 * gla_vw],
            _pad_heads(wl[:, o_gdn:o_gdn_gate], 3 * GDN_H, 1),
            _pad_heads(wl[:, o_gdn_gate:o_gdn_small], GDN_H, 1),
            w_small,
        ]
        in_weights = [w.astype(BF16) for w in in_weights]
        gdn_cw = _pad_heads(gdn_conv_w[l], 3 * GDN_H, 1)
        gdn_gain = _pad_heads(gdn_norm_g[l][None, :].repeat(GDN_H, 0).reshape(1, gdn_w), GDN_H, 1)
        wo = w_out[l]
        w_hy = wo[:hy_c].astype(BF16)
        w_gla = wo[hy_c:hy_c + gla_vw].astype(BF16)
        w_gdn = _pad_heads(wo[hy_c + gla_vw:], GDN_H, 0).astype(BF16)
        w1 = w_mlp1[l].astype(BF16)
        w2 = w_mlp2[l].astype(BF16)
        filt_args = (hy_f_w1[l], hy_f_b1[l], hy_f_w2[l], hy_f_b2[l], hy_f_w3[l], hy_sin_freq[l], hy_c)

        def mixer_parts(tokens, shift, scale, seg_len, tl, s0_gla, s0_gdn):
            z_hy, z_qk, z_v, z_gate, z_gdn, z_gdn_gate, z_small = _inproj(
                tokens, shift, scale, norm1_g[l], in_weights, tl)
            b_f, b_b = _gla_prep(z_small, gla_w_a2[l], gla_b_a[l], tl)
            gla_f, gla_b, gla_sf, gla_sb = _gla_scan(z_qk, z_v, b_f, b_b, s0_gla[0], s0_gla[1], tl)
            q, k, v, gc, gct = _gdn_prep(z_gdn, gdn_cw, z_small, gdn_a_log[l], gdn_dt_bias[l], gdn_d, seg_len, tl)
            gdn_f, gdn_b, gdn_sf, gdn_sb = _gdn_scan(q, k, v, gc, gct, s0_gdn[0], s0_gdn[1], tl)
            u = _hy_short_conv(z_hy, hy_conv_w[l], hy_conv_b[l], seg_len, tl)
            return u, (gla_f, gla_b, z_gate), (gdn_f, gdn_b, z_gdn_gate), (gla_sf, gla_sb), (gdn_sf, gdn_sb)

        def finish(tokens, gate1, hy, gla, gdn, tm):
            return _outproj(tokens, gate1, hy, *gla, *gdn, gla_bd_gain[l][None, :], gdn_gain, w_hy, w_gla, w_gdn,
                            gla_dv, gdn_d, tm)

        tl_c = ctx_len
        u_c, gla_c, gdn_c, gla_s, gdn_s = mixer_parts(ctx, csh1, csc1, ctx_len, tl_c,
                                                      (zeros_gla, zeros_gla), (zeros_gdn, zeros_gdn))
        tl = 256
        u_l, gla_l, gdn_l, _, _ = mixer_parts(x, sh1, sc1, GRID_W, tl, gla_s, gdn_s)
        spec = _spectrum_two_stage(_hyena_filter_spectrum(seq, *filt_args), n1, n2)
        nb = hy_c // LANES
        y1 = _fftconv_gate(u_l, 0, u_l, nb, spec[0][0], spec[0][1], hy_d[l, 0], fft_tabs, n1, n2)
        hy_l = _fftconv_gate(y1, 0, u_l, 2 * nb, spec[1][0], spec[1][1], hy_d[l, 1], fft_tabs, n1, n2)
        x = finish(x, g1, hy_l, gla_l, gdn_l, 512)
        x = _mlp(x, sh2, sc2, g2, norm2_g[l], w1, w2, final_norm_g, not with_ctx_out, 1024, 512)

        if with_ctx_out:
            coef = _spectrum_dense(_hyena_filter_spectrum(ctx_len, *filt_args))
            y1c = _dftconv_gate(u_c, 0, u_c, 1, *coef[0], hy_d[l, 0], dft_tabs)
            hy_c_out = _dftconv_gate(y1c, 0, u_c, 2, *coef[1], hy_d[l, 1], dft_tabs)
            ctx = finish(ctx, cg1, hy_c_out, gla_c, gdn_c, ctx_len)
            ctx = _mlp(ctx, csh2, csc2, cg2, norm2_g[l], w1, w2, final_norm_g, False, ctx_len, 512)
    return x
```

```python
import functools
import math

import numpy as np
import jax
import jax.numpy as jnp
from jax import lax
from jax.experimental import pallas as pl
from jax.experimental.pallas import tpu as pltpu

F32 = jnp.float32
BF16 = jnp.bfloat16
HIGHEST = lax.Precision.HIGHEST

NORM_EPS = 1e-6
N_MOD = 6
GRID_W = 64
CHUNK = 64
LANES = 128

HY_ORDER = 2
HY_SHORT = 3
HY_EMB = 33
HY_DECAY_TARGET = 1e-2
HY_FAST_PCT = 0.3
HY_SLOW_PCT = 1.5

GLA_H = 4
GLA_RANK = 16
GLA_TAU = 16.0
GDN_H = 4
GDN_DP = LANES
FFT_UNROLL = 8

VMEM_LIMIT = 56 * 1024 * 1024


def _cparams(n_grid):
    return pltpu.CompilerParams(dimension_semantics=("arbitrary",) * n_grid, vmem_limit_bytes=VMEM_LIMIT)


def _bdot(a, b):
    return jnp.dot(a.astype(BF16), b.astype(BF16), preferred_element_type=F32)


def _bdot_nt(a, b):
    return lax.dot_general(a.astype(BF16), b.astype(BF16), (((1,), (1,)), ((), ())), preferred_element_type=F32)


def _bdot_tn(a, b):
    return lax.dot_general(a.astype(BF16), b.astype(BF16), (((0,), (0,)), ((), ())), preferred_element_type=F32)


def _const_spec(shape):
    return pl.BlockSpec(shape, lambda *_: (0,) * len(shape))


def _mod_kernel(c_ref, w_ref, b_ref, o_ref):
    c = c_ref[...]
    o_ref[0] = _bdot(c * jax.nn.sigmoid(c), w_ref[0]) + b_ref[0]


def _modulation(cc, w_mod, b_mod):
    depth, d, n = w_mod.shape
    tn = n // 4
    return pl.pallas_call(
        _mod_kernel,
        grid=(depth, n // tn),
        in_specs=[pl.BlockSpec(cc.shape, lambda l, j: (0, 0)),
                  pl.BlockSpec((1, d, tn), lambda l, j: (l, 0, j)),
                  pl.BlockSpec((1, 1, tn), lambda l, j: (l, 0, j))],
        out_specs=pl.BlockSpec((1, cc.shape[0], tn), lambda l, j: (l, 0, j)),
        out_shape=jax.ShapeDtypeStruct((depth, cc.shape[0], n), F32),
        compiler_params=_cparams(2),
        name="modulation",
    )(cc, w_mod.astype(BF16), b_mod.reshape(depth, 1, n))


def _rms_modulate(x, g, shift, scale):
    y = x * lax.rsqrt(jnp.mean(x * x, axis=-1, keepdims=True) + NORM_EPS) * g
    return y * (1.0 + scale) + shift


def _inproj_kernel(x_ref, sh_ref, sc_ref, g_ref, *refs):
    n_out = len(refs) // 2
    hb = _rms_modulate(x_ref[0], g_ref[...], sh_ref[0], sc_ref[0]).astype(BF16)
    for w_ref, o_ref in zip(refs[:n_out], refs[n_out:]):
        o_ref[0] = jnp.dot(hb, w_ref[...], preferred_element_type=F32)


def _inproj(x, shift, scale, g, weights, tm):
    b, l, d = x.shape
    vec = pl.BlockSpec((1, 1, d), lambda i, j: (i, 0, 0))
    return pl.pallas_call(
        _inproj_kernel,
        grid=(b, l // tm),
        in_specs=[pl.BlockSpec((1, tm, d), lambda i, j: (i, j, 0)), vec, vec, _const_spec((1, d))]
        + [_const_spec(w.shape) for w in weights],
        out_specs=[pl.BlockSpec((1, tm, w.shape[1]), lambda i, j: (i, j, 0)) for w in weights],
        out_shape=[jax.ShapeDtypeStruct((b, l, w.shape[1]), F32) for w in weights],
        compiler_params=_cparams(2),
        name="inproj",
    )(x, shift, scale, g.reshape(1, d), *weights)


def _seg_conv(z, w, seg_len):
    rows, width = z.shape
    taps = w.shape[0]
    half = taps // 2
    pos = lax.broadcasted_iota(jnp.int32, (rows, width), 0) % seg_len
    acc = z * w[half:half + 1, :]
    for k in range(taps):
        d = k - half
        if d == 0:
            continue
        shifted = pltpu.roll(z, (-d) % rows, 0)
        valid = (pos >= -d) if d < 0 else (pos < seg_len - d)
        acc = acc + jnp.where(valid, shifted, 0.0) * w[k:k + 1, :]
    return acc


def _hy_conv_kernel(z_ref, w_ref, b_ref, o_ref, *, seg_len):
    o_ref[0] = _seg_conv(z_ref[0], w_ref[...], seg_len) + b_ref[...]


def _hy_short_conv(z, w, bias, seg_len, tl):
    b, l, width = z.shape
    return pl.pallas_call(
        functools.partial(_hy_conv_kernel, seg_len=seg_len),
        grid=(b, l // tl),
        in_specs=[pl.BlockSpec((1, tl, width), lambda i, j: (i, j, 0)), _const_spec(w.shape),
                  _const_spec((1, width))],
        out_specs=pl.BlockSpec((1, tl, width), lambda i, j: (i, j, 0)),
        out_shape=jax.ShapeDtypeStruct(z.shape, F32),
        compiler_params=_cparams(2),
        name="hy_short_conv",
    )(z, w, bias.reshape(1, width))


def _fft_tables(n1, n2):
    n = n1 * n2
    s1 = np.arange(n1 // 2)[None, :, None]
    s2 = np.arange(n2)[:, None, None]
    k1 = np.arange(n1)[None, None, :]
    theta = 2.0 * np.pi * ((k1 * (n2 * s1 + s2)) % n) / n
    tt = np.concatenate([np.cos(theta), -np.sin(theta)], axis=-1)
    phi = 2.0 * np.pi * ((np.arange(n2)[:, None] * np.arange(n2)[None, :]) % n2) / n2
    c2, s2m = np.cos(phi), np.sin(phi)
    fwd = np.block([[c2, s2m], [-s2m, c2]])
    inv = np.block([[c2, -s2m], [s2m, c2]])
    return jnp.asarray(tt, BF16), jnp.asarray(fwd, BF16), jnp.asarray(inv, BF16)


def _fftconv_kernel(u_ref, gate_ref, hr_ref, hi_ref, d_ref, tt_ref, fwd_ref, inv_ref, o_ref, spec_ref, *, n1, n2):
    n1h = n1 // 2

    def stage_a(s2, carry):
        slab = u_ref[0, pl.ds(s2, n1h, stride=n2), :]
        res = _bdot_tn(tt_ref[s2], slab)
        spec_ref[pl.ds(s2, n1, stride=2 * n2), :] = res[:n1]
        spec_ref[pl.ds(n2 + s2, n1, stride=2 * n2), :] = res[n1:]
        return carry

    lax.fori_loop(0, n2, stage_a, 0, unroll=FFT_UNROLL)

    def stage_b(k1, carry):
        rows = pl.ds(pl.multiple_of(k1 * 2 * n2, 2 * n2), 2 * n2)
        spec = jnp.dot(fwd_ref[...], spec_ref[rows, :].astype(BF16), preferred_element_type=F32)
        br, bi = spec[:n2], spec[n2:]
        hrows = pl.ds(pl.multiple_of(k1 * n2, n2), n2)
        hr, hi = hr_ref[hrows, :], hi_ref[hrows, :]
        y = jnp.concatenate([br * hr - bi * hi, br * hi + bi * hr], axis=0)
        spec_ref[rows, :] = jnp.dot(inv_ref[...], y.astype(BF16), preferred_element_type=F32)
        return carry

    lax.fori_loop(0, n1, stage_b, 0, unroll=min(FFT_UNROLL, n1))

    def stage_a_inv(s2, carry):
        g = jnp.concatenate([spec_ref[pl.ds(s2, n1, stride=2 * n2), :],
                             spec_ref[pl.ds(n2 + s2, n1, stride=2 * n2), :]], axis=0)
        conv = jnp.dot(tt_ref[s2], g.astype(BF16), preferred_element_type=F32)
        rows = pl.ds(s2, n1h, stride=n2)
        u = u_ref[0, rows, :]
        o_ref[0, rows, :] = gate_ref[0, rows, :] * (conv + d_ref[...] * u)
        return carry

    lax.fori_loop(0, n2, stage_a_inv, 0, unroll=FFT_UNROLL)


def _fftconv_gate(u_arr, u_blk, gate_arr, gate_blk, hr, hi, d, tables, n1, n2):
    tt, fwd, inv = tables
    b, l, _ = u_arr.shape
    n, c = hr.shape
    ncb = c // LANES
    seq = lambda off: pl.BlockSpec((1, l, LANES), lambda j, i: (i, 0, off + j))
    chan = lambda rows: pl.BlockSpec((rows, LANES), lambda j, i: (0, j))
    return pl.pallas_call(
        functools.partial(_fftconv_kernel, n1=n1, n2=n2),
        grid=(ncb, b),
        in_specs=[seq(u_blk), seq(gate_blk), chan(n), chan(n), chan(1),
                  _const_spec(tt.shape), _const_spec(fwd.shape), _const_spec(inv.shape)],
        out_specs=pl.BlockSpec((1, l, LANES), lambda j, i: (i, 0, j)),
        out_shape=jax.ShapeDtypeStruct((b, l, c), F32),
        scratch_shapes=[pltpu.VMEM((2 * n, LANES), F32)],
        compiler_params=_cparams(2),
        name="hy_fftconv",
    )(u_arr, gate_arr, hr, hi, d.reshape(1, c), tt, fwd, inv)


def _dft_tables(l):
    n = 2 * l
    ang = 2.0 * np.pi * ((np.arange(l)[:, None] * np.arange(l)[None, :]) % n) / n
    cf = np.cos(ang)
    sf = -np.sin(ang)
    sf[0, :] = (-1.0) ** np.arange(l)
    fwd = np.concatenate([cf, sf], axis=0)
    return jnp.asarray(fwd, BF16), jnp.asarray(fwd.T, BF16)


def _dftconv_kernel(u_ref, gate_ref, p_ref, q_ref, r_ref, d_ref, fwd_ref, inv_ref, o_ref):
    u = u_ref[0]
    l = u.shape[0]
    spec = jnp.dot(fwd_ref[...], u.astype(BF16), preferred_element_type=F32)
    xr, xi = spec[:l], spec[l:]
    q = q_ref[...]
    y = jnp.concatenate([xr * p_ref[...] - xi * q, xr * q + xi * r_ref[...]], axis=0)
    conv = jnp.dot(inv_ref[...], y.astype(BF16), preferred_element_type=F32)
    o_ref[0] = gate_ref[0] * (conv + d_ref[...] * u)


def _dftconv_gate(u_arr, u_blk, gate_arr, gate_blk, p, q, r, d, tables):
    fwd, inv = tables
    b, l, _ = u_arr.shape
    c = p.shape[1]
    seq = lambda off: pl.BlockSpec((1, l, c), lambda i: (i, 0, off))
    return pl.pallas_call(
        _dftconv_kernel,
        grid=(b,),
        in_specs=[seq(u_blk), seq(gate_blk), _const_spec(p.shape), _const_spec(q.shape), _const_spec(r.shape),
                  _const_spec((1, c)), _const_spec(fwd.shape), _const_spec(inv.shape)],
        out_specs=pl.BlockSpec((1, l, c), lambda i: (i, 0, 0)),
        out_shape=jax.ShapeDtypeStruct((b, l, c), F32),
        compiler_params=_cparams(1),
        name="hy_dftconv",
    )(u_arr, gate_arr, p, q, r, d.reshape(1, c), fwd, inv)


def _chunk_cumsum_mats(tl):
    r = np.arange(tl)
    same = (r[:, None] // CHUNK) == (r[None, :] // CHUNK)
    lower = same & (r[None, :] <= r[:, None])
    upper = same & (r[None, :] >= r[:, None])
    return jnp.asarray(lower, F32), jnp.asarray(upper, F32)


def _gla_prep_kernel(a_ref, w_ref, b_ref, lo_ref, up_ref, bf_ref, bb_ref):
    a = a_ref[0]
    for d, (tri_ref, o_ref) in enumerate(((lo_ref, bf_ref), (up_ref, bb_ref))):
        pre = jnp.dot(a[:, d * GLA_RANK:(d + 1) * GLA_RANK], w_ref[d], precision=HIGHEST,
                      preferred_element_type=F32) + b_ref[d]
        log_a = jax.nn.log_sigmoid(pre) * (1.0 / GLA_TAU)
        o_ref[0] = jnp.dot(tri_ref[...], log_a, precision=HIGHEST, preferred_element_type=F32)


def _gla_prep(small, w_a2, b_a, tl):
    b, l, sw = small.shape
    kw = w_a2.shape[-1]
    lower, upper = _chunk_cumsum_mats(tl)
    out = pl.BlockSpec((1, tl, kw), lambda i, j: (i, j, 0))
    return pl.pallas_call(
        _gla_prep_kernel,
        grid=(b, l // tl),
        in_specs=[pl.BlockSpec((1, tl, sw), lambda i, j: (i, j, 0)), _const_spec(w_a2.shape),
                  _const_spec((2, 1, kw)), _const_spec((tl, tl)), _const_spec((tl, tl))],
        out_specs=[out, out],
        out_shape=[jax.ShapeDtypeStruct((b, l, kw), F32)] * 2,
        compiler_params=_cparams(2),
        name="gla_prep",
    )(small, w_a2, b_a.reshape(2, 1, kw), lower, upper)


def _gla_scan_kernel(qkf_ref, vf_ref, bf_ref, qkb_ref, vb_ref, bb_ref, s0f_ref, s0b_ref, hm_ref, vm_ref, bd_ref,
                     of_ref, ob_ref, sff_ref, sfb_ref, stf_ref, stb_ref, *, n_chunks, kw, q_scale):
    j = pl.program_id(1)

    @pl.when(j == 0)
    def _():
        stf_ref[...] = s0f_ref[0]
        stb_ref[...] = s0b_ref[0]

    ri = lax.broadcasted_iota(jnp.int32, (CHUNK, CHUNK), 0)
    ci = lax.broadcasted_iota(jnp.int32, (CHUNK, CHUNK), 1)
    hm = hm_ref[...]
    vm = vm_ref[...]
    bd = bd_ref[...]

    def one_chunk(qk_ref, v_ref, b_ref, o_ref, st_ref, c, keep, last, mid):
        rows = pl.ds(pl.multiple_of(c * CHUNK, CHUNK), CHUNK)
        qk = qk_ref[0, rows, :]
        q, k = qk[:, :kw] * q_scale, qk[:, kw:]
        v = v_ref[0, rows, :]
        bc = b_ref[0, rows, :]
        st = st_ref[...]
        b_mid = bc[mid:mid + 1, :]
        b_last = bc[last:last + 1, :]
        qt = q * jnp.exp(bc - b_mid)
        kt = (k * jnp.exp(b_mid - bc)).astype(BF16)
        o = _bdot_nt(q * jnp.exp(bc), st)
        for h in range(GLA_H):
            attn = jnp.where(keep, _bdot_nt(qt * hm[h:h + 1, :], kt), 0.0)
            o = o + _bdot(attn, v * vm[h:h + 1, :])
        o_ref[0, rows, :] = o
        kd = k * jnp.exp(b_last - bc)
        st_ref[...] = st * jnp.exp(b_last) + _bdot_tn(v, kd) * bd

    def body(c, carry):
        one_chunk(qkf_ref, vf_ref, bf_ref, of_ref, stf_ref, c, ri >= ci, CHUNK - 1, CHUNK // 2 - 1)
        one_chunk(qkb_ref, vb_ref, bb_ref, ob_ref, stb_ref, n_chunks - 1 - c, ci >= ri, 0, CHUNK // 2)
        return carry

    lax.fori_loop(0, n_chunks, body, 0)

    @pl.when(j == pl.num_programs(1) - 1)
    def _():
        sff_ref[0] = stf_ref[...]
        sfb_ref[0] = stb_ref[...]


def _gla_scan(qk, v, b_f, b_b, s0_f, s0_b, tl):
    b, l, kw2 = qk.shape
    kw, vw = kw2 // 2, v.shape[-1]
    dk, dv = kw // GLA_H, vw // GLA_H
    nblk = l // tl
    heads_k = np.arange(kw) // dk
    heads_v = np.arange(vw) // dv
    hm = jnp.asarray(np.arange(8)[:, None] == heads_k[None, :], F32)
    vm = jnp.asarray(np.arange(8)[:, None] == heads_v[None, :], F32)
    bd = jnp.asarray(heads_v[:, None] == heads_k[None, :], F32)
    fwd = lambda w: pl.BlockSpec((1, tl, w), lambda i, j: (i, j, 0))
    bwd = lambda w: pl.BlockSpec((1, tl, w), lambda i, j: (i, nblk - 1 - j, 0))
    state = pl.BlockSpec((1, vw, kw), lambda i, j: (i, 0, 0))
    return pl.pallas_call(
        functools.partial(_gla_scan_kernel, n_chunks=tl // CHUNK, kw=kw, q_scale=dk ** -0.5),
        grid=(b, nblk),
        in_specs=[fwd(kw2), fwd(vw), fwd(kw), bwd(kw2), bwd(vw), bwd(kw), state, state,
                  _const_spec(hm.shape), _const_spec(vm.shape), _const_spec(bd.shape)],
        out_specs=[fwd(vw), bwd(vw), state, state],
        out_shape=[jax.ShapeDtypeStruct((b, l, vw), F32)] * 2 + [jax.ShapeDtypeStruct((b, vw, kw), F32)] * 2,
        scratch_shapes=[pltpu.VMEM((vw, kw), F32)] * 2,
        compiler_params=_cparams(2),
        name="gla_scan",
    )(qk, v, b_f, qk, v, b_b, s0_f, s0_b, hm, vm, bd)


GDN_GF, GDN_GB, GDN_BF, GDN_BB = 32, 36, 40, 44

def _gdn_prep_kernel(z_ref, w_ref, small_ref, nea_ref, dtb_ref, lo_ref, up_ref,
                     q_ref, k_ref, v_ref, gc_ref, *, seg_len, q_scale):
    width = q_ref.shape[-1]
    y = _seg_conv(z_ref[0], w_ref[...], seg_len)
    y = y * jax.nn.sigmoid(y)
    for part, (o_ref, scale) in enumerate(((q_ref, q_scale), (k_ref, 1.0))):
        for h in range(GDN_H):
            cols = slice(part * width + h * GDN_DP, part * width + (h + 1) * GDN_DP)
            t = y[:, cols]
            o_ref[0, :, h * GDN_DP:(h + 1) * GDN_DP] = t * (
                lax.rsqrt(jnp.sum(t * t, axis=-1, keepdims=True) + NORM_EPS) * scale)
    v_ref[0] = y[:, 2 * width:]
    small = small_ref[0]
    log_decay = nea_ref[...] * jax.nn.softplus(small + dtb_ref[...])
    cum_f = jnp.dot(lo_ref[...], log_decay, precision=HIGHEST, preferred_element_type=F32)
    cum_b = jnp.dot(up_ref[...], log_decay, precision=HIGHEST, preferred_element_type=F32)
    lane = lax.broadcasted_iota(jnp.int32, small.shape, 1)
    gc_ref[0] = jnp.where(lane < GDN_GB, cum_f, jnp.where(lane < GDN_BF, cum_b, jax.nn.sigmoid(small)))


def _gdn_prep(z, conv_w, small, a_log, dt_bias, head_dim, seg_len, tl):
    b, l, w3 = z.shape
    width = w3 // 3
    nea = jnp.zeros((1, LANES), F32).at[0, GDN_GF:GDN_BF].set(-jnp.exp(a_log.reshape(-1)))
    dtb = jnp.zeros((1, LANES), F32).at[0, GDN_GF:GDN_BF].set(dt_bias.reshape(-1))
    lower, upper = _chunk_cumsum_mats(tl)
    tok = lambda w: pl.BlockSpec((1, tl, w), lambda i, j: (i, j, 0))
    return pl.pallas_call(
        functools.partial(_gdn_prep_kernel, seg_len=seg_len, q_scale=head_dim ** -0.5),
        grid=(b, l // tl),
        in_specs=[tok(w3), _const_spec(conv_w.shape), tok(LANES), _const_spec((1, LANES)), _const_spec((1, LANES)),
                  _const_spec((tl, tl)), _const_spec((tl, tl))],
        out_specs=[tok(width), tok(width), tok(width), tok(LANES)],
        out_shape=[jax.ShapeDtypeStruct((b, l, width), F32)] * 3 + [jax.ShapeDtypeStruct((b, l, LANES), F32)],
        compiler_params=_cparams(2),
        name="gdn_prep",
    )(z, conv_w, small, nea, dtb, lower, upper)


def _unit_tri_inverses(mats, eye, m16, m32, m64):
    diag = [a * m16 for a in mats]
    inv = [eye + a for a in diag]
    pw = diag
    for _ in range(3):
        pw = [_bdot(p, p) for p in pw]
        inv = [t + _bdot(t, p) for t, p in zip(inv, pw)]
    for mask in (m32, m64):
        mid = [_bdot(t, a * mask) for t, a in zip(inv, mats)]
        inv = [t + _bdot(m, t) for t, m in zip(inv, mid)]
    return inv


def _gdn_scan_kernel(qf_ref, kf_ref, vf_ref, gcf_ref, qb_ref, kb_ref, vb_ref, gcb_ref,
                     s0f_ref, s0b_ref, of_ref, ob_ref, sff_ref, sfb_ref, sf_ref, sb_ref, *, tl):
    j = pl.program_id(1)

    @pl.when(j == 0)
    def _():
        sf_ref[...] = s0f_ref[0]
        sb_ref[...] = s0b_ref[0]

    nck = tl // CHUNK
    ri = lax.broadcasted_iota(jnp.int32, (tl, tl), 0)
    ci = lax.broadcasted_iota(jnp.int32, (tl, tl), 1)
    same = lambda n: (ri // n) == (ci // n)
    chunk = same(CHUNK)
    eye = (ri == ci).astype(F32)
    m16 = same(16).astype(F32)
    m32 = (same(32) & ~same(16)).astype(F32)
    m64 = (chunk & ~same(32)).astype(F32)
    dirs = ((qf_ref, kf_ref, vf_ref, gcf_ref, of_ref, sf_ref, ri > ci, ri >= ci, CHUNK - 1, GDN_GF, GDN_BF, False),
            (qb_ref, kb_ref, vb_ref, gcb_ref, ob_ref, sb_ref, ci > ri, ci >= ri, 0, GDN_GB, GDN_BB, True))

    qs, ks, vs, gcums, betas, stricts, incls, glasts, outs = [], [], [], [], [], [], [], [], []
    for q_ref, k_ref, v_ref, gc_ref, o_ref, s_ref, strict, incl, last, g_col, b_col, rev in dirs:
        gc = gc_ref[0]
        gct = gc.T
        for h in range(GDN_H):
            cols = slice(h * GDN_DP, (h + 1) * GDN_DP)
            gcum = gc[:, g_col + h:g_col + h + 1]
            decay = jnp.exp(jnp.minimum(gcum - gct[g_col + h:g_col + h + 1, :], 0.0))
            qs.append(q_ref[0, :, cols])
            ks.append(k_ref[0, :, cols])
            vs.append(v_ref[0, :, cols])
            gcums.append(gcum)
            betas.append(gc[:, b_col + h:b_col + h + 1])
            stricts.append(jnp.where(chunk & strict, decay, 0.0))
            incls.append(jnp.where(chunk & incl, decay, 0.0))
            glasts.append(jnp.concatenate(
                [jnp.broadcast_to(gcum[c * CHUNK + last:c * CHUNK + last + 1, :], (CHUNK, 1)) for c in range(nck)],
                axis=0))
            outs.append((o_ref, s_ref, h, cols, rev))
    n_pairs = len(qs)
    e_cols = [jnp.exp(g) for g in gcums]
    kbs = [k * b for k, b in zip(ks, betas)]
    mats = [-_bdot_nt(kb, k) * d for kb, k, d in zip(kbs, ks, stricts)]
    invs = _unit_tri_inverses(mats, eye, m16, m32, m64)
    wus = [_bdot(t, jnp.concatenate([kb * e, v * b], axis=1))
           for t, kb, e, v, b in zip(invs, kbs, e_cols, vs, betas)]
    attns = [_bdot_nt(q, k) * d for q, k, d in zip(qs, ks, incls)]
    awus = [_bdot(a, wu) for a, wu in zip(attns, wus)]
    q_effs = [(q * e - awu[:, :GDN_DP]).astype(BF16) for q, e, awu in zip(qs, e_cols, awus)]
    ws = [wu[:, :GDN_DP].astype(BF16) for wu in wus]
    us = [wu[:, GDN_DP:] for wu in wus]
    o_intra = [awu[:, GDN_DP:] for awu in awus]
    kds = [(k * jnp.exp(gl - g)).astype(BF16) for k, gl, g in zip(ks, glasts, gcums)]
    decs = [jnp.exp(gl) for gl in glasts]

    states = [s_ref[h] for (_, s_ref, h, _, _) in outs]
    for c in range(nck):
        rows = [slice((nck - 1 - c if rev else c) * CHUNK, (nck - c if rev else c + 1) * CHUNK)
                for (_, _, _, _, rev) in outs]
        prods = [jnp.dot(jnp.concatenate([q_effs[i][rows[i]], ws[i][rows[i]]], axis=0), states[i].astype(BF16),
                         preferred_element_type=F32) for i in range(n_pairs)]
        for i, (o_ref, _, _, cols, _) in enumerate(outs):
            o_ref[0, rows[i], cols] = o_intra[i][rows[i]] + prods[i][:CHUNK]
        v_news = [us[i][rows[i]] - prods[i][CHUNK:] for i in range(n_pairs)]
        states = [decs[i][rows[i]][0:1] * states[i] + _bdot_tn(kds[i][rows[i]], v_news[i]) for i in range(n_pairs)]
    for i, (_, s_ref, h, _, _) in enumerate(outs):
        s_ref[h] = states[i]

    @pl.when(j == pl.num_programs(1) - 1)
    def _():
        sff_ref[0] = sf_ref[...]
        sfb_ref[0] = sb_ref[...]


def _gdn_scan(q, k, v, gc, s0_f, s0_b, tl):
    b, l, width = q.shape
    nblk = l // tl
    fwd = lambda w: pl.BlockSpec((1, tl, w), lambda i, j: (i, j, 0))
    bwd = lambda w: pl.BlockSpec((1, tl, w), lambda i, j: (i, nblk - 1 - j, 0))
    state = pl.BlockSpec((1, GDN_H, GDN_DP, GDN_DP), lambda i, j: (i, 0, 0, 0))
    st_shape = jax.ShapeDtypeStruct((b, GDN_H, GDN_DP, GDN_DP), F32)
    return pl.pallas_call(
        functools.partial(_gdn_scan_kernel, tl=tl),
        grid=(b, nblk),
        in_specs=[fwd(width), fwd(width), fwd(width), fwd(LANES),
                  bwd(width), bwd(width), bwd(width), bwd(LANES), state, state],
        out_specs=[fwd(width), bwd(width), state, state],
        out_shape=[jax.ShapeDtypeStruct((b, l, width), F32)] * 2 + [st_shape] * 2,
        scratch_shapes=[pltpu.VMEM((GDN_H, GDN_DP, GDN_DP), F32)] * 2,
        compiler_params=_cparams(2),
        name="gdn_scan",
    )(q, k, v, gc, q, k, v, gc, s0_f, s0_b)


def _gated_head_norm(o, gate, g_norm, ones_bd, inv_d):
    sq = o * o
    hi = sq.astype(BF16)
    lo = (sq - hi.astype(F32)).astype(BF16)
    ms = (jnp.dot(hi, ones_bd, preferred_element_type=F32) + jnp.dot(lo, ones_bd, preferred_element_type=F32)) * inv_d
    return o * lax.rsqrt(ms + NORM_EPS) * g_norm * (gate * jax.nn.sigmoid(gate))


def _outproj_kernel(x_ref, g1_ref, hy_ref, glf_ref, glb_ref, glg_ref, gdf_ref, gdb_ref, gdg_ref,
                    gln_ref, gdn_ref, glm_ref, gdm_ref, why_ref, wgl_ref, wgd_ref, o_ref, *, gla_dv, gdn_d):
    y_gla = _gated_head_norm(glf_ref[0] + glb_ref[0], glg_ref[0], gln_ref[...], glm_ref[...], 1.0 / gla_dv)
    y_gdn = _gated_head_norm(gdf_ref[0] + gdb_ref[0], gdg_ref[0], gdn_ref[...], gdm_ref[...], 1.0 / gdn_d)
    acc = _bdot(hy_ref[0], why_ref[...]) + _bdot(y_gla, wgl_ref[...]) + _bdot(y_gdn, wgd_ref[...])
    o_ref[0] = x_ref[0] + g1_ref[0] * acc


def _outproj(x, g1, hy, gla_f, gla_b, gla_gate, gdn_f, gdn_b, gdn_gate, gla_g, gdn_g, w_hy, w_gla, w_gdn,
             gla_dv, gdn_d, tm):
    b, l, d = x.shape
    hyw, glw, gdw = hy.shape[-1], gla_f.shape[-1], gdn_f.shape[-1]
    gl_heads = np.arange(glw) // gla_dv
    gd_heads = np.arange(gdw) // GDN_DP
    gl_m = jnp.asarray(gl_heads[:, None] == gl_heads[None, :], BF16)
    gd_m = jnp.asarray(gd_heads[:, None] == gd_heads[None, :], BF16)
    tok = lambda w: pl.BlockSpec((1, tm, w), lambda i, j: (i, j, 0))
    return pl.pallas_call(
        functools.partial(_outproj_kernel, gla_dv=gla_dv, gdn_d=gdn_d),
        grid=(b, l // tm),
        in_specs=[tok(d), pl.BlockSpec((1, 1, d), lambda i, j: (i, 0, 0)), tok(hyw), tok(glw), tok(glw), tok(glw),
                  tok(gdw), tok(gdw), tok(gdw), _const_spec((1, glw)), _const_spec((1, gdw)),
                  _const_spec(gl_m.shape), _const_spec(gd_m.shape),
                  _const_spec(w_hy.shape), _const_spec(w_gla.shape), _const_spec(w_gdn.shape)],
        out_specs=tok(d),
        out_shape=jax.ShapeDtypeStruct(x.shape, F32),
        compiler_params=_cparams(2),
        name="outproj",
    )(x, g1, hy, gla_f, gla_b, gla_gate, gdn_f, gdn_b, gdn_gate, gla_g, gdn_g, gl_m, gd_m, w_hy, w_gla, w_gdn)


def _mlp_kernel(x_ref, sh_ref, sc_ref, g2_ref, ng_ref, w1_ref, w2_ref, fg_ref, o_ref, hn_ref, acc_ref, *, final_norm):
    kk = pl.program_id(2)

    @pl.when(kk == 0)
    def _():
        hn_ref[...] = _rms_modulate(x_ref[0], ng_ref[...], sh_ref[0], sc_ref[0]).astype(BF16)
        acc_ref[...] = jnp.zeros_like(acc_ref)

    hid = jnp.maximum(jnp.dot(hn_ref[...], w1_ref[...], preferred_element_type=F32), 0.0)
    acc_ref[...] += _bdot(hid * hid, w2_ref[...])

    @pl.when(kk == pl.num_programs(2) - 1)
    def _():
        y = x_ref[0] + g2_ref[0] * acc_ref[...]
        if final_norm:
            y = y * lax.rsqrt(jnp.mean(y * y, axis=-1, keepdims=True) + NORM_EPS) * fg_ref[...]
        o_ref[0] = y


def _mlp(x, shift, scale, gate, norm_g, w1, w2, final_g, final_norm, tm, th):
    b, l, d = x.shape
    dff = w1.shape[1]
    vec = pl.BlockSpec((1, 1, d), lambda i, j, k: (i, 0, 0))
    row = pl.BlockSpec((1, d), lambda i, j, k: (0, 0))
    return pl.pallas_call(
        functools.partial(_mlp_kernel, final_norm=final_norm),
        grid=(b, l // tm, dff // th),
        in_specs=[pl.BlockSpec((1, tm, d), lambda i, j, k: (i, j, 0)), vec, vec, vec, row,
                  pl.BlockSpec((d, th), lambda i, j, k: (0, k)), pl.BlockSpec((th, d), lambda i, j, k: (k, 0)), row],
        out_specs=pl.BlockSpec((1, tm, d), lambda i, j, k: (i, j, 0)),
        out_shape=jax.ShapeDtypeStruct(x.shape, F32),
        scratch_shapes=[pltpu.VMEM((tm, d), BF16), pltpu.VMEM((tm, d), F32)],
        compiler_params=_cparams(3),
        name="mlp",
    )(x, shift, scale, gate, norm_g.reshape(1, d), w1, w2, final_g.reshape(1, d))


def _hyena_filter_spectrum(length, w1, b1, w2, b2, w3, sin_freq, hy_c):
    t = jnp.linspace(0.0, 1.0, length, dtype=F32)[:, None]
    bands = (HY_EMB - 1) // 2
    f = jnp.linspace(1e-4, bands - 1, bands, dtype=F32)
    w = 2 * math.pi * jnp.arange(length, dtype=F32) / length
    ang = w[:, None] * f[None, :]
    z = jnp.concatenate([t, jnp.cos(ang), -jnp.sin(ang)], axis=-1)
    h = jnp.sin(sin_freq[0] * (z @ w1 + b1))
    h = jnp.sin(sin_freq[1] * (h @ w2 + b2))
    h = (h @ w3).reshape(length, HY_ORDER, 2, hy_c)
    deltas = jnp.abs(jnp.linspace(math.log(HY_DECAY_TARGET) / HY_SLOW_PCT,
                                  math.log(HY_DECAY_TARGET) / HY_FAST_PCT, hy_c, dtype=F32))
    h = h * jnp.exp(-t * deltas[None, :])[:, None, None, :]
    h_fwd, h_bwd = h[:, :, 0], h[:, :, 1]
    filt = jnp.concatenate([h_fwd, jnp.zeros_like(h_fwd[:1]), h_bwd[:0:-1]], axis=0)
    return jnp.fft.rfft(filt, axis=0)


def _spectrum_two_stage(spec, n1, n2):
    n = n1 * n2
    full = jnp.concatenate([spec, jnp.conj(spec[-2:0:-1])], axis=0) / n
    full = full.reshape(n2, n1, *full.shape[1:]).swapaxes(0, 1).reshape(n, *full.shape[1:])
    return [(jnp.real(full[:, o]).astype(F32), jnp.imag(full[:, o]).astype(F32)) for o in range(HY_ORDER)]


def _spectrum_dense(spec):
    l = spec.shape[0] - 1
    n = 2 * l
    re, im = jnp.real(spec).astype(F32), jnp.imag(spec).astype(F32)
    out = []
    for o in range(HY_ORDER):
        p = (2.0 / n) * re[:l, o]
        q = (2.0 / n) * im[:l, o]
        r = p
        p = p.at[0].set(re[0, o] / n)
        q = q.at[0].set(0.0)
        r = r.at[0].set(re[l, o] / n)
        out.append((p, q, r))
    return out


def _pad_heads(w, n_heads, axis):
    shape = w.shape
    d = shape[axis] // n_heads
    w = w.reshape(shape[:axis] + (n_heads, d) + shape[axis + 1:])
    pad = [(0, 0)] * w.ndim
    pad[axis + 1] = (0, GDN_DP - d)
    w = jnp.pad(w, pad)
    return w.reshape(shape[:axis] + (n_heads * GDN_DP,) + shape[axis + 1:])


def kernel(x, c, ctx, c_ctx, norm1_g, norm2_g, w_mod, b_mod, w_in, w_out, hy_conv_w, hy_conv_b, hy_f_w1, hy_f_b1, hy_f_w2, hy_f_b2, hy_f_w3, hy_sin_freq, hy_d, gla_w_a2, gla_b_a, gla_norm_g, gdn_conv_w, gdn_a_log, gdn_dt_bias, gdn_norm_g, w_mlp1, w_mlp2, final_norm_g):
    batch, seq, d_model = x.shape
    ctx_len = ctx.shape[1]
    depth = w_mod.shape[0]
    hy_c = hy_d.shape[-1]
    gla_kw = gla_w_a2.shape[-1]
    gla_dv = gla_norm_g.shape[-1]
    gla_vw = GLA_H * gla_dv
    gdn_d = gdn_norm_g.shape[-1]
    gdn_w = GDN_H * gdn_d
    gdn_wp = GDN_H * GDN_DP

    n2 = LANES
    n1 = 2 * seq // n2
    fft_tabs = _fft_tables(n1, n2)
    dft_tabs = _dft_tables(ctx_len)

    pad_rows = -(batch + 1) % 8
    cc = jnp.concatenate([c, c_ctx[None, :], jnp.zeros((pad_rows, d_model), F32)], axis=0)
    mod = _modulation(cc, w_mod, b_mod)

    def mod_vecs(l, i):
        v = mod[l, :, i * d_model:(i + 1) * d_model]
        lat = v[:batch, None, :]
        cx = jnp.broadcast_to(v[batch][None, None, :], (batch, 1, d_model))
        return lat, cx

    zeros_gla = jnp.zeros((batch, gla_vw, gla_kw), F32)
    zeros_gdn = jnp.zeros((batch, GDN_H, GDN_DP, GDN_DP), F32)
    gla_bd_gain = jnp.tile(gla_norm_g, (1, GLA_H))

    for l in range(depth):
        with_ctx_out = l < depth - 1
        (sh1, csh1), (sc1, csc1), (g1, cg1), (sh2, csh2), (sc2, csc2), (g2, cg2) = [mod_vecs(l, i) for i in range(N_MOD)]

        wl = w_in[l]
        o_gla = 3 * hy_c
        o_gdn = o_gla + 2 * gla_kw + 2 * gla_vw + 2 * GLA_RANK
        o_gdn_gate = o_gdn + 3 * gdn_w
        o_gdn_small = o_gdn_gate + gdn_w
        w_small = jnp.concatenate([wl[:, o_gdn - 2 * GLA_RANK:o_gdn], wl[:, o_gdn_small:]], axis=1)
        w_small = jnp.pad(w_small, ((0, 0), (0, LANES - w_small.shape[1])))
        in_weights = [
            wl[:, :o_gla],
            wl[:, o_gla:o_gla + 2 * gla_kw],
            wl[:, o_gla + 2 * gla_kw:o_gla + 2 * gla_kw + gla_vw],
            wl[:, o_gla + 2 * gla_kw + gla_vw:o_gla + 2 * gla_kw + 2 * gla_vw],
            _pad_heads(wl[:, o_gdn:o_gdn_gate], 3 * GDN_H, 1),
            _pad_heads(wl[:, o_gdn_gate:o_gdn_small], GDN_H, 1),
            w_small,
        ]
        in_weights = [w.astype(BF16) for w in in_weights]
        gdn_cw = _pad_heads(gdn_conv_w[l], 3 * GDN_H, 1)
        gdn_gain = _pad_heads(gdn_norm_g[l][None, :].repeat(GDN_H, 0).reshape(1, gdn_w), GDN_H, 1)
        wo = w_out[l]
        w_hy = wo[:hy_c].astype(BF16)
        w_gla = wo[hy_c:hy_c + gla_vw].astype(BF16)
        w_gdn = _pad_heads(wo[hy_c + gla_vw:], GDN_H, 0).astype(BF16)
        w1 = w_mlp1[l].astype(BF16)
        w2 = w_mlp2[l].astype(BF16)
        filt_args = (hy_f_w1[l], hy_f_b1[l], hy_f_w2[l], hy_f_b2[l], hy_f_w3[l], hy_sin_freq[l], hy_c)

        def mixer_parts(tokens, shift, scale, seg_len, tl, s0_gla, s0_gdn):
            z_hy, z_qk, z_v, z_gate, z_gdn, z_gdn_gate, z_small = _inproj(
                tokens, shift, scale, norm1_g[l], in_weights, tl)
            b_f, b_b = _gla_prep(z_small, gla_w_a2[l], gla_b_a[l], tl)
            gla_f, gla_b, gla_sf, gla_sb = _gla_scan(z_qk, z_v, b_f, b_b, s0_gla[0], s0_gla[1], tl)
            q, k, v, gc = _gdn_prep(z_gdn, gdn_cw, z_small, gdn_a_log[l], gdn_dt_bias[l], gdn_d, seg_len, tl)
            gdn_f, gdn_b, gdn_sf, gdn_sb = _gdn_scan(q, k, v, gc, s0_gdn[0], s0_gdn[1], tl)
            u = _hy_short_conv(z_hy, hy_conv_w[l], hy_conv_b[l], seg_len, tl)
            return u, (gla_f, gla_b, z_gate), (gdn_f, gdn_b, z_gdn_gate), (gla_sf, gla_sb), (gdn_sf, gdn_sb)

        def finish(tokens, gate1, hy, gla, gdn, tm):
            return _outproj(tokens, gate1, hy, *gla, *gdn, gla_bd_gain[l][None, :], gdn_gain, w_hy, w_gla, w_gdn,
                            gla_dv, gdn_d, tm)

        tl_c = ctx_len
        u_c, gla_c, gdn_c, gla_s, gdn_s = mixer_parts(ctx, csh1, csc1, ctx_len, tl_c,
                                                      (zeros_gla, zeros_gla), (zeros_gdn, zeros_gdn))
        tl = 256
        u_l, gla_l, gdn_l, _, _ = mixer_parts(x, sh1, sc1, GRID_W, tl, gla_s, gdn_s)
        spec = _spectrum_two_stage(_hyena_filter_spectrum(seq, *filt_args), n1, n2)
        nb = hy_c // LANES
        y1 = _fftconv_gate(u_l, 0, u_l, nb, spec[0][0], spec[0][1], hy_d[l, 0], fft_tabs, n1, n2)
        hy_l = _fftconv_gate(y1, 0, u_l, 2 * nb, spec[1][0], spec[1][1], hy_d[l, 1], fft_tabs, n1, n2)
        x = finish(x, g1, hy_l, gla_l, gdn_l, 512)
        x = _mlp(x, sh2, sc2, g2, norm2_g[l], w1, w2, final_norm_g, not with_ctx_out, 1024, 512)

        if with_ctx_out:
            coef = _spectrum_dense(_hyena_filter_spectrum(ctx_len, *filt_args))
            y1c = _dftconv_gate(u_c, 0, u_c, 1, *coef[0], hy_d[l, 0], dft_tabs)
            hy_c_out = _dftconv_gate(y1c, 0, u_c, 2, *coef[1], hy_d[l, 1], dft_tabs)
            ctx = finish(ctx, cg1, hy_c_out, gla_c, gdn_c, ctx_len)
            ctx = _mlp(ctx, csh2, csc2, cg2, norm2_g[l], w1, w2, final_norm_g, False, ctx_len, 512)
    return x
```

```python
import functools
import math

import numpy as np
import jax
import jax.numpy as jnp
from jax import lax
from jax.experimental import pallas as pl
from jax.experimental.pallas import tpu as pltpu

F32 = jnp.float32
BF16 = jnp.bfloat16
HIGHEST = lax.Precision.HIGHEST

NORM_EPS = 1e-6
N_MOD = 6
GRID_W = 64
CHUNK = 64
LANES = 128

HY_ORDER = 2
HY_SHORT = 3
HY_EMB = 33
HY_EMB_PAD = 40
HY_DECAY_TARGET = 1e-2
HY_FAST_PCT = 0.3
HY_SLOW_PCT = 1.5

GLA_H = 4
GLA_RANK = 16
GLA_TAU = 16.0
GDN_H = 4
GDN_DP = LANES
PITCH_PAD = 8
FFT_UNROLL = 8

VMEM_LIMIT = 56 * 1024 * 1024

SCAN_ROWS = 4 * CHUNK
PROJ_ROWS = 512
MLP_ROWS = 1024
MLP_HIDDEN = 1024


def _cparams(n_grid):
    return pltpu.CompilerParams(dimension_semantics=("arbitrary",) * n_grid, vmem_limit_bytes=VMEM_LIMIT)


def _bdot(a, b):
    return jnp.dot(a.astype(BF16), b.astype(BF16), preferred_element_type=F32)


def _bdot_nt(a, b):
    return lax.dot_general(a.astype(BF16), b.astype(BF16), (((1,), (1,)), ((), ())), preferred_element_type=F32)


def _bdot_tn(a, b):
    return lax.dot_general(a.astype(BF16), b.astype(BF16), (((0,), (0,)), ((), ())), preferred_element_type=F32)


def _const_spec(shape):
    return pl.BlockSpec(shape, lambda *_: (0,) * len(shape))


def _mod_kernel(c_ref, w_ref, b_ref, o_ref):
    c = c_ref[...]
    o_ref[0] = _bdot(c * jax.nn.sigmoid(c), w_ref[0]) + b_ref[0]


def _modulation(cc, w_mod, b_mod):
    depth, d, n = w_mod.shape
    tn = n // 4
    return pl.pallas_call(
        _mod_kernel,
        grid=(depth, n // tn),
        in_specs=[pl.BlockSpec(cc.shape, lambda l, j: (0, 0)),
                  pl.BlockSpec((1, d, tn), lambda l, j: (l, 0, j)),
                  pl.BlockSpec((1, 1, tn), lambda l, j: (l, 0, j))],
        out_specs=pl.BlockSpec((1, cc.shape[0], tn), lambda l, j: (l, 0, j)),
        out_shape=jax.ShapeDtypeStruct((depth, cc.shape[0], n), F32),
        compiler_params=_cparams(2),
        name="modulation",
    )(cc, w_mod.astype(BF16), b_mod.reshape(depth, 1, n))


def _rms_modulate(x, g, shift, scale):
    y = x * lax.rsqrt(jnp.mean(x * x, axis=-1, keepdims=True) + NORM_EPS) * g
    return y * (1.0 + scale) + shift


def _inproj_kernel(x_ref, sh_ref, sc_ref, g_ref, *refs):
    n_out = len(refs) // 2
    hb = _rms_modulate(x_ref[0], g_ref[...], sh_ref[0], sc_ref[0]).astype(BF16)
    for w_ref, o_ref in zip(refs[:n_out], refs[n_out:]):
        o_ref[0] = jnp.dot(hb, w_ref[...], preferred_element_type=F32)


def _inproj(x, shift, scale, g, weights, tm):
    b, l, d = x.shape
    vec = pl.BlockSpec((1, 1, d), lambda i, j: (i, 0, 0))
    return pl.pallas_call(
        _inproj_kernel,
        grid=(b, l // tm),
        in_specs=[pl.BlockSpec((1, tm, d), lambda i, j: (i, j, 0)), vec, vec, _const_spec((1, d))]
        + [_const_spec(w.shape) for w in weights],
        out_specs=[pl.BlockSpec((1, tm, w.shape[1]), lambda i, j: (i, j, 0)) for w in weights],
        out_shape=[jax.ShapeDtypeStruct((b, l, w.shape[1]), F32) for w in weights],
        compiler_params=_cparams(2),
        name="inproj",
    )(x, shift, scale, g.reshape(1, d), *weights)


def _seg_conv(z, w, seg_len):
    rows, width = z.shape
    taps = w.shape[0]
    half = taps // 2
    pos = lax.broadcasted_iota(jnp.int32, (rows, LANES), 0) % seg_len
    valid = {d: (pos >= -d) if d < 0 else (pos < seg_len - d) for d in range(-half, half + 1) if d != 0}
    tiles = []
    for c0 in range(0, width, LANES):
        cols = slice(c0, c0 + LANES)
        zc = z[:, cols]
        acc = zc * w[half:half + 1, cols]
        for k in range(taps):
            d = k - half
            if d != 0:
                shifted = pltpu.roll(zc, (-d) % rows, 0)
                acc = acc + jnp.where(valid[d], shifted, 0.0) * w[k:k + 1, cols]
        tiles.append(acc)
    return jnp.concatenate(tiles, axis=1)


def _hy_conv_kernel(z_ref, w_ref, b_ref, o_ref, *, seg_len):
    o_ref[0] = _seg_conv(z_ref[0], w_ref[...], seg_len) + b_ref[...]


def _hy_short_conv(z, w, bias, seg_len, tl):
    b, l, width = z.shape
    return pl.pallas_call(
        functools.partial(_hy_conv_kernel, seg_len=seg_len),
        grid=(b, l // tl),
        in_specs=[pl.BlockSpec((1, tl, width), lambda i, j: (i, j, 0)), _const_spec(w.shape),
                  _const_spec((1, width))],
        out_specs=pl.BlockSpec((1, tl, width), lambda i, j: (i, j, 0)),
        out_shape=jax.ShapeDtypeStruct(z.shape, F32),
        compiler_params=_cparams(2),
        name="hy_short_conv",
    )(z, w, bias.reshape(1, width))


def _hy_filter_kernel(z_ref, w1_ref, b1_ref, w2_ref, b2_ref, w3_ref, f_ref, dl_ref, hf_ref, hb_ref):
    hy_c = dl_ref.shape[1]
    z = z_ref[...]
    fdot = lambda a, b: jnp.dot(a, b, precision=HIGHEST, preferred_element_type=F32)
    h = jnp.sin(f_ref[0:1, :] * (fdot(z, w1_ref[...]) + b1_ref[...]))
    h = jnp.sin(f_ref[1:2, :] * (fdot(h, w2_ref[...]) + b2_ref[...]))
    h = fdot(h, w3_ref[...])
    window = jnp.exp(-z[:, 0:1] * dl_ref[...])
    row = lax.broadcasted_iota(jnp.int32, window.shape, 0) + pl.program_id(0) * z.shape[0]
    for o in range(HY_ORDER):
        base = 2 * o * hy_c
        hf_ref[:, o * hy_c:(o + 1) * hy_c] = h[:, base:base + hy_c] * window
        hb_ref[:, o * hy_c:(o + 1) * hy_c] = jnp.where(row > 0, h[:, base + hy_c:base + 2 * hy_c] * window, 0.0)


def _hy_filters(z, w1, b1, w2, b2, w3, sin_freq, deltas, tl):
    l, emb = z.shape
    fh = w2.shape[0]
    hy_c = deltas.shape[1]
    out = jax.ShapeDtypeStruct((l, HY_ORDER * hy_c), F32)
    rows = lambda w: pl.BlockSpec((tl, w), lambda i: (i, 0))
    return pl.pallas_call(
        _hy_filter_kernel,
        grid=(l // tl,),
        in_specs=[rows(emb), _const_spec((emb, fh)), _const_spec((1, fh)), _const_spec((fh, fh)), _const_spec((1, fh)),
                  _const_spec(w3.shape), _const_spec((2, fh)), _const_spec((1, hy_c))],
        out_specs=[rows(HY_ORDER * hy_c)] * 2,
        out_shape=[out, out],
        compiler_params=_cparams(1),
        name="hy_filters",
    )(z, jnp.pad(w1, ((0, emb - w1.shape[0]), (0, 0))), b1.reshape(1, fh), w2, b2.reshape(1, fh), w3, sin_freq, deltas)


def _hy_embedding(length, emb):
    t = jnp.linspace(0.0, 1.0, length, dtype=F32)[:, None]
    bands = (HY_EMB - 1) // 2
    f = jnp.linspace(1e-4, bands - 1, bands, dtype=F32)
    w = 2 * math.pi * jnp.arange(length, dtype=F32) / length
    ang = w[:, None] * f[None, :]
    z = jnp.concatenate([t, jnp.cos(ang), -jnp.sin(ang)], axis=-1)
    return jnp.pad(z, ((0, 0), (0, emb - z.shape[1])))


def _fft_tables(n1, n2):
    n = n1 * n2
    s1 = np.arange(n1 // 2)[None, :, None]
    s2 = np.arange(n2)[:, None, None]
    k1 = np.arange(n1)[None, None, :]
    theta = 2.0 * np.pi * ((k1 * (n2 * s1 + s2)) % n) / n
    tt = np.concatenate([np.cos(theta), -np.sin(theta)], axis=-1)
    phi = 2.0 * np.pi * ((np.arange(n2)[:, None] * np.arange(n2)[None, :]) % n2) / n2
    c2, s2m = np.cos(phi), np.sin(phi)
    fwd = np.block([[c2, s2m], [-s2m, c2]])
    inv = np.block([[c2, -s2m], [s2m, c2]])
    return jnp.asarray(tt, F32), jnp.asarray(fwd, F32), jnp.asarray(inv, F32)


def _time_pitch(n2):
    return n2 + PITCH_PAD


def _spec_pitch(n2):
    return 2 * n2 + PITCH_PAD


def _dft_load_time(load_block, time_ref, n1, n2):
    def copy(s1, carry):
        time_ref[pl.ds(pl.multiple_of(s1 * _time_pitch(n2), 8), n2), :] = load_block(
            pl.ds(pl.multiple_of(s1 * n2, n2), n2))
        return carry

    lax.fori_loop(0, n1 // 2, copy, 0, unroll=min(FFT_UNROLL, n1 // 2))


def _dft_stage_a(time_ref, tt_ref, spec_ref, n1, n2):
    def stage_a(s2, carry):
        slab = time_ref[pl.ds(s2, n1 // 2, stride=_time_pitch(n2)), :]
        res = _bdot_tn(tt_ref[s2], slab)
        spec_ref[pl.ds(s2, n1, stride=_spec_pitch(n2)), :] = res[:n1]
        spec_ref[pl.ds(n2 + s2, n1, stride=_spec_pitch(n2)), :] = res[n1:]
        return carry

    lax.fori_loop(0, n2, stage_a, 0, unroll=FFT_UNROLL)


def _spec_rows(k1, n2):
    return pl.ds(pl.multiple_of(k1 * _spec_pitch(n2), 8), 2 * n2)


def _spectrum_kernel(hf_ref, hb_ref, tt_ref, fwd_ref, hr_ref, hi_ref, time_ref, spec_ref, *, n1, n2):
    inv_n = 1.0 / (n1 * n2)
    fwd = fwd_ref[...].astype(BF16)
    for sign, src_ref in ((1.0, hf_ref), (-1.0, hb_ref)):
        _dft_load_time(lambda rows: src_ref[rows, :], time_ref, n1, n2)
        _dft_stage_a(time_ref, tt_ref, spec_ref, n1, n2)

        def stage_b(k1, carry):
            rows = _spec_rows(k1, n2)
            spec = jnp.dot(fwd, spec_ref[rows, :].astype(BF16), preferred_element_type=F32) * inv_n
            hrows = pl.ds(pl.multiple_of(k1 * n2, n2), n2)
            if sign > 0:
                hr_ref[hrows, :] = spec[:n2]
                hi_ref[hrows, :] = spec[n2:]
            else:
                hr_ref[hrows, :] += spec[:n2]
                hi_ref[hrows, :] -= spec[n2:]
            return carry

        lax.fori_loop(0, n1, stage_b, 0, unroll=min(FFT_UNROLL, n1))


def _filter_spectrum(hf, hb, tables, n1, n2):
    tt, fwd, _ = tables
    l, c = hf.shape
    n = n1 * n2
    chan = lambda rows: pl.BlockSpec((rows, LANES), lambda j: (0, j))
    return pl.pallas_call(
        functools.partial(_spectrum_kernel, n1=n1, n2=n2),
        grid=(c // LANES,),
        in_specs=[chan(l), chan(l), _const_spec(tt.shape), _const_spec(fwd.shape)],
        out_specs=[chan(n), chan(n)],
        out_shape=[jax.ShapeDtypeStruct((n, c), F32)] * 2,
        scratch_shapes=_fft_scratch(n1, n2),
        compiler_params=_cparams(1),
        name="hy_spectrum",
    )(hf, hb, tt, fwd)


def _fft_scratch(n1, n2):
    return [pltpu.VMEM((n1 // 2 * _time_pitch(n2), LANES), F32), pltpu.VMEM((n1 * _spec_pitch(n2), LANES), F32)]


def _fftconv_kernel(u_ref, gate_ref, hr_ref, hi_ref, d_ref, tt_ref, fwd_ref, inv_ref, o_ref, time_ref, spec_ref,
                    *, n1, n2):
    n1h = n1 // 2
    fwd = fwd_ref[...].astype(BF16)
    inv = inv_ref[...].astype(BF16)
    _dft_load_time(lambda rows: u_ref[0, rows, :], time_ref, n1, n2)
    _dft_stage_a(time_ref, tt_ref, spec_ref, n1, n2)

    def stage_b(k1, carry):
        rows = _spec_rows(k1, n2)
        spec = jnp.dot(fwd, spec_ref[rows, :].astype(BF16), preferred_element_type=F32)
        br, bi = spec[:n2], spec[n2:]
        hrows = pl.ds(pl.multiple_of(k1 * n2, n2), n2)
        hr, hi = hr_ref[hrows, :], hi_ref[hrows, :]
        y = jnp.concatenate([br * hr - bi * hi, br * hi + bi * hr], axis=0)
        spec_ref[rows, :] = jnp.dot(inv, y.astype(BF16), preferred_element_type=F32)
        return carry

    lax.fori_loop(0, n1, stage_b, 0, unroll=min(FFT_UNROLL, n1))

    def stage_a_inv(s2, carry):
        g = jnp.concatenate([spec_ref[pl.ds(s2, n1, stride=_spec_pitch(n2)), :],
                             spec_ref[pl.ds(n2 + s2, n1, stride=_spec_pitch(n2)), :]], axis=0)
        time_ref[pl.ds(s2, n1h, stride=_time_pitch(n2)), :] = jnp.dot(
            tt_ref[s2], g.astype(BF16), preferred_element_type=F32)
        return carry

    lax.fori_loop(0, n2, stage_a_inv, 0, unroll=FFT_UNROLL)

    def gate_rows(s1, carry):
        rows = pl.ds(pl.multiple_of(s1 * n2, n2), n2)
        conv = time_ref[pl.ds(pl.multiple_of(s1 * _time_pitch(n2), 8), n2), :]
        o_ref[0, rows, :] = gate_ref[0, rows, :] * (conv + d_ref[...] * u_ref[0, rows, :])
        return carry

    lax.fori_loop(0, n1h, gate_rows, 0, unroll=min(FFT_UNROLL, n1h))


def _fftconv_gate(u_arr, u_blk, gate_arr, gate_blk, hr, hi, h_blk, d, tables, n1, n2):
    tt, fwd, inv = tables
    b, l, _ = u_arr.shape
    n = hr.shape[0]
    c = d.shape[0]
    ncb = c // LANES
    seq = lambda off: pl.BlockSpec((1, l, LANES), lambda j, i: (i, 0, off + j))
    chan = lambda rows, off=0: pl.BlockSpec((rows, LANES), lambda j, i: (0, off + j))
    return pl.pallas_call(
        functools.partial(_fftconv_kernel, n1=n1, n2=n2),
        grid=(ncb, b),
        in_specs=[seq(u_blk), seq(gate_blk), chan(n, h_blk), chan(n, h_blk), chan(1),
                  _const_spec(tt.shape), _const_spec(fwd.shape), _const_spec(inv.shape)],
        out_specs=pl.BlockSpec((1, l, LANES), lambda j, i: (i, 0, j)),
        out_shape=jax.ShapeDtypeStruct((b, l, c), F32),
        scratch_shapes=_fft_scratch(n1, n2),
        compiler_params=_cparams(2),
        name="hy_fftconv",
    )(u_arr, gate_arr, hr, hi, d.reshape(1, c), tt, fwd, inv)


def _dft_tables(l):
    n = 2 * l
    ang = 2.0 * np.pi * ((np.arange(l)[:, None] * np.arange(l)[None, :]) % n) / n
    cf = np.cos(ang)
    sf = -np.sin(ang)
    sf[0, :] = (-1.0) ** np.arange(l)
    fwd = np.concatenate([cf, sf], axis=0)
    return jnp.asarray(fwd, F32), jnp.asarray(fwd.T, F32)


def _dftconv_kernel(u_ref, gate_ref, p_ref, q_ref, r_ref, d_ref, fwd_ref, inv_ref, o_ref):
    u = u_ref[0]
    l = u.shape[0]
    spec = _bdot(fwd_ref[...], u)
    xr, xi = spec[:l], spec[l:]
    q = q_ref[...]
    y = jnp.concatenate([xr * p_ref[...] - xi * q, xr * q + xi * r_ref[...]], axis=0)
    conv = _bdot(inv_ref[...], y)
    o_ref[0] = gate_ref[0] * (conv + d_ref[...] * u)


def _dense_spectrum_kernel(hf_ref, hb_ref, fwd_ref, p_ref, q_ref, r_ref):
    l = hf_ref.shape[0]
    xf = _bdot(fwd_ref[...], hf_ref[...])
    xb = _bdot(fwd_ref[...], hb_ref[...])
    first = lax.broadcasted_iota(jnp.int32, (l, hf_ref.shape[1]), 0) == 0
    scale = jnp.where(first, 0.5 / l, 1.0 / l)
    re = (xf[:l] + xb[:l]) * scale
    p_ref[...] = re
    q_ref[...] = jnp.where(first, 0.0, (xf[l:] - xb[l:]) * scale)
    r_ref[...] = jnp.where(first, (xf[l:] + xb[l:]) * scale, re)


def _dense_filter_spectrum(hf, hb, tables):
    fwd, _ = tables
    out = jax.ShapeDtypeStruct(hf.shape, F32)
    return pl.pallas_call(
        _dense_spectrum_kernel,
        grid=(1,),
        in_specs=[_const_spec(hf.shape), _const_spec(hb.shape), _const_spec(fwd.shape)],
        out_specs=[_const_spec(hf.shape)] * 3,
        out_shape=[out] * 3,
        compiler_params=_cparams(1),
        name="hy_dense_spectrum",
    )(hf, hb, fwd)


def _dftconv_gate(u_arr, u_blk, gate_arr, gate_blk, p, q, r, h_blk, d, tables):
    fwd, inv = tables
    b, l, _ = u_arr.shape
    c = d.shape[0]
    seq = lambda off: pl.BlockSpec((1, l, c), lambda i: (i, 0, off))
    coef = pl.BlockSpec((l, c), lambda i: (0, h_blk))
    return pl.pallas_call(
        _dftconv_kernel,
        grid=(b,),
        in_specs=[seq(u_blk), seq(gate_blk), coef, coef, coef,
                  _const_spec((1, c)), _const_spec(fwd.shape), _const_spec(inv.shape)],
        out_specs=pl.BlockSpec((1, l, c), lambda i: (i, 0, 0)),
        out_shape=jax.ShapeDtypeStruct((b, l, c), F32),
        compiler_params=_cparams(1),
        name="hy_dftconv",
    )(u_arr, gate_arr, p, q, r, d.reshape(1, c), fwd, inv)


def _chunk_cumsum_mats(tl):
    r = np.arange(tl)
    same = (r[:, None] // CHUNK) == (r[None, :] // CHUNK)
    lower = same & (r[None, :] <= r[:, None])
    upper = same & (r[None, :] >= r[:, None])
    return jnp.asarray(lower, F32), jnp.asarray(upper, F32)


def _gla_prep_kernel(a_ref, w_ref, b_ref, lo_ref, up_ref, bf_ref, bb_ref):
    a = a_ref[0]
    for d, (tri_ref, o_ref) in enumerate(((lo_ref, bf_ref), (up_ref, bb_ref))):
        pre = jnp.dot(a[:, d * GLA_RANK:(d + 1) * GLA_RANK], w_ref[d], precision=HIGHEST,
                      preferred_element_type=F32) + b_ref[d]
        log_a = jax.nn.log_sigmoid(pre) * (1.0 / GLA_TAU)
        o_ref[0] = jnp.dot(tri_ref[...], log_a, precision=HIGHEST, preferred_element_type=F32)


def _gla_prep(small, w_a2, b_a, tl):
    b, l, sw = small.shape
    kw = w_a2.shape[-1]
    lower, upper = _chunk_cumsum_mats(tl)
    out = pl.BlockSpec((1, tl, kw), lambda i, j: (i, j, 0))
    return pl.pallas_call(
        _gla_prep_kernel,
        grid=(b, l // tl),
        in_specs=[pl.BlockSpec((1, tl, sw), lambda i, j: (i, j, 0)), _const_spec(w_a2.shape),
                  _const_spec((2, 1, kw)), _const_spec((tl, tl)), _const_spec((tl, tl))],
        out_specs=[out, out],
        out_shape=[jax.ShapeDtypeStruct((b, l, kw), F32)] * 2,
        compiler_params=_cparams(2),
        name="gla_prep",
    )(small, w_a2, b_a.reshape(2, 1, kw), lower, upper)


def _gla_scan_kernel(qkf_ref, vf_ref, bf_ref, qkb_ref, vb_ref, bb_ref, s0f_ref, s0b_ref, hm_ref, vm_ref, bd_ref,
                     of_ref, ob_ref, sff_ref, sfb_ref, stf_ref, stb_ref, *, n_chunks, kw, q_scale):
    j = pl.program_id(1)

    @pl.when(j == 0)
    def _():
        stf_ref[...] = s0f_ref[0]
        stb_ref[...] = s0b_ref[0]

    ri = lax.broadcasted_iota(jnp.int32, (CHUNK, GLA_H * CHUNK), 0)
    ci = lax.broadcasted_iota(jnp.int32, (CHUNK, GLA_H * CHUNK), 1) % CHUNK
    hm = hm_ref[...]
    vm = vm_ref[...]
    bd = bd_ref[...]
    dirs = ((qkf_ref, vf_ref, bf_ref, of_ref, stf_ref, ri >= ci, CHUNK - 1, CHUNK // 2 - 1, False),
            (qkb_ref, vb_ref, bb_ref, ob_ref, stb_ref, ci >= ri, 0, CHUNK // 2, True))

    qts, kstacks, vstacks, keeps, qes, vs, kds, decs, sinks = [], [], [], [], [], [], [], [], []
    for qk_ref, v_ref, b_ref, o_ref, st_ref, keep, last, mid, rev in dirs:
        for c in range(n_chunks):
            rows = slice((n_chunks - 1 - c if rev else c) * CHUNK, (n_chunks - c if rev else c + 1) * CHUNK)
            qk = qk_ref[0, rows, :]
            q, k = qk[:, :kw] * q_scale, qk[:, kw:]
            v = v_ref[0, rows, :]
            bc = b_ref[0, rows, :]
            b_mid = bc[mid:mid + 1, :]
            b_last = bc[last:last + 1, :]
            kt = k * jnp.exp(b_mid - bc)
            qts.append(q * jnp.exp(bc - b_mid))
            kstacks.append(jnp.concatenate([kt * hm[h:h + 1, :] for h in range(GLA_H)], axis=0).astype(BF16))
            vstacks.append(jnp.concatenate([v * vm[h:h + 1, :] for h in range(GLA_H)], axis=0).astype(BF16))
            keeps.append(keep)
            qes.append((q * jnp.exp(bc)).astype(BF16))
            vs.append(v)
            kds.append(k * jnp.exp(b_last - bc))
            decs.append(jnp.exp(b_last))
            sinks.append((o_ref, rows))
    attns = [jnp.where(keep, _bdot_nt(qt, ks), 0.0) for keep, qt, ks in zip(keeps, qts, kstacks)]
    o_intra = [_bdot(a, vst) for a, vst in zip(attns, vstacks)]
    upds = [_bdot_tn(v, kd) * bd for v, kd in zip(vs, kds)]

    for d, (_, _, _, _, st_ref, _, _, _, _) in enumerate(dirs):
        st = st_ref[...]
        for c in range(n_chunks):
            i = d * n_chunks + c
            o_ref, rows = sinks[i]
            o_ref[0, rows, :] = o_intra[i] + _bdot_nt(qes[i], st)
            st = st * decs[i] + upds[i]
        st_ref[...] = st

    @pl.when(j == pl.num_programs(1) - 1)
    def _():
        sff_ref[0] = stf_ref[...]
        sfb_ref[0] = stb_ref[...]


def _gla_scan(qk, v, b_f, b_b, s0_f, s0_b, tl):
    b, l, kw2 = qk.shape
    kw, vw = kw2 // 2, v.shape[-1]
    dk, dv = kw // GLA_H, vw // GLA_H
    nblk = l // tl
    heads_k = np.arange(kw) // dk
    heads_v = np.arange(vw) // dv
    hm = jnp.asarray(np.arange(8)[:, None] == heads_k[None, :], F32)
    vm = jnp.asarray(np.arange(8)[:, None] == heads_v[None, :], F32)
    bd = jnp.asarray(heads_v[:, None] == heads_k[None, :], F32)
    fwd = lambda w: pl.BlockSpec((1, tl, w), lambda i, j: (i, j, 0))
    bwd = lambda w: pl.BlockSpec((1, tl, w), lambda i, j: (i, nblk - 1 - j, 0))
    state = pl.BlockSpec((1, vw, kw), lambda i, j: (i, 0, 0))
    return pl.pallas_call(
        functools.partial(_gla_scan_kernel, n_chunks=tl // CHUNK, kw=kw, q_scale=dk ** -0.5),
        grid=(b, nblk),
        in_specs=[fwd(kw2), fwd(vw), fwd(kw), bwd(kw2), bwd(vw), bwd(kw), state, state,
                  _const_spec(hm.shape), _const_spec(vm.shape), _const_spec(bd.shape)],
        out_specs=[fwd(vw), bwd(vw), state, state],
        out_shape=[jax.ShapeDtypeStruct((b, l, vw), F32)] * 2 + [jax.ShapeDtypeStruct((b, vw, kw), F32)] * 2,
        scratch_shapes=[pltpu.VMEM((vw, kw), F32)] * 2,
        compiler_params=_cparams(2),
        name="gla_scan",
    )(qk, v, b_f, qk, v, b_b, s0_f, s0_b, hm, vm, bd)


GDN_GF, GDN_GB, GDN_BF, GDN_BB = 32, 36, 40, 44

def _gdn_prep_kernel(z_ref, w_ref, small_ref, nea_ref, dtb_ref, lo_ref, up_ref,
                     q_ref, k_ref, v_ref, gc_ref, *, seg_len, q_scale):
    width = q_ref.shape[-1]
    y = _seg_conv(z_ref[0], w_ref[...], seg_len)
    y = y * jax.nn.sigmoid(y)
    for part, (o_ref, scale) in enumerate(((q_ref, q_scale), (k_ref, 1.0))):
        for h in range(GDN_H):
            cols = slice(part * width + h * GDN_DP, part * width + (h + 1) * GDN_DP)
            t = y[:, cols]
            o_ref[0, :, h * GDN_DP:(h + 1) * GDN_DP] = t * (
                lax.rsqrt(jnp.sum(t * t, axis=-1, keepdims=True) + NORM_EPS) * scale)
    v_ref[0] = y[:, 2 * width:]
    small = small_ref[0]
    log_decay = nea_ref[...] * jax.nn.softplus(small + dtb_ref[...])
    cum_f = jnp.dot(lo_ref[...], log_decay, precision=HIGHEST, preferred_element_type=F32)
    cum_b = jnp.dot(up_ref[...], log_decay, precision=HIGHEST, preferred_element_type=F32)
    lane = lax.broadcasted_iota(jnp.int32, small.shape, 1)
    gc_ref[0] = jnp.where(lane < GDN_GB, cum_f, jnp.where(lane < GDN_BF, cum_b, jax.nn.sigmoid(small)))


def _gdn_prep(z, conv_w, small, a_log, dt_bias, head_dim, seg_len, tl):
    b, l, w3 = z.shape
    width = w3 // 3
    nea = jnp.zeros((1, LANES), F32).at[0, GDN_GF:GDN_BF].set(-jnp.exp(a_log.reshape(-1)))
    dtb = jnp.zeros((1, LANES), F32).at[0, GDN_GF:GDN_BF].set(dt_bias.reshape(-1))
    lower, upper = _chunk_cumsum_mats(tl)
    tok = lambda w: pl.BlockSpec((1, tl, w), lambda i, j: (i, j, 0))
    return pl.pallas_call(
        functools.partial(_gdn_prep_kernel, seg_len=seg_len, q_scale=head_dim ** -0.5),
        grid=(b, l // tl),
        in_specs=[tok(w3), _const_spec(conv_w.shape), tok(LANES), _const_spec((1, LANES)), _const_spec((1, LANES)),
                  _const_spec((tl, tl)), _const_spec((tl, tl))],
        out_specs=[tok(width), tok(width), tok(width), tok(LANES)],
        out_shape=[jax.ShapeDtypeStruct((b, l, width), F32)] * 3 + [jax.ShapeDtypeStruct((b, l, LANES), F32)],
        compiler_params=_cparams(2),
        name="gdn_prep",
    )(z, conv_w, small, nea, dtb, lower, upper)


def _unit_tri_inverses(mats, eye, m16, m32, m64):
    diag = [a * m16 for a in mats]
    inv = [eye + a for a in diag]
    pw = diag
    for _ in range(3):
        pw = [_bdot(p, p) for p in pw]
        inv = [t + _bdot(t, p) for t, p in zip(inv, pw)]
    for mask in (m32, m64):
        mid = [_bdot(t, a * mask) for t, a in zip(inv, mats)]
        inv = [t + _bdot(m, t) for t, m in zip(inv, mid)]
    return inv


def _gdn_scan_kernel(qf_ref, kf_ref, vf_ref, gcf_ref, qb_ref, kb_ref, vb_ref, gcb_ref,
                     s0f_ref, s0b_ref, of_ref, ob_ref, sff_ref, sfb_ref, sf_ref, sb_ref, *, tl):
    j = pl.program_id(1)

    @pl.when(j == 0)
    def _():
        sf_ref[...] = s0f_ref[0]
        sb_ref[...] = s0b_ref[0]

    nck = tl // CHUNK
    ri = lax.broadcasted_iota(jnp.int32, (tl, tl), 0)
    ci = lax.broadcasted_iota(jnp.int32, (tl, tl), 1)
    same = lambda n: (ri // n) == (ci // n)
    chunk = same(CHUNK)
    eye = (ri == ci).astype(F32)
    m16 = same(16).astype(F32)
    m32 = (same(32) & ~same(16)).astype(F32)
    m64 = (chunk & ~same(32)).astype(F32)
    dirs = ((qf_ref, kf_ref, vf_ref, gcf_ref, of_ref, sf_ref, ri > ci, ri >= ci, CHUNK - 1, GDN_GF, GDN_BF, False),
            (qb_ref, kb_ref, vb_ref, gcb_ref, ob_ref, sb_ref, ci > ri, ci >= ri, 0, GDN_GB, GDN_BB, True))

    qs, ks, vs, gcums, betas, stricts, incls, glasts, outs = [], [], [], [], [], [], [], [], []
    for q_ref, k_ref, v_ref, gc_ref, o_ref, s_ref, strict, incl, last, g_col, b_col, rev in dirs:
        gc = gc_ref[0]
        gct = gc.T
        for h in range(GDN_H):
            cols = slice(h * GDN_DP, (h + 1) * GDN_DP)
            gcum = gc[:, g_col + h:g_col + h + 1]
            decay = jnp.exp(jnp.minimum(gcum - gct[g_col + h:g_col + h + 1, :], 0.0))
            qs.append(q_ref[0, :, cols])
            ks.append(k_ref[0, :, cols])
            vs.append(v_ref[0, :, cols])
            gcums.append(gcum)
            betas.append(gc[:, b_col + h:b_col + h + 1])
            stricts.append(jnp.where(chunk & strict, decay, 0.0))
            incls.append(jnp.where(chunk & incl, decay, 0.0))
            glasts.append(jnp.concatenate(
                [jnp.broadcast_to(gcum[c * CHUNK + last:c * CHUNK + last + 1, :], (CHUNK, 1)) for c in range(nck)],
                axis=0))
            outs.append((o_ref, s_ref, h, cols, rev))
    n_pairs = len(qs)
    e_cols = [jnp.exp(g) for g in gcums]
    kbs = [k * b for k, b in zip(ks, betas)]
    mats = [-_bdot_nt(kb, k) * d for kb, k, d in zip(kbs, ks, stricts)]
    invs = _unit_tri_inverses(mats, eye, m16, m32, m64)
    wus = [_bdot(t, jnp.concatenate([kb * e, v * b], axis=1))
           for t, kb, e, v, b in zip(invs, kbs, e_cols, vs, betas)]
    attns = [_bdot_nt(q, k) * d for q, k, d in zip(qs, ks, incls)]
    awus = [_bdot(a, wu) for a, wu in zip(attns, wus)]
    q_effs = [(q * e - awu[:, :GDN_DP]).astype(BF16) for q, e, awu in zip(qs, e_cols, awus)]
    ws = [wu[:, :GDN_DP].astype(BF16) for wu in wus]
    us = [wu[:, GDN_DP:] for wu in wus]
    o_intra = [awu[:, GDN_DP:] for awu in awus]
    kds = [(k * jnp.exp(gl - g)).astype(BF16) for k, gl, g in zip(ks, glasts, gcums)]
    decs = [jnp.exp(gl) for gl in glasts]

    states = [s_ref[h] for (_, s_ref, h, _, _) in outs]
    for c in range(nck):
        rows = [slice((nck - 1 - c if rev else c) * CHUNK, (nck - c if rev else c + 1) * CHUNK)
                for (_, _, _, _, rev) in outs]
        prods = [jnp.dot(jnp.concatenate([q_effs[i][rows[i]], ws[i][rows[i]]], axis=0), states[i].astype(BF16),
                         preferred_element_type=F32) for i in range(n_pairs)]
        for i, (o_ref, _, _, cols, _) in enumerate(outs):
            o_ref[0, rows[i], cols] = o_intra[i][rows[i]] + prods[i][:CHUNK]
        v_news = [us[i][rows[i]] - prods[i][CHUNK:] for i in range(n_pairs)]
        states = [decs[i][rows[i]][0:1] * states[i] + _bdot_tn(kds[i][rows[i]], v_news[i]) for i in range(n_pairs)]
    for i, (_, s_ref, h, _, _) in enumerate(outs):
        s_ref[h] = states[i]

    @pl.when(j == pl.num_programs(1) - 1)
    def _():
        sff_ref[0] = sf_ref[...]
        sfb_ref[0] = sb_ref[...]


def _gdn_scan(q, k, v, gc, s0_f, s0_b, tl):
    b, l, width = q.shape
    nblk = l // tl
    fwd = lambda w: pl.BlockSpec((1, tl, w), lambda i, j: (i, j, 0))
    bwd = lambda w: pl.BlockSpec((1, tl, w), lambda i, j: (i, nblk - 1 - j, 0))
    state = pl.BlockSpec((1, GDN_H, GDN_DP, GDN_DP), lambda i, j: (i, 0, 0, 0))
    st_shape = jax.ShapeDtypeStruct((b, GDN_H, GDN_DP, GDN_DP), F32)
    return pl.pallas_call(
        functools.partial(_gdn_scan_kernel, tl=tl),
        grid=(b, nblk),
        in_specs=[fwd(width), fwd(width), fwd(width), fwd(LANES),
                  bwd(width), bwd(width), bwd(width), bwd(LANES), state, state],
        out_specs=[fwd(width), bwd(width), state, state],
        out_shape=[jax.ShapeDtypeStruct((b, l, width), F32)] * 2 + [st_shape] * 2,
        scratch_shapes=[pltpu.VMEM((GDN_H, GDN_DP, GDN_DP), F32)] * 2,
        compiler_params=_cparams(2),
        name="gdn_scan",
    )(q, k, v, gc, q, k, v, gc, s0_f, s0_b)


def _gated_head_norm(o, gate, g_norm, ones_bd, inv_d):
    sq = o * o
    hi = sq.astype(BF16)
    lo = (sq - hi.astype(F32)).astype(BF16)
    ms = (jnp.dot(hi, ones_bd, preferred_element_type=F32) + jnp.dot(lo, ones_bd, preferred_element_type=F32)) * inv_d
    return o * lax.rsqrt(ms + NORM_EPS) * g_norm * (gate * jax.nn.sigmoid(gate))


def _outproj_kernel(x_ref, g1_ref, hy_ref, glf_ref, glb_ref, glg_ref, gdf_ref, gdb_ref, gdg_ref,
                    gln_ref, gdn_ref, glm_ref, gdm_ref, why_ref, wgl_ref, wgd_ref, o_ref, *, gla_dv, gdn_d):
    y_gla = _gated_head_norm(glf_ref[0] + glb_ref[0], glg_ref[0], gln_ref[...], glm_ref[...], 1.0 / gla_dv)
    y_gdn = _gated_head_norm(gdf_ref[0] + gdb_ref[0], gdg_ref[0], gdn_ref[...], gdm_ref[...], 1.0 / gdn_d)
    acc = _bdot(hy_ref[0], why_ref[...]) + _bdot(y_gla, wgl_ref[...]) + _bdot(y_gdn, wgd_ref[...])
    o_ref[0] = x_ref[0] + g1_ref[0] * acc


def _outproj(x, g1, hy, gla_f, gla_b, gla_gate, gdn_f, gdn_b, gdn_gate, gla_g, gdn_g, w_hy, w_gla, w_gdn,
             gla_dv, gdn_d, tm):
    b, l, d = x.shape
    hyw, glw, gdw = hy.shape[-1], gla_f.shape[-1], gdn_f.shape[-1]
    gl_heads = np.arange(glw) // gla_dv
    gd_heads = np.arange(gdw) // GDN_DP
    gl_m = jnp.asarray(gl_heads[:, None] == gl_heads[None, :], BF16)
    gd_m = jnp.asarray(gd_heads[:, None] == gd_heads[None, :], BF16)
    tok = lambda w: pl.BlockSpec((1, tm, w), lambda i, j: (i, j, 0))
    return pl.pallas_call(
        functools.partial(_outproj_kernel, gla_dv=gla_dv, gdn_d=gdn_d),
        grid=(b, l // tm),
        in_specs=[tok(d), pl.BlockSpec((1, 1, d), lambda i, j: (i, 0, 0)), tok(hyw), tok(glw), tok(glw), tok(glw),
                  tok(gdw), tok(gdw), tok(gdw), _const_spec((1, glw)), _const_spec((1, gdw)),
                  _const_spec(gl_m.shape), _const_spec(gd_m.shape),
                  _const_spec(w_hy.shape), _const_spec(w_gla.shape), _const_spec(w_gdn.shape)],
        out_specs=tok(d),
        out_shape=jax.ShapeDtypeStruct(x.shape, F32),
        compiler_params=_cparams(2),
        name="outproj",
    )(x, g1, hy, gla_f, gla_b, gla_gate, gdn_f, gdn_b, gdn_gate, gla_g, gdn_g, gl_m, gd_m, w_hy, w_gla, w_gdn)


def _mlp_kernel(x_ref, sh_ref, sc_ref, g2_ref, ng_ref, w1_ref, w2_ref, fg_ref, o_ref, hn_ref, acc_ref, *, final_norm):
    kk = pl.program_id(2)

    @pl.when(kk == 0)
    def _():
        hn_ref[...] = _rms_modulate(x_ref[0], ng_ref[...], sh_ref[0], sc_ref[0]).astype(BF16)
        acc_ref[...] = jnp.zeros_like(acc_ref)

    hid = jnp.maximum(jnp.dot(hn_ref[...], w1_ref[...], preferred_element_type=F32), 0.0)
    acc_ref[...] += _bdot(hid * hid, w2_ref[...])

    @pl.when(kk == pl.num_programs(2) - 1)
    def _():
        y = x_ref[0] + g2_ref[0] * acc_ref[...]
        if final_norm:
            y = y * lax.rsqrt(jnp.mean(y * y, axis=-1, keepdims=True) + NORM_EPS) * fg_ref[...]
        o_ref[0] = y


def _mlp(x, shift, scale, gate, norm_g, w1, w2, final_g, final_norm, tm, th):
    b, l, d = x.shape
    dff = w1.shape[1]
    vec = pl.BlockSpec((1, 1, d), lambda i, j, k: (i, 0, 0))
    row = pl.BlockSpec((1, d), lambda i, j, k: (0, 0))
    return pl.pallas_call(
        functools.partial(_mlp_kernel, final_norm=final_norm),
        grid=(b, l // tm, dff // th),
        in_specs=[pl.BlockSpec((1, tm, d), lambda i, j, k: (i, j, 0)), vec, vec, vec, row,
                  pl.BlockSpec((d, th), lambda i, j, k: (0, k)), pl.BlockSpec((th, d), lambda i, j, k: (k, 0)), row],
        out_specs=pl.BlockSpec((1, tm, d), lambda i, j, k: (i, j, 0)),
        out_shape=jax.ShapeDtypeStruct(x.shape, F32),
        scratch_shapes=[pltpu.VMEM((tm, d), BF16), pltpu.VMEM((tm, d), F32)],
        compiler_params=_cparams(3),
        name="mlp",
    )(x, shift, scale, gate, norm_g.reshape(1, d), w1, w2, final_g.reshape(1, d))


def _pad_heads(w, n_heads, axis):
    shape = w.shape
    d = shape[axis] // n_heads
    w = w.reshape(shape[:axis] + (n_heads, d) + shape[axis + 1:])
    pad = [(0, 0)] * w.ndim
    pad[axis + 1] = (0, GDN_DP - d)
    w = jnp.pad(w, pad)
    return w.reshape(shape[:axis] + (n_heads * GDN_DP,) + shape[axis + 1:])


def kernel(x, c, ctx, c_ctx, norm1_g, norm2_g, w_mod, b_mod, w_in, w_out, hy_conv_w, hy_conv_b, hy_f_w1, hy_f_b1, hy_f_w2, hy_f_b2, hy_f_w3, hy_sin_freq, hy_d, gla_w_a2, gla_b_a, gla_norm_g, gdn_conv_w, gdn_a_log, gdn_dt_bias, gdn_norm_g, w_mlp1, w_mlp2, final_norm_g):
    batch, seq, d_model = x.shape
    ctx_len = ctx.shape[1]
    depth = w_mod.shape[0]
    hy_c = hy_d.shape[-1]
    gla_kw = gla_w_a2.shape[-1]
    gla_dv = gla_norm_g.shape[-1]
    gla_vw = GLA_H * gla_dv
    gdn_d = gdn_norm_g.shape[-1]
    gdn_w = GDN_H * gdn_d
    gdn_wp = GDN_H * GDN_DP

    n2 = LANES
    n1 = 2 * seq // n2
    fft_tabs = _fft_tables(n1, n2)
    dft_tabs = _dft_tables(ctx_len)
    emb_lat = _hy_embedding(seq, HY_EMB_PAD)
    emb_ctx = _hy_embedding(ctx_len, HY_EMB_PAD)
    hy_deltas = jnp.abs(jnp.linspace(math.log(HY_DECAY_TARGET) / HY_SLOW_PCT,
                                     math.log(HY_DECAY_TARGET) / HY_FAST_PCT, hy_c, dtype=F32))[None, :]

    pad_rows = -(batch + 1) % 8
    cc = jnp.concatenate([c, c_ctx[None, :], jnp.zeros((pad_rows, d_model), F32)], axis=0)
    mod = _modulation(cc, w_mod, b_mod)

    def mod_vecs(l, i):
        v = mod[l, :, i * d_model:(i + 1) * d_model]
        lat = v[:batch, None, :]
        cx = jnp.broadcast_to(v[batch][None, None, :], (batch, 1, d_model))
        return lat, cx

    zeros_gla = jnp.zeros((batch, gla_vw, gla_kw), F32)
    zeros_gdn = jnp.zeros((batch, GDN_H, GDN_DP, GDN_DP), F32)
    gla_bd_gain = jnp.tile(gla_norm_g, (1, GLA_H))

    for l in range(depth):
        with_ctx_out = l < depth - 1
        (sh1, csh1), (sc1, csc1), (g1, cg1), (sh2, csh2), (sc2, csc2), (g2, cg2) = [mod_vecs(l, i) for i in range(N_MOD)]

        wl = w_in[l]
        o_gla = 3 * hy_c
        o_gdn = o_gla + 2 * gla_kw + 2 * gla_vw + 2 * GLA_RANK
        o_gdn_gate = o_gdn + 3 * gdn_w
        o_gdn_small = o_gdn_gate + gdn_w
        w_small = jnp.concatenate([wl[:, o_gdn - 2 * GLA_RANK:o_gdn], wl[:, o_gdn_small:]], axis=1)
        w_small = jnp.pad(w_small, ((0, 0), (0, LANES - w_small.shape[1])))
        in_weights = [
            wl[:, :o_gla],
            wl[:, o_gla:o_gla + 2 * gla_kw],
            wl[:, o_gla + 2 * gla_kw:o_gla + 2 * gla_kw + gla_vw],
            wl[:, o_gla + 2 * gla_kw + gla_vw:o_gla + 2 * gla_kw + 2 * gla_vw],
            _pad_heads(wl[:, o_gdn:o_gdn_gate], 3 * GDN_H, 1),
            _pad_heads(wl[:, o_gdn_gate:o_gdn_small], GDN_H, 1),
            w_small,
        ]
        in_weights = [w.astype(BF16) for w in in_weights]
        gdn_cw = _pad_heads(gdn_conv_w[l], 3 * GDN_H, 1)
        gdn_gain = _pad_heads(gdn_norm_g[l][None, :].repeat(GDN_H, 0).reshape(1, gdn_w), GDN_H, 1)
        wo = w_out[l]
        w_hy = wo[:hy_c].astype(BF16)
        w_gla = wo[hy_c:hy_c + gla_vw].astype(BF16)
        w_gdn = _pad_heads(wo[hy_c + gla_vw:], GDN_H, 0).astype(BF16)
        w1 = w_mlp1[l].astype(BF16)
        w2 = w_mlp2[l].astype(BF16)
        filt_args = (hy_f_w1[l], hy_f_b1[l], hy_f_w2[l], hy_f_b2[l], hy_f_w3[l], hy_sin_freq[l], hy_deltas)

        def mixer_parts(tokens, shift, scale, seg_len, s0_gla, s0_gdn):
            length = tokens.shape[1]
            tl = min(SCAN_ROWS, length)
            z_hy, z_qk, z_v, z_gate, z_gdn, z_gdn_gate, z_small = _inproj(
                tokens, shift, scale, norm1_g[l], in_weights, min(PROJ_ROWS, length))
            b_f, b_b = _gla_prep(z_small, gla_w_a2[l], gla_b_a[l], tl)
            gla_f, gla_b, gla_sf, gla_sb = _gla_scan(z_qk, z_v, b_f, b_b, s0_gla[0], s0_gla[1], tl)
            q, k, v, gc = _gdn_prep(z_gdn, gdn_cw, z_small, gdn_a_log[l], gdn_dt_bias[l], gdn_d, seg_len, tl)
            gdn_f, gdn_b, gdn_sf, gdn_sb = _gdn_scan(q, k, v, gc, s0_gdn[0], s0_gdn[1], tl)
            u = _hy_short_conv(z_hy, hy_conv_w[l], hy_conv_b[l], seg_len, tl)
            return u, (gla_f, gla_b, z_gate), (gdn_f, gdn_b, z_gdn_gate), (gla_sf, gla_sb), (gdn_sf, gdn_sb)

        def finish(tokens, gate1, hy, gla, gdn):
            return _outproj(tokens, gate1, hy, *gla, *gdn, gla_bd_gain[l][None, :], gdn_gain, w_hy, w_gla, w_gdn,
                            gla_dv, gdn_d, min(PROJ_ROWS, tokens.shape[1]))

        def mlp(tokens, shift, scale, gate2, final_norm):
            return _mlp(tokens, shift, scale, gate2, norm2_g[l], w1, w2, final_norm_g, final_norm,
                        min(MLP_ROWS, tokens.shape[1]), MLP_HIDDEN)

        u_c, gla_c, gdn_c, gla_s, gdn_s = mixer_parts(ctx, csh1, csc1, ctx_len,
                                                      (zeros_gla, zeros_gla), (zeros_gdn, zeros_gdn))
        u_l, gla_l, gdn_l, _, _ = mixer_parts(x, sh1, sc1, GRID_W, gla_s, gdn_s)
        h_re, h_im = _filter_spectrum(*_hy_filters(emb_lat, *filt_args, min(PROJ_ROWS, seq)), fft_tabs, n1, n2)
        nb = hy_c // LANES
        y1 = _fftconv_gate(u_l, 0, u_l, nb, h_re, h_im, 0, hy_d[l, 0], fft_tabs, n1, n2)
        hy_l = _fftconv_gate(y1, 0, u_l, 2 * nb, h_re, h_im, nb, hy_d[l, 1], fft_tabs, n1, n2)
        x = finish(x, g1, hy_l, gla_l, gdn_l)
        x = mlp(x, sh2, sc2, g2, not with_ctx_out)

        if with_ctx_out:
            coef = _dense_filter_spectrum(*_hy_filters(emb_ctx, *filt_args, ctx_len), dft_tabs)
            y1c = _dftconv_gate(u_c, 0, u_c, 1, *coef, 0, hy_d[l, 0], dft_tabs)
            hy_c_out = _dftconv_gate(y1c, 0, u_c, 2, *coef, 1, hy_d[l, 1], dft_tabs)
            ctx = finish(ctx, cg1, hy_c_out, gla_c, gdn_c)
            ctx = mlp(ctx, csh2, csc2, cg2, False)
    return x
```

```python
import functools
import math

import numpy as np
import jax
import jax.numpy as jnp
from jax import lax
from jax.experimental import pallas as pl
from jax.experimental.pallas import tpu as pltpu

F32 = jnp.float32
BF16 = jnp.bfloat16
HIGHEST = lax.Precision.HIGHEST

NORM_EPS = 1e-6
N_MOD = 6
GRID_W = 64
CHUNK = 64
LANES = 128

HY_ORDER = 2
HY_SHORT = 3
HY_EMB = 33
HY_EMB_PAD = 40
HY_DECAY_TARGET = 1e-2
HY_FAST_PCT = 0.3
HY_SLOW_PCT = 1.5

GLA_H = 4
GLA_RANK = 16
GLA_TAU = 16.0
GDN_H = 4
GDN_DP = LANES
PITCH_PAD = 8
FFT_UNROLL = 8

VMEM_LIMIT = 56 * 1024 * 1024

SCAN_ROWS = 4 * CHUNK
PROJ_ROWS = 512
MLP_ROWS = 1024
MLP_HIDDEN = 1024


def _cparams(n_grid):
    return pltpu.CompilerParams(dimension_semantics=("arbitrary",) * n_grid, vmem_limit_bytes=VMEM_LIMIT)


def _bdot(a, b):
    return jnp.dot(a.astype(BF16), b.astype(BF16), preferred_element_type=F32)


def _bdot_nt(a, b):
    return lax.dot_general(a.astype(BF16), b.astype(BF16), (((1,), (1,)), ((), ())), preferred_element_type=F32)


def _bdot_tn(a, b):
    return lax.dot_general(a.astype(BF16), b.astype(BF16), (((0,), (0,)), ((), ())), preferred_element_type=F32)


def _const_spec(shape):
    return pl.BlockSpec(shape, lambda *_: (0,) * len(shape))


def _mod_kernel(c_ref, w_ref, b_ref, o_ref):
    c = c_ref[...]
    o_ref[0] = _bdot(c * jax.nn.sigmoid(c), w_ref[0]) + b_ref[0]


def _modulation(cc, w_mod, b_mod):
    depth, d, n = w_mod.shape
    tn = n // 4
    return pl.pallas_call(
        _mod_kernel,
        grid=(depth, n // tn),
        in_specs=[pl.BlockSpec(cc.shape, lambda l, j: (0, 0)),
                  pl.BlockSpec((1, d, tn), lambda l, j: (l, 0, j)),
                  pl.BlockSpec((1, 1, tn), lambda l, j: (l, 0, j))],
        out_specs=pl.BlockSpec((1, cc.shape[0], tn), lambda l, j: (l, 0, j)),
        out_shape=jax.ShapeDtypeStruct((depth, cc.shape[0], n), F32),
        compiler_params=_cparams(2),
        name="modulation",
    )(cc, w_mod.astype(BF16), b_mod.reshape(depth, 1, n))


def _rms_modulate(x, g, shift, scale):
    y = x * lax.rsqrt(jnp.mean(x * x, axis=-1, keepdims=True) + NORM_EPS) * g
    return y * (1.0 + scale) + shift


def _inproj_kernel(x_ref, sh_ref, sc_ref, g_ref, w_hy_ref, w_gdn_ref, hy_cw_ref, hy_cb_ref, gdn_cw_ref, *refs,
                   seg_len, q_scale):
    n_plain = (len(refs) - 4) // 2
    w_refs, plain_refs = refs[:n_plain], refs[n_plain:2 * n_plain]
    u_ref, q_ref, k_ref, v_ref = refs[2 * n_plain:]
    hb = _rms_modulate(x_ref[0], g_ref[...], sh_ref[0], sc_ref[0]).astype(BF16)
    for w_ref, o_ref in zip(w_refs, plain_refs):
        o_ref[0] = jnp.dot(hb, w_ref[...], preferred_element_type=F32)
    piece = 2 * LANES
    for c0 in range(0, u_ref.shape[-1], piece):
        cols = slice(c0, c0 + piece)
        z = jnp.dot(hb, w_hy_ref[:, cols], preferred_element_type=F32)
        u_ref[0, :, cols] = _seg_conv(z, hy_cw_ref[:, cols], seg_len) + hy_cb_ref[:, cols]
    width = q_ref.shape[-1]
    for part, (o_ref, norm_scale) in enumerate(((q_ref, q_scale), (k_ref, 1.0), (v_ref, None))):
        for c0 in range(0, width, piece):
            cols = slice(part * width + c0, part * width + c0 + piece)
            y = _seg_conv(jnp.dot(hb, w_gdn_ref[:, cols], preferred_element_type=F32), gdn_cw_ref[:, cols], seg_len)
            y = y * jax.nn.sigmoid(y)
            for h0 in range(0, piece, GDN_DP):
                t = y[:, h0:h0 + GDN_DP]
                if norm_scale is not None:
                    t = t * (lax.rsqrt(jnp.sum(t * t, axis=-1, keepdims=True) + NORM_EPS) * norm_scale)
                o_ref[0, :, c0 + h0:c0 + h0 + GDN_DP] = t


def _inproj(x, shift, scale, g, plain_weights, w_hy, w_gdn, hy_cw, hy_cb, gdn_cw, seg_len, q_scale, tm):
    b, l, d = x.shape
    hyw, gdw = w_hy.shape[1], w_gdn.shape[1] // 3
    vec = pl.BlockSpec((1, 1, d), lambda i, j: (i, 0, 0))
    widths = [w.shape[1] for w in plain_weights] + [hyw, gdw, gdw, gdw]
    return pl.pallas_call(
        functools.partial(_inproj_kernel, seg_len=seg_len, q_scale=q_scale),
        grid=(b, l // tm),
        in_specs=[pl.BlockSpec((1, tm, d), lambda i, j: (i, j, 0)), vec, vec, _const_spec((1, d)),
                  _const_spec(w_hy.shape), _const_spec(w_gdn.shape), _const_spec(hy_cw.shape),
                  _const_spec((1, hyw)), _const_spec(gdn_cw.shape)]
        + [_const_spec(w.shape) for w in plain_weights],
        out_specs=[pl.BlockSpec((1, tm, w), lambda i, j: (i, j, 0)) for w in widths],
        out_shape=[jax.ShapeDtypeStruct((b, l, w), F32) for w in widths],
        compiler_params=_cparams(2),
        name="inproj",
    )(x, shift, scale, g.reshape(1, d), w_hy, w_gdn, hy_cw, hy_cb.reshape(1, hyw), gdn_cw, *plain_weights)


def _seg_conv(z, w, seg_len):
    rows, width = z.shape
    taps = w.shape[0]
    half = taps // 2
    pos = lax.broadcasted_iota(jnp.int32, (rows, LANES), 0) % seg_len
    valid = {d: (pos >= -d) if d < 0 else (pos < seg_len - d) for d in range(-half, half + 1) if d != 0}
    tiles = []
    for c0 in range(0, width, LANES):
        cols = slice(c0, c0 + LANES)
        zc = z[:, cols]
        acc = zc * w[half:half + 1, cols]
        for k in range(taps):
            d = k - half
            if d != 0:
                shifted = pltpu.roll(zc, (-d) % rows, 0)
                acc = acc + jnp.where(valid[d], shifted, 0.0) * w[k:k + 1, cols]
        tiles.append(acc)
    return jnp.concatenate(tiles, axis=1)


def _hy_filter_kernel(z_ref, w1_ref, b1_ref, w2_ref, b2_ref, w3_ref, f_ref, dl_ref, hf_ref, hb_ref):
    hy_c = dl_ref.shape[1]
    z = z_ref[...]
    fdot = lambda a, b: jnp.dot(a, b, precision=HIGHEST, preferred_element_type=F32)
    h = jnp.sin(f_ref[0:1, :] * (fdot(z, w1_ref[...]) + b1_ref[...]))
    h = jnp.sin(f_ref[1:2, :] * (fdot(h, w2_ref[...]) + b2_ref[...]))
    h = fdot(h, w3_ref[...])
    window = jnp.exp(-z[:, 0:1] * dl_ref[...])
    row = lax.broadcasted_iota(jnp.int32, window.shape, 0) + pl.program_id(0) * z.shape[0]
    for o in range(HY_ORDER):
        base = 2 * o * hy_c
        hf_ref[:, o * hy_c:(o + 1) * hy_c] = h[:, base:base + hy_c] * window
        hb_ref[:, o * hy_c:(o + 1) * hy_c] = jnp.where(row > 0, h[:, base + hy_c:base + 2 * hy_c] * window, 0.0)


def _hy_filters(z, w1, b1, w2, b2, w3, sin_freq, deltas, tl):
    l, emb = z.shape
    fh = w2.shape[0]
    hy_c = deltas.shape[1]
    out = jax.ShapeDtypeStruct((l, HY_ORDER * hy_c), F32)
    rows = lambda w: pl.BlockSpec((tl, w), lambda i: (i, 0))
    return pl.pallas_call(
        _hy_filter_kernel,
        grid=(l // tl,),
        in_specs=[rows(emb), _const_spec((emb, fh)), _const_spec((1, fh)), _const_spec((fh, fh)), _const_spec((1, fh)),
                  _const_spec(w3.shape), _const_spec((2, fh)), _const_spec((1, hy_c))],
        out_specs=[rows(HY_ORDER * hy_c)] * 2,
        out_shape=[out, out],
        compiler_params=_cparams(1),
        name="hy_filters",
    )(z, jnp.pad(w1, ((0, emb - w1.shape[0]), (0, 0))), b1.reshape(1, fh), w2, b2.reshape(1, fh), w3, sin_freq, deltas)


def _hy_embedding(length, emb):
    t = jnp.linspace(0.0, 1.0, length, dtype=F32)[:, None]
    bands = (HY_EMB - 1) // 2
    f = jnp.linspace(1e-4, bands - 1, bands, dtype=F32)
    w = 2 * math.pi * jnp.arange(length, dtype=F32) / length
    ang = w[:, None] * f[None, :]
    z = jnp.concatenate([t, jnp.cos(ang), -jnp.sin(ang)], axis=-1)
    return jnp.pad(z, ((0, 0), (0, emb - z.shape[1])))


def _fft_tables(n1, n2):
    n = n1 * n2
    s1 = np.arange(n1 // 2)[None, :, None]
    s2 = np.arange(n2)[:, None, None]
    k1 = np.arange(n1)[None, None, :]
    theta = 2.0 * np.pi * ((k1 * (n2 * s1 + s2)) % n) / n
    tt = np.concatenate([np.cos(theta), -np.sin(theta)], axis=-1)
    phi = 2.0 * np.pi * ((np.arange(n2)[:, None] * np.arange(n2)[None, :]) % n2) / n2
    c2, s2m = np.cos(phi), np.sin(phi)
    fwd = np.block([[c2, s2m], [-s2m, c2]])
    inv = np.block([[c2, -s2m], [s2m, c2]])
    return jnp.asarray(tt, F32), jnp.asarray(fwd, F32), jnp.asarray(inv, F32)


def _time_pitch(n2):
    return n2 + PITCH_PAD


def _spec_pitch(n2):
    return 2 * n2 + PITCH_PAD


def _dft_load_time(load_block, time_ref, n1, n2):
    def copy(s1, carry):
        time_ref[pl.ds(pl.multiple_of(s1 * _time_pitch(n2), 8), n2), :] = load_block(
            pl.ds(pl.multiple_of(s1 * n2, n2), n2))
        return carry

    lax.fori_loop(0, n1 // 2, copy, 0, unroll=min(FFT_UNROLL, n1 // 2))


def _dft_stage_a(time_ref, tt_ref, spec_ref, n1, n2):
    def stage_a(s2, carry):
        slab = time_ref[pl.ds(s2, n1 // 2, stride=_time_pitch(n2)), :]
        res = _bdot_tn(tt_ref[s2], slab)
        spec_ref[pl.ds(s2, n1, stride=_spec_pitch(n2)), :] = res[:n1]
        spec_ref[pl.ds(n2 + s2, n1, stride=_spec_pitch(n2)), :] = res[n1:]
        return carry

    lax.fori_loop(0, n2, stage_a, 0, unroll=FFT_UNROLL)


def _spec_rows(k1, n2):
    return pl.ds(pl.multiple_of(k1 * _spec_pitch(n2), 8), 2 * n2)


def _spectrum_kernel(hf_ref, hb_ref, tt_ref, fwd_ref, hr_ref, hi_ref, time_ref, spec_ref, *, n1, n2):
    inv_n = 1.0 / (n1 * n2)
    fwd = fwd_ref[...].astype(BF16)
    for sign, src_ref in ((1.0, hf_ref), (-1.0, hb_ref)):
        _dft_load_time(lambda rows: src_ref[rows, :], time_ref, n1, n2)
        _dft_stage_a(time_ref, tt_ref, spec_ref, n1, n2)

        def stage_b(k1, carry):
            rows = _spec_rows(k1, n2)
            spec = jnp.dot(fwd, spec_ref[rows, :].astype(BF16), preferred_element_type=F32) * inv_n
            hrows = pl.ds(pl.multiple_of(k1 * n2, n2), n2)
            if sign > 0:
                hr_ref[hrows, :] = spec[:n2]
                hi_ref[hrows, :] = spec[n2:]
            else:
                hr_ref[hrows, :] += spec[:n2]
                hi_ref[hrows, :] -= spec[n2:]
            return carry

        lax.fori_loop(0, n1, stage_b, 0, unroll=min(FFT_UNROLL, n1))


def _filter_spectrum(hf, hb, tables, n1, n2):
    tt, fwd, _ = tables
    l, c = hf.shape
    n = n1 * n2
    chan = lambda rows: pl.BlockSpec((rows, LANES), lambda j: (0, j))
    return pl.pallas_call(
        functools.partial(_spectrum_kernel, n1=n1, n2=n2),
        grid=(c // LANES,),
        in_specs=[chan(l), chan(l), _const_spec(tt.shape), _const_spec(fwd.shape)],
        out_specs=[chan(n), chan(n)],
        out_shape=[jax.ShapeDtypeStruct((n, c), F32)] * 2,
        scratch_shapes=_fft_scratch(n1, n2),
        compiler_params=_cparams(1),
        name="hy_spectrum",
    )(hf, hb, tt, fwd)


def _fft_scratch(n1, n2):
    return [pltpu.VMEM((n1 // 2 * _time_pitch(n2), LANES), F32), pltpu.VMEM((n1 * _spec_pitch(n2), LANES), F32)]


def _fftconv_kernel(u_ref, gate_ref, hr_ref, hi_ref, d_ref, tt_ref, fwd_ref, inv_ref, o_ref, time_ref, spec_ref,
                    *, n1, n2):
    n1h = n1 // 2
    fwd = fwd_ref[...].astype(BF16)
    inv = inv_ref[...].astype(BF16)
    _dft_load_time(lambda rows: u_ref[0, rows, :], time_ref, n1, n2)
    _dft_stage_a(time_ref, tt_ref, spec_ref, n1, n2)

    def stage_b(k1, carry):
        rows = _spec_rows(k1, n2)
        spec = jnp.dot(fwd, spec_ref[rows, :].astype(BF16), preferred_element_type=F32)
        br, bi = spec[:n2], spec[n2:]
        hrows = pl.ds(pl.multiple_of(k1 * n2, n2), n2)
        hr, hi = hr_ref[hrows, :], hi_ref[hrows, :]
        y = jnp.concatenate([br * hr - bi * hi, br * hi + bi * hr], axis=0)
        spec_ref[rows, :] = jnp.dot(inv, y.astype(BF16), preferred_element_type=F32)
        return carry

    lax.fori_loop(0, n1, stage_b, 0, unroll=min(FFT_UNROLL, n1))

    def stage_a_inv(s2, carry):
        g = jnp.concatenate([spec_ref[pl.ds(s2, n1, stride=_spec_pitch(n2)), :],
                             spec_ref[pl.ds(n2 + s2, n1, stride=_spec_pitch(n2)), :]], axis=0)
        time_ref[pl.ds(s2, n1h, stride=_time_pitch(n2)), :] = jnp.dot(
            tt_ref[s2], g.astype(BF16), preferred_element_type=F32)
        return carry

    lax.fori_loop(0, n2, stage_a_inv, 0, unroll=FFT_UNROLL)

    def gate_rows(s1, carry):
        rows = pl.ds(pl.multiple_of(s1 * n2, n2), n2)
        conv = time_ref[pl.ds(pl.multiple_of(s1 * _time_pitch(n2), 8), n2), :]
        o_ref[0, rows, :] = gate_ref[0, rows, :] * (conv + d_ref[...] * u_ref[0, rows, :])
        return carry

    lax.fori_loop(0, n1h, gate_rows, 0, unroll=min(FFT_UNROLL, n1h))


def _fftconv_gate(u_arr, u_blk, gate_arr, gate_blk, hr, hi, h_blk, d, tables, n1, n2):
    tt, fwd, inv = tables
    b, l, _ = u_arr.shape
    n = hr.shape[0]
    c = d.shape[0]
    ncb = c // LANES
    seq = lambda off: pl.BlockSpec((1, l, LANES), lambda j, i: (i, 0, off + j))
    chan = lambda rows, off=0: pl.BlockSpec((rows, LANES), lambda j, i: (0, off + j))
    return pl.pallas_call(
        functools.partial(_fftconv_kernel, n1=n1, n2=n2),
        grid=(ncb, b),
        in_specs=[seq(u_blk), seq(gate_blk), chan(n, h_blk), chan(n, h_blk), chan(1),
                  _const_spec(tt.shape), _const_spec(fwd.shape), _const_spec(inv.shape)],
        out_specs=pl.BlockSpec((1, l, LANES), lambda j, i: (i, 0, j)),
        out_shape=jax.ShapeDtypeStruct((b, l, c), F32),
        scratch_shapes=_fft_scratch(n1, n2),
        compiler_params=_cparams(2),
        name="hy_fftconv",
    )(u_arr, gate_arr, hr, hi, d.reshape(1, c), tt, fwd, inv)


def _dft_tables(l):
    n = 2 * l
    ang = 2.0 * np.pi * ((np.arange(l)[:, None] * np.arange(l)[None, :]) % n) / n
    cf = np.cos(ang)
    sf = -np.sin(ang)
    sf[0, :] = (-1.0) ** np.arange(l)
    fwd = np.concatenate([cf, sf], axis=0)
    return jnp.asarray(fwd, F32), jnp.asarray(fwd.T, F32)


def _dftconv_kernel(u_ref, gate_ref, p_ref, q_ref, r_ref, d_ref, fwd_ref, inv_ref, o_ref):
    u = u_ref[0]
    l = u.shape[0]
    spec = _bdot(fwd_ref[...], u)
    xr, xi = spec[:l], spec[l:]
    q = q_ref[...]
    y = jnp.concatenate([xr * p_ref[...] - xi * q, xr * q + xi * r_ref[...]], axis=0)
    conv = _bdot(inv_ref[...], y)
    o_ref[0] = gate_ref[0] * (conv + d_ref[...] * u)


def _dense_spectrum_kernel(hf_ref, hb_ref, fwd_ref, p_ref, q_ref, r_ref):
    l = hf_ref.shape[0]
    xf = _bdot(fwd_ref[...], hf_ref[...])
    xb = _bdot(fwd_ref[...], hb_ref[...])
    first = lax.broadcasted_iota(jnp.int32, (l, hf_ref.shape[1]), 0) == 0
    scale = jnp.where(first, 0.5 / l, 1.0 / l)
    re = (xf[:l] + xb[:l]) * scale
    p_ref[...] = re
    q_ref[...] = jnp.where(first, 0.0, (xf[l:] - xb[l:]) * scale)
    r_ref[...] = jnp.where(first, (xf[l:] + xb[l:]) * scale, re)


def _dense_filter_spectrum(hf, hb, tables):
    fwd, _ = tables
    out = jax.ShapeDtypeStruct(hf.shape, F32)
    return pl.pallas_call(
        _dense_spectrum_kernel,
        grid=(1,),
        in_specs=[_const_spec(hf.shape), _const_spec(hb.shape), _const_spec(fwd.shape)],
        out_specs=[_const_spec(hf.shape)] * 3,
        out_shape=[out] * 3,
        compiler_params=_cparams(1),
        name="hy_dense_spectrum",
    )(hf, hb, fwd)


def _dftconv_gate(u_arr, u_blk, gate_arr, gate_blk, p, q, r, h_blk, d, tables):
    fwd, inv = tables
    b, l, _ = u_arr.shape
    c = d.shape[0]
    seq = lambda off: pl.BlockSpec((1, l, c), lambda i: (i, 0, off))
    coef = pl.BlockSpec((l, c), lambda i: (0, h_blk))
    return pl.pallas_call(
        _dftconv_kernel,
        grid=(b,),
        in_specs=[seq(u_blk), seq(gate_blk), coef, coef, coef,
                  _const_spec((1, c)), _const_spec(fwd.shape), _const_spec(inv.shape)],
        out_specs=pl.BlockSpec((1, l, c), lambda i: (i, 0, 0)),
        out_shape=jax.ShapeDtypeStruct((b, l, c), F32),
        compiler_params=_cparams(1),
        name="hy_dftconv",
    )(u_arr, gate_arr, p, q, r, d.reshape(1, c), fwd, inv)


GDN_GF, GDN_GB, GDN_BF, GDN_BB = 32, 36, 40, 44

def _chunk_cumsum_mats(tl):
    r = np.arange(tl)
    same = (r[:, None] // CHUNK) == (r[None, :] // CHUNK)
    lower = same & (r[None, :] <= r[:, None])
    upper = same & (r[None, :] >= r[:, None])
    return jnp.asarray(lower, F32), jnp.asarray(upper, F32)


def _gates_kernel(small_ref, w_ref, b_ref, nea_ref, dtb_ref, lo_ref, up_ref, bf_ref, bb_ref, gc_ref):
    small = small_ref[0]
    fdot = lambda a, b: jnp.dot(a, b, precision=HIGHEST, preferred_element_type=F32)
    for d, (tri_ref, o_ref) in enumerate(((lo_ref, bf_ref), (up_ref, bb_ref))):
        pre = fdot(small[:, d * GLA_RANK:(d + 1) * GLA_RANK], w_ref[d]) + b_ref[d]
        o_ref[0] = fdot(tri_ref[...], jax.nn.log_sigmoid(pre) * (1.0 / GLA_TAU))
    log_decay = nea_ref[...] * jax.nn.softplus(small + dtb_ref[...])
    cum_f = fdot(lo_ref[...], log_decay)
    cum_b = fdot(up_ref[...], log_decay)
    lane = lax.broadcasted_iota(jnp.int32, small.shape, 1)
    gc_ref[0] = jnp.where(lane < GDN_GB, cum_f, jnp.where(lane < GDN_BF, cum_b, jax.nn.sigmoid(small)))


def _gates(small, w_a2, b_a, a_log, dt_bias, tl):
    b, l, sw = small.shape
    kw = w_a2.shape[-1]
    nea = jnp.zeros((1, LANES), F32).at[0, GDN_GF:GDN_BF].set(-jnp.exp(a_log.reshape(-1)))
    dtb = jnp.zeros((1, LANES), F32).at[0, GDN_GF:GDN_BF].set(dt_bias.reshape(-1))
    lower, upper = _chunk_cumsum_mats(tl)
    tok = lambda w: pl.BlockSpec((1, tl, w), lambda i, j: (i, j, 0))
    return pl.pallas_call(
        _gates_kernel,
        grid=(b, l // tl),
        in_specs=[tok(sw), _const_spec(w_a2.shape), _const_spec((2, 1, kw)), _const_spec((1, LANES)),
                  _const_spec((1, LANES)), _const_spec((tl, tl)), _const_spec((tl, tl))],
        out_specs=[tok(kw), tok(kw), tok(LANES)],
        out_shape=[jax.ShapeDtypeStruct((b, l, kw), F32)] * 2 + [jax.ShapeDtypeStruct((b, l, LANES), F32)],
        compiler_params=_cparams(2),
        name="scan_gates",
    )(small, w_a2, b_a.reshape(2, 1, kw), nea, dtb, lower, upper)


def _gla_scan_kernel(qkf_ref, vf_ref, bf_ref, qkb_ref, vb_ref, bb_ref, s0f_ref, s0b_ref, hm_ref, vm_ref, bd_ref,
                     of_ref, ob_ref, sff_ref, sfb_ref, stf_ref, stb_ref, *, n_chunks, kw, q_scale):
    j = pl.program_id(1)

    @pl.when(j == 0)
    def _():
        stf_ref[...] = s0f_ref[0]
        stb_ref[...] = s0b_ref[0]

    ri = lax.broadcasted_iota(jnp.int32, (CHUNK, GLA_H * CHUNK), 0)
    ci = lax.broadcasted_iota(jnp.int32, (CHUNK, GLA_H * CHUNK), 1) % CHUNK
    hm = hm_ref[...]
    vm = vm_ref[...]
    bd = bd_ref[...]
    dirs = ((qkf_ref, vf_ref, bf_ref, of_ref, stf_ref, ri >= ci, CHUNK - 1, CHUNK // 2 - 1, False),
            (qkb_ref, vb_ref, bb_ref, ob_ref, stb_ref, ci >= ri, 0, CHUNK // 2, True))

    qts, kstacks, vstacks, keeps, qes, vs, kds, decs, sinks = [], [], [], [], [], [], [], [], []
    for qk_ref, v_ref, b_ref, o_ref, st_ref, keep, last, mid, rev in dirs:
        for c in range(n_chunks):
            rows = slice((n_chunks - 1 - c if rev else c) * CHUNK, (n_chunks - c if rev else c + 1) * CHUNK)
            qk = qk_ref[0, rows, :]
            q, k = qk[:, :kw] * q_scale, qk[:, kw:]
            v = v_ref[0, rows, :]
            bc = b_ref[0, rows, :]
            b_mid = bc[mid:mid + 1, :]
            b_last = bc[last:last + 1, :]
            kt = k * jnp.exp(b_mid - bc)
            qts.append(q * jnp.exp(bc - b_mid))
            kstacks.append(jnp.concatenate([kt * hm[h:h + 1, :] for h in range(GLA_H)], axis=0).astype(BF16))
            vstacks.append(jnp.concatenate([v * vm[h:h + 1, :] for h in range(GLA_H)], axis=0).astype(BF16))
            keeps.append(keep)
            qes.append((q * jnp.exp(bc)).astype(BF16))
            vs.append(v)
            kds.append(k * jnp.exp(b_last - bc))
            decs.append(jnp.exp(b_last))
            sinks.append((o_ref, rows))
    attns = [jnp.where(keep, _bdot_nt(qt, ks), 0.0) for keep, qt, ks in zip(keeps, qts, kstacks)]
    o_intra = [_bdot(a, vst) for a, vst in zip(attns, vstacks)]
    upds = [_bdot_tn(v, kd) * bd for v, kd in zip(vs, kds)]

    for d, (_, _, _, _, st_ref, _, _, _, _) in enumerate(dirs):
        st = st_ref[...]
        for c in range(n_chunks):
            i = d * n_chunks + c
            o_ref, rows = sinks[i]
            o_ref[0, rows, :] = o_intra[i] + _bdot_nt(qes[i], st)
            st = st * decs[i] + upds[i]
        st_ref[...] = st

    @pl.when(j == pl.num_programs(1) - 1)
    def _():
        sff_ref[0] = stf_ref[...]
        sfb_ref[0] = stb_ref[...]


def _gla_scan(qk, v, b_f, b_b, s0_f, s0_b, tl):
    b, l, kw2 = qk.shape
    kw, vw = kw2 // 2, v.shape[-1]
    dk, dv = kw // GLA_H, vw // GLA_H
    nblk = l // tl
    heads_k = np.arange(kw) // dk
    heads_v = np.arange(vw) // dv
    hm = jnp.asarray(np.arange(8)[:, None] == heads_k[None, :], F32)
    vm = jnp.asarray(np.arange(8)[:, None] == heads_v[None, :], F32)
    bd = jnp.asarray(heads_v[:, None] == heads_k[None, :], F32)
    fwd = lambda w: pl.BlockSpec((1, tl, w), lambda i, j: (i, j, 0))
    bwd = lambda w: pl.BlockSpec((1, tl, w), lambda i, j: (i, nblk - 1 - j, 0))
    state = pl.BlockSpec((1, vw, kw), lambda i, j: (i, 0, 0))
    return pl.pallas_call(
        functools.partial(_gla_scan_kernel, n_chunks=tl // CHUNK, kw=kw, q_scale=dk ** -0.5),
        grid=(b, nblk),
        in_specs=[fwd(kw2), fwd(vw), fwd(kw), bwd(kw2), bwd(vw), bwd(kw), state, state,
                  _const_spec(hm.shape), _const_spec(vm.shape), _const_spec(bd.shape)],
        out_specs=[fwd(vw), bwd(vw), state, state],
        out_shape=[jax.ShapeDtypeStruct((b, l, vw), F32)] * 2 + [jax.ShapeDtypeStruct((b, vw, kw), F32)] * 2,
        scratch_shapes=[pltpu.VMEM((vw, kw), F32)] * 2,
        compiler_params=_cparams(2),
        name="gla_scan",
    )(qk, v, b_f, qk, v, b_b, s0_f, s0_b, hm, vm, bd)


def _unit_tri_inverses(mats, eye, m16, m32, m64):
    diag = [a * m16 for a in mats]
    inv = [eye + a for a in diag]
    pw = diag
    for _ in range(3):
        pw = [_bdot(p, p) for p in pw]
        inv = [t + _bdot(t, p) for t, p in zip(inv, pw)]
    for mask in (m32, m64):
        mid = [_bdot(t, a * mask) for t, a in zip(inv, mats)]
        inv = [t + _bdot(m, t) for t, m in zip(inv, mid)]
    return inv


def _gdn_scan_kernel(qf_ref, kf_ref, vf_ref, gcf_ref, qb_ref, kb_ref, vb_ref, gcb_ref,
                     s0f_ref, s0b_ref, of_ref, ob_ref, sff_ref, sfb_ref, sf_ref, sb_ref, *, tl):
    j = pl.program_id(1)

    @pl.when(j == 0)
    def _():
        sf_ref[...] = s0f_ref[0]
        sb_ref[...] = s0b_ref[0]

    nck = tl // CHUNK
    ri = lax.broadcasted_iota(jnp.int32, (tl, tl), 0)
    ci = lax.broadcasted_iota(jnp.int32, (tl, tl), 1)
    same = lambda n: (ri // n) == (ci // n)
    chunk = same(CHUNK)
    eye = (ri == ci).astype(F32)
    m16 = same(16).astype(F32)
    m32 = (same(32) & ~same(16)).astype(F32)
    m64 = (chunk & ~same(32)).astype(F32)
    dirs = ((qf_ref, kf_ref, vf_ref, gcf_ref, of_ref, sf_ref, ri > ci, ri >= ci, CHUNK - 1, GDN_GF, GDN_BF, False),
            (qb_ref, kb_ref, vb_ref, gcb_ref, ob_ref, sb_ref, ci > ri, ci >= ri, 0, GDN_GB, GDN_BB, True))

    qs, ks, vs, gcums, betas, stricts, incls, glasts, outs = [], [], [], [], [], [], [], [], []
    for q_ref, k_ref, v_ref, gc_ref, o_ref, s_ref, strict, incl, last, g_col, b_col, rev in dirs:
        gc = gc_ref[0]
        gct = gc.T
        for h in range(GDN_H):
            cols = slice(h * GDN_DP, (h + 1) * GDN_DP)
            gcum = gc[:, g_col + h:g_col + h + 1]
            decay = jnp.exp(jnp.minimum(gcum - gct[g_col + h:g_col + h + 1, :], 0.0))
            qs.append(q_ref[0, :, cols])
            ks.append(k_ref[0, :, cols])
            vs.append(v_ref[0, :, cols])
            gcums.append(gcum)
            betas.append(gc[:, b_col + h:b_col + h + 1])
            stricts.append(jnp.where(chunk & strict, decay, 0.0))
            incls.append(jnp.where(chunk & incl, decay, 0.0))
            glasts.append(jnp.concatenate(
                [jnp.broadcast_to(gcum[c * CHUNK + last:c * CHUNK + last + 1, :], (CHUNK, 1)) for c in range(nck)],
                axis=0))
            outs.append((o_ref, s_ref, h, cols, rev))
    n_pairs = len(qs)
    e_cols = [jnp.exp(g) for g in gcums]
    kbs = [k * b for k, b in zip(ks, betas)]
    mats = [-_bdot_nt(kb, k) * d for kb, k, d in zip(kbs, ks, stricts)]
    invs = _unit_tri_inverses(mats, eye, m16, m32, m64)
    wus = [_bdot(t, jnp.concatenate([kb * e, v * b], axis=1))
           for t, kb, e, v, b in zip(invs, kbs, e_cols, vs, betas)]
    attns = [_bdot_nt(q, k) * d for q, k, d in zip(qs, ks, incls)]
    awus = [_bdot(a, wu) for a, wu in zip(attns, wus)]
    q_effs = [(q * e - awu[:, :GDN_DP]).astype(BF16) for q, e, awu in zip(qs, e_cols, awus)]
    ws = [wu[:, :GDN_DP].astype(BF16) for wu in wus]
    us = [wu[:, GDN_DP:] for wu in wus]
    o_intra = [awu[:, GDN_DP:] for awu in awus]
    kds = [(k * jnp.exp(gl - g)).astype(BF16) for k, gl, g in zip(ks, glasts, gcums)]
    decs = [jnp.exp(gl) for gl in glasts]

    states = [s_ref[h] for (_, s_ref, h, _, _) in outs]
    for c in range(nck):
        rows = [slice((nck - 1 - c if rev else c) * CHUNK, (nck - c if rev else c + 1) * CHUNK)
                for (_, _, _, _, rev) in outs]
        prods = [jnp.dot(jnp.concatenate([q_effs[i][rows[i]], ws[i][rows[i]]], axis=0), states[i].astype(BF16),
                         preferred_element_type=F32) for i in range(n_pairs)]
        for i, (o_ref, _, _, cols, _) in enumerate(outs):
            o_ref[0, rows[i], cols] = o_intra[i][rows[i]] + prods[i][:CHUNK]
        v_news = [us[i][rows[i]] - prods[i][CHUNK:] for i in range(n_pairs)]
        states = [decs[i][rows[i]][0:1] * states[i] + _bdot_tn(kds[i][rows[i]], v_news[i]) for i in range(n_pairs)]
    for i, (_, s_ref, h, _, _) in enumerate(outs):
        s_ref[h] = states[i]

    @pl.when(j == pl.num_programs(1) - 1)
    def _():
        sff_ref[0] = sf_ref[...]
        sfb_ref[0] = sb_ref[...]


def _gdn_scan(q, k, v, gc, s0_f, s0_b, tl):
    b, l, width = q.shape
    nblk = l // tl
    fwd = lambda w: pl.BlockSpec((1, tl, w), lambda i, j: (i, j, 0))
    bwd = lambda w: pl.BlockSpec((1, tl, w), lambda i, j: (i, nblk - 1 - j, 0))
    state = pl.BlockSpec((1, GDN_H, GDN_DP, GDN_DP), lambda i, j: (i, 0, 0, 0))
    st_shape = jax.ShapeDtypeStruct((b, GDN_H, GDN_DP, GDN_DP), F32)
    return pl.pallas_call(
        functools.partial(_gdn_scan_kernel, tl=tl),
        grid=(b, nblk),
        in_specs=[fwd(width), fwd(width), fwd(width), fwd(LANES),
                  bwd(width), bwd(width), bwd(width), bwd(LANES), state, state],
        out_specs=[fwd(width), bwd(width), state, state],
        out_shape=[jax.ShapeDtypeStruct((b, l, width), F32)] * 2 + [st_shape] * 2,
        scratch_shapes=[pltpu.VMEM((GDN_H, GDN_DP, GDN_DP), F32)] * 2,
        compiler_params=_cparams(2),
        name="gdn_scan",
    )(q, k, v, gc, q, k, v, gc, s0_f, s0_b)


def _gated_head_norm(o, gate, g_norm, ones_bd, inv_d):
    sq = o * o
    hi = sq.astype(BF16)
    lo = (sq - hi.astype(F32)).astype(BF16)
    ms = (jnp.dot(hi, ones_bd, preferred_element_type=F32) + jnp.dot(lo, ones_bd, preferred_element_type=F32)) * inv_d
    return o * lax.rsqrt(ms + NORM_EPS) * g_norm * (gate * jax.nn.sigmoid(gate))


def _outproj_kernel(x_ref, g1_ref, hy_ref, glf_ref, glb_ref, glg_ref, gdf_ref, gdb_ref, gdg_ref,
                    gln_ref, gdn_ref, glm_ref, gdm_ref, why_ref, wgl_ref, wgd_ref, o_ref, *, gla_dv, gdn_d):
    y_gla = _gated_head_norm(glf_ref[0] + glb_ref[0], glg_ref[0], gln_ref[...], glm_ref[...], 1.0 / gla_dv)
    y_gdn = _gated_head_norm(gdf_ref[0] + gdb_ref[0], gdg_ref[0], gdn_ref[...], gdm_ref[...], 1.0 / gdn_d)
    acc = _bdot(hy_ref[0], why_ref[...]) + _bdot(y_gla, wgl_ref[...]) + _bdot(y_gdn, wgd_ref[...])
    o_ref[0] = x_ref[0] + g1_ref[0] * acc


def _outproj(x, g1, hy, gla_f, gla_b, gla_gate, gdn_f, gdn_b, gdn_gate, gla_g, gdn_g, w_hy, w_gla, w_gdn,
             gla_dv, gdn_d, tm):
    b, l, d = x.shape
    hyw, glw, gdw = hy.shape[-1], gla_f.shape[-1], gdn_f.shape[-1]
    gl_heads = np.arange(glw) // gla_dv
    gd_heads = np.arange(gdw) // GDN_DP
    gl_m = jnp.asarray(gl_heads[:, None] == gl_heads[None, :], BF16)
    gd_m = jnp.asarray(gd_heads[:, None] == gd_heads[None, :], BF16)
    tok = lambda w: pl.BlockSpec((1, tm, w), lambda i, j: (i, j, 0))
    return pl.pallas_call(
        functools.partial(_outproj_kernel, gla_dv=gla_dv, gdn_d=gdn_d),
        grid=(b, l // tm),
        in_specs=[tok(d), pl.BlockSpec((1, 1, d), lambda i, j: (i, 0, 0)), tok(hyw), tok(glw), tok(glw), tok(glw),
                  tok(gdw), tok(gdw), tok(gdw), _const_spec((1, glw)), _const_spec((1, gdw)),
                  _const_spec(gl_m.shape), _const_spec(gd_m.shape),
                  _const_spec(w_hy.shape), _const_spec(w_gla.shape), _const_spec(w_gdn.shape)],
        out_specs=tok(d),
        out_shape=jax.ShapeDtypeStruct(x.shape, F32),
        compiler_params=_cparams(2),
        name="outproj",
    )(x, g1, hy, gla_f, gla_b, gla_gate, gdn_f, gdn_b, gdn_gate, gla_g, gdn_g, gl_m, gd_m, w_hy, w_gla, w_gdn)


def _mlp_kernel(x_ref, sh_ref, sc_ref, g2_ref, ng_ref, w1_ref, w2_ref, fg_ref, o_ref, hn_ref, acc_ref, *, final_norm):
    kk = pl.program_id(2)

    @pl.when(kk == 0)
    def _():
        hn_ref[...] = _rms_modulate(x_ref[0], ng_ref[...], sh_ref[0], sc_ref[0]).astype(BF16)
        acc_ref[...] = jnp.zeros_like(acc_ref)

    hid = jnp.maximum(jnp.dot(hn_ref[...], w1_ref[...], preferred_element_type=F32), 0.0)
    acc_ref[...] += _bdot(hid * hid, w2_ref[...])

    @pl.when(kk == pl.num_programs(2) - 1)
    def _():
        y = x_ref[0] + g2_ref[0] * acc_ref[...]
        if final_norm:
            y = y * lax.rsqrt(jnp.mean(y * y, axis=-1, keepdims=True) + NORM_EPS) * fg_ref[...]
        o_ref[0] = y


def _mlp(x, shift, scale, gate, norm_g, w1, w2, final_g, final_norm, tm, th):
    b, l, d = x.shape
    dff = w1.shape[1]
    vec = pl.BlockSpec((1, 1, d), lambda i, j, k: (i, 0, 0))
    row = pl.BlockSpec((1, d), lambda i, j, k: (0, 0))
    return pl.pallas_call(
        functools.partial(_mlp_kernel, final_norm=final_norm),
        grid=(b, l // tm, dff // th),
        in_specs=[pl.BlockSpec((1, tm, d), lambda i, j, k: (i, j, 0)), vec, vec, vec, row,
                  pl.BlockSpec((d, th), lambda i, j, k: (0, k)), pl.BlockSpec((th, d), lambda i, j, k: (k, 0)), row],
        out_specs=pl.BlockSpec((1, tm, d), lambda i, j, k: (i, j, 0)),
        out_shape=jax.ShapeDtypeStruct(x.shape, F32),
        scratch_shapes=[pltpu.VMEM((tm, d), BF16), pltpu.VMEM((tm, d), F32)],
        compiler_params=_cparams(3),
        name="mlp",
    )(x, shift, scale, gate, norm_g.reshape(1, d), w1, w2, final_g.reshape(1, d))


def _pad_heads(w, n_heads, axis):
    shape = w.shape
    d = shape[axis] // n_heads
    w = w.reshape(shape[:axis] + (n_heads, d) + shape[axis + 1:])
    pad = [(0, 0)] * w.ndim
    pad[axis + 1] = (0, GDN_DP - d)
    w = jnp.pad(w, pad)
    return w.reshape(shape[:axis] + (n_heads * GDN_DP,) + shape[axis + 1:])


def kernel(x, c, ctx, c_ctx, norm1_g, norm2_g, w_mod, b_mod, w_in, w_out, hy_conv_w, hy_conv_b, hy_f_w1, hy_f_b1, hy_f_w2, hy_f_b2, hy_f_w3, hy_sin_freq, hy_d, gla_w_a2, gla_b_a, gla_norm_g, gdn_conv_w, gdn_a_log, gdn_dt_bias, gdn_norm_g, w_mlp1, w_mlp2, final_norm_g):
    batch, seq, d_model = x.shape
    ctx_len = ctx.shape[1]
    depth = w_mod.shape[0]
    hy_c = hy_d.shape[-1]
    gla_kw = gla_w_a2.shape[-1]
    gla_dv = gla_norm_g.shape[-1]
    gla_vw = GLA_H * gla_dv
    gdn_d = gdn_norm_g.shape[-1]
    gdn_w = GDN_H * gdn_d
    gdn_wp = GDN_H * GDN_DP

    n2 = LANES
    n1 = 2 * seq // n2
    fft_tabs = _fft_tables(n1, n2)
    dft_tabs = _dft_tables(ctx_len)
    emb_lat = _hy_embedding(seq, HY_EMB_PAD)
    emb_ctx = _hy_embedding(ctx_len, HY_EMB_PAD)
    hy_deltas = jnp.abs(jnp.linspace(math.log(HY_DECAY_TARGET) / HY_SLOW_PCT,
                                     math.log(HY_DECAY_TARGET) / HY_FAST_PCT, hy_c, dtype=F32))[None, :]

    pad_rows = -(batch + 1) % 8
    cc = jnp.concatenate([c, c_ctx[None, :], jnp.zeros((pad_rows, d_model), F32)], axis=0)
    mod = _modulation(cc, w_mod, b_mod)

    def mod_vecs(l, i):
        v = mod[l, :, i * d_model:(i + 1) * d_model]
        lat = v[:batch, None, :]
        cx = jnp.broadcast_to(v[batch][None, None, :], (batch, 1, d_model))
        return lat, cx

    zeros_gla = jnp.zeros((batch, gla_vw, gla_kw), F32)
    zeros_gdn = jnp.zeros((batch, GDN_H, GDN_DP, GDN_DP), F32)
    gla_bd_gain = jnp.tile(gla_norm_g, (1, GLA_H))

    for l in range(depth):
        with_ctx_out = l < depth - 1
        (sh1, csh1), (sc1, csc1), (g1, cg1), (sh2, csh2), (sc2, csc2), (g2, cg2) = [mod_vecs(l, i) for i in range(N_MOD)]

        wl = w_in[l]
        o_gla = 3 * hy_c
        o_gdn = o_gla + 2 * gla_kw + 2 * gla_vw + 2 * GLA_RANK
        o_gdn_gate = o_gdn + 3 * gdn_w
        o_gdn_small = o_gdn_gate + gdn_w
        w_small = jnp.concatenate([wl[:, o_gdn - 2 * GLA_RANK:o_gdn], wl[:, o_gdn_small:]], axis=1)
        w_small = jnp.pad(w_small, ((0, 0), (0, LANES - w_small.shape[1])))
        in_plain = [
            wl[:, o_gla:o_gla + 2 * gla_kw],
            wl[:, o_gla + 2 * gla_kw:o_gla + 2 * gla_kw + gla_vw],
            wl[:, o_gla + 2 * gla_kw + gla_vw:o_gla + 2 * gla_kw + 2 * gla_vw],
            _pad_heads(wl[:, o_gdn_gate:o_gdn_small], GDN_H, 1),
            w_small,
        ]
        in_plain = [w.astype(BF16) for w in in_plain]
        in_hy = wl[:, :o_gla].astype(BF16)
        in_gdn = _pad_heads(wl[:, o_gdn:o_gdn_gate], 3 * GDN_H, 1).astype(BF16)
        gdn_cw = _pad_heads(gdn_conv_w[l], 3 * GDN_H, 1)
        gdn_gain = _pad_heads(gdn_norm_g[l][None, :].repeat(GDN_H, 0).reshape(1, gdn_w), GDN_H, 1)
        wo = w_out[l]
        w_hy = wo[:hy_c].astype(BF16)
        w_gla = wo[hy_c:hy_c + gla_vw].astype(BF16)
        w_gdn = _pad_heads(wo[hy_c + gla_vw:], GDN_H, 0).astype(BF16)
        w1 = w_mlp1[l].astype(BF16)
        w2 = w_mlp2[l].astype(BF16)
        filt_args = (hy_f_w1[l], hy_f_b1[l], hy_f_w2[l], hy_f_b2[l], hy_f_w3[l], hy_sin_freq[l], hy_deltas)

        def mixer_parts(tokens, shift, scale, seg_len, s0_gla, s0_gdn):
            length = tokens.shape[1]
            tl = min(SCAN_ROWS, length)
            z_qk, z_v, z_gate, z_gdn_gate, z_small, u, q, k, v = _inproj(
                tokens, shift, scale, norm1_g[l], in_plain, in_hy, in_gdn, hy_conv_w[l], hy_conv_b[l], gdn_cw,
                seg_len, gdn_d ** -0.5, min(PROJ_ROWS, length))
            b_f, b_b, gc = _gates(z_small, gla_w_a2[l], gla_b_a[l], gdn_a_log[l], gdn_dt_bias[l], tl)
            gla_f, gla_b, gla_sf, gla_sb = _gla_scan(z_qk, z_v, b_f, b_b, s0_gla[0], s0_gla[1], tl)
            gdn_f, gdn_b, gdn_sf, gdn_sb = _gdn_scan(q, k, v, gc, s0_gdn[0], s0_gdn[1], tl)
            return u, (gla_f, gla_b, z_gate), (gdn_f, gdn_b, z_gdn_gate), (gla_sf, gla_sb), (gdn_sf, gdn_sb)

        def finish(tokens, gate1, hy, gla, gdn):
            return _outproj(tokens, gate1, hy, *gla, *gdn, gla_bd_gain[l][None, :], gdn_gain, w_hy, w_gla, w_gdn,
                            gla_dv, gdn_d, min(PROJ_ROWS, tokens.shape[1]))

        def mlp(tokens, shift, scale, gate2, final_norm):
            return _mlp(tokens, shift, scale, gate2, norm2_g[l], w1, w2, final_norm_g, final_norm,
                        min(MLP_ROWS, tokens.shape[1]), MLP_HIDDEN)

        u_c, gla_c, gdn_c, gla_s, gdn_s = mixer_parts(ctx, csh1, csc1, ctx_len,
                                                      (zeros_gla, zeros_gla), (zeros_gdn, zeros_gdn))
        u_l, gla_l, gdn_l, _, _ = mixer_parts(x, sh1, sc1, GRID_W, gla_s, gdn_s)
        h_re, h_im = _filter_spectrum(*_hy_filters(emb_lat, *filt_args, min(PROJ_ROWS, seq)), fft_tabs, n1, n2)
        nb = hy_c // LANES
        y1 = _fftconv_gate(u_l, 0, u_l, nb, h_re, h_im, 0, hy_d[l, 0], fft_tabs, n1, n2)
        hy_l = _fftconv_gate(y1, 0, u_l, 2 * nb, h_re, h_im, nb, hy_d[l, 1], fft_tabs, n1, n2)
        x = finish(x, g1, hy_l, gla_l, gdn_l)
        x = mlp(x, sh2, sc2, g2, not with_ctx_out)

        if with_ctx_out:
            coef = _dense_filter_spectrum(*_hy_filters(emb_ctx, *filt_args, ctx_len), dft_tabs)
            y1c = _dftconv_gate(u_c, 0, u_c, 1, *coef, 0, hy_d[l, 0], dft_tabs)
            hy_c_out = _dftconv_gate(y1c, 0, u_c, 2, *coef, 1, hy_d[l, 1], dft_tabs)
            ctx = finish(ctx, cg1, hy_c_out, gla_c, gdn_c)
            ctx = mlp(ctx, csh2, csc2, cg2, False)
    return x
```

```python
import functools
import math

import numpy as np
import jax
import jax.numpy as jnp
from jax import lax
from jax.experimental import pallas as pl
from jax.experimental.pallas import tpu as pltpu

F32 = jnp.float32
BF16 = jnp.bfloat16
HIGHEST = lax.Precision.HIGHEST

NORM_EPS = 1e-6
N_MOD = 6
GRID_W = 64
CHUNK = 64
LANES = 128

HY_ORDER = 2
HY_SHORT = 3
HY_EMB = 33
HY_EMB_PAD = 40
HY_DECAY_TARGET = 1e-2
HY_FAST_PCT = 0.3
HY_SLOW_PCT = 1.5

GLA_H = 4
GLA_RANK = 16
GLA_TAU = 16.0
GDN_H = 4
GDN_DP = LANES
PITCH_PAD = 8
FFT_UNROLL = 8

VMEM_LIMIT = 56 * 1024 * 1024

SCAN_ROWS = 4 * CHUNK
PROJ_ROWS = 512
MLP_ROWS = 1024
MLP_HIDDEN = 1024


def _cparams(n_grid):
    return pltpu.CompilerParams(dimension_semantics=("arbitrary",) * n_grid, vmem_limit_bytes=VMEM_LIMIT)


def _bdot(a, b):
    return jnp.dot(a.astype(BF16), b.astype(BF16), preferred_element_type=F32)


def _bdot_nt(a, b):
    return lax.dot_general(a.astype(BF16), b.astype(BF16), (((1,), (1,)), ((), ())), preferred_element_type=F32)


def _bdot_tn(a, b):
    return lax.dot_general(a.astype(BF16), b.astype(BF16), (((0,), (0,)), ((), ())), preferred_element_type=F32)


def _const_spec(shape):
    return pl.BlockSpec(shape, lambda *_: (0,) * len(shape))


def _mod_kernel(c_ref, w_ref, b_ref, o_ref):
    c = c_ref[...]
    o_ref[0] = _bdot(c * jax.nn.sigmoid(c), w_ref[0]) + b_ref[0]


def _modulation(cc, w_mod, b_mod):
    depth, d, n = w_mod.shape
    tn = n // 4
    return pl.pallas_call(
        _mod_kernel,
        grid=(depth, n // tn),
        in_specs=[pl.BlockSpec(cc.shape, lambda l, j: (0, 0)),
                  pl.BlockSpec((1, d, tn), lambda l, j: (l, 0, j)),
                  pl.BlockSpec((1, 1, tn), lambda l, j: (l, 0, j))],
        out_specs=pl.BlockSpec((1, cc.shape[0], tn), lambda l, j: (l, 0, j)),
        out_shape=jax.ShapeDtypeStruct((depth, cc.shape[0], n), F32),
        compiler_params=_cparams(2),
        name="modulation",
    )(cc, w_mod.astype(BF16), b_mod.reshape(depth, 1, n))


def _rms_modulate(x, g, shift, scale):
    y = x * lax.rsqrt(jnp.mean(x * x, axis=-1, keepdims=True) + NORM_EPS) * g
    return y * (1.0 + scale) + shift


def _inproj_kernel(x_ref, sh_ref, sc_ref, g_ref, w_hy_ref, w_gdn_ref, hy_cw_ref, hy_cb_ref, gdn_cw_ref, *refs,
                   seg_len, q_scale):
    n_plain = (len(refs) - 4) // 2
    w_refs, plain_refs = refs[:n_plain], refs[n_plain:2 * n_plain]
    u_ref, q_ref, k_ref, v_ref = refs[2 * n_plain:]
    hb = _rms_modulate(x_ref[0], g_ref[...], sh_ref[0], sc_ref[0]).astype(BF16)
    for w_ref, o_ref in zip(w_refs, plain_refs):
        o_ref[0] = jnp.dot(hb, w_ref[...], preferred_element_type=F32)
    piece = 2 * LANES
    for c0 in range(0, u_ref.shape[-1], piece):
        cols = slice(c0, c0 + piece)
        z = jnp.dot(hb, w_hy_ref[:, cols], preferred_element_type=F32)
        u_ref[0, :, cols] = _seg_conv(z, hy_cw_ref[:, cols], seg_len) + hy_cb_ref[:, cols]
    width = q_ref.shape[-1]
    for part, (o_ref, norm_scale) in enumerate(((q_ref, q_scale), (k_ref, 1.0), (v_ref, None))):
        for c0 in range(0, width, piece):
            cols = slice(part * width + c0, part * width + c0 + piece)
            y = _seg_conv(jnp.dot(hb, w_gdn_ref[:, cols], preferred_element_type=F32), gdn_cw_ref[:, cols], seg_len)
            y = y * jax.nn.sigmoid(y)
            for h0 in range(0, piece, GDN_DP):
                t = y[:, h0:h0 + GDN_DP]
                if norm_scale is not None:
                    t = t * (lax.rsqrt(jnp.sum(t * t, axis=-1, keepdims=True) + NORM_EPS) * norm_scale)
                o_ref[0, :, c0 + h0:c0 + h0 + GDN_DP] = t


def _inproj(x, shift, scale, g, plain_weights, w_hy, w_gdn, hy_cw, hy_cb, gdn_cw, seg_len, q_scale, tm):
    b, l, d = x.shape
    hyw, gdw = w_hy.shape[1], w_gdn.shape[1] // 3
    vec = pl.BlockSpec((1, 1, d), lambda i, j: (i, 0, 0))
    widths = [w.shape[1] for w in plain_weights] + [hyw, gdw, gdw, gdw]
    return pl.pallas_call(
        functools.partial(_inproj_kernel, seg_len=seg_len, q_scale=q_scale),
        grid=(b, l // tm),
        in_specs=[pl.BlockSpec((1, tm, d), lambda i, j: (i, j, 0)), vec, vec, _const_spec((1, d)),
                  _const_spec(w_hy.shape), _const_spec(w_gdn.shape), _const_spec(hy_cw.shape),
                  _const_spec((1, hyw)), _const_spec(gdn_cw.shape)]
        + [_const_spec(w.shape) for w in plain_weights],
        out_specs=[pl.BlockSpec((1, tm, w), lambda i, j: (i, j, 0)) for w in widths],
        out_shape=[jax.ShapeDtypeStruct((b, l, w), F32) for w in widths],
        compiler_params=_cparams(2),
        name="inproj",
    )(x, shift, scale, g.reshape(1, d), w_hy, w_gdn, hy_cw, hy_cb.reshape(1, hyw), gdn_cw, *plain_weights)


def _seg_conv(z, w, seg_len):
    rows, width = z.shape
    taps = w.shape[0]
    half = taps // 2
    pos = lax.broadcasted_iota(jnp.int32, (rows, LANES), 0) % seg_len
    valid = {d: (pos >= -d) if d < 0 else (pos < seg_len - d) for d in range(-half, half + 1) if d != 0}
    tiles = []
    for c0 in range(0, width, LANES):
        cols = slice(c0, c0 + LANES)
        zc = z[:, cols]
        acc = zc * w[half:half + 1, cols]
        for k in range(taps):
            d = k - half
            if d != 0:
                shifted = pltpu.roll(zc, (-d) % rows, 0)
                acc = acc + jnp.where(valid[d], shifted, 0.0) * w[k:k + 1, cols]
        tiles.append(acc)
    return jnp.concatenate(tiles, axis=1)


def _hy_filter_kernel(z_ref, w1_ref, b1_ref, w2_ref, b2_ref, w3_ref, f_ref, dl_ref, hf_ref, hb_ref):
    hy_c = dl_ref.shape[1]
    z = z_ref[...]
    fdot = lambda a, b: jnp.dot(a, b, precision=HIGHEST, preferred_element_type=F32)
    h = jnp.sin(f_ref[0:1, :] * (fdot(z, w1_ref[...]) + b1_ref[...]))
    h = jnp.sin(f_ref[1:2, :] * (fdot(h, w2_ref[...]) + b2_ref[...]))
    h = fdot(h, w3_ref[...])
    window = jnp.exp(-z[:, 0:1] * dl_ref[...])
    row = lax.broadcasted_iota(jnp.int32, window.shape, 0) + pl.program_id(0) * z.shape[0]
    for o in range(HY_ORDER):
        base = 2 * o * hy_c
        hf_ref[:, o * hy_c:(o + 1) * hy_c] = h[:, base:base + hy_c] * window
        hb_ref[:, o * hy_c:(o + 1) * hy_c] = jnp.where(row > 0, h[:, base + hy_c:base + 2 * hy_c] * window, 0.0)


def _hy_filters(z, w1, b1, w2, b2, w3, sin_freq, deltas, tl):
    l, emb = z.shape
    fh = w2.shape[0]
    hy_c = deltas.shape[1]
    out = jax.ShapeDtypeStruct((l, HY_ORDER * hy_c), F32)
    rows = lambda w: pl.BlockSpec((tl, w), lambda i: (i, 0))
    return pl.pallas_call(
        _hy_filter_kernel,
        grid=(l // tl,),
        in_specs=[rows(emb), _const_spec((emb, fh)), _const_spec((1, fh)), _const_spec((fh, fh)), _const_spec((1, fh)),
                  _const_spec(w3.shape), _const_spec((2, fh)), _const_spec((1, hy_c))],
        out_specs=[rows(HY_ORDER * hy_c)] * 2,
        out_shape=[out, out],
        compiler_params=_cparams(1),
        name="hy_filters",
    )(z, jnp.pad(w1, ((0, emb - w1.shape[0]), (0, 0))), b1.reshape(1, fh), w2, b2.reshape(1, fh), w3, sin_freq, deltas)


def _hy_embedding(length, emb):
    t = jnp.linspace(0.0, 1.0, length, dtype=F32)[:, None]
    bands = (HY_EMB - 1) // 2
    f = jnp.linspace(1e-4, bands - 1, bands, dtype=F32)
    w = 2 * math.pi * jnp.arange(length, dtype=F32) / length
    ang = w[:, None] * f[None, :]
    z = jnp.concatenate([t, jnp.cos(ang), -jnp.sin(ang)], axis=-1)
    return jnp.pad(z, ((0, 0), (0, emb - z.shape[1])))


def _fft_tables(n1, n2):
    n = n1 * n2
    nk = n1 // 2 + 1
    kp = -(-nk // 8) * 8
    s1 = np.arange(n1 // 2)[None, :, None]
    s2 = np.arange(n2)[:, None, None]
    k1 = np.arange(kp)[None, None, :]
    theta = 2.0 * np.pi * ((k1 * (n2 * s1 + s2)) % n) / n
    tt_fwd = np.concatenate([np.cos(theta), -np.sin(theta)], axis=-1)
    weight = np.where(np.arange(kp) < nk, 2.0, 0.0)
    weight[0] = weight[nk - 1] = 1.0
    tt_inv = np.concatenate([np.cos(theta) * weight, -np.sin(theta) * weight], axis=-1)
    phi = 2.0 * np.pi * ((np.arange(n2)[:, None] * np.arange(n2)[None, :]) % n2) / n2
    c2, s2m = np.cos(phi), np.sin(phi)
    fwd = np.block([[c2, s2m], [-s2m, c2]])
    inv = np.block([[c2, -s2m], [s2m, c2]])
    return tuple(jnp.asarray(t, F32) for t in (tt_fwd, tt_inv, fwd, inv))


def _for_k1(body, nk):
    lax.fori_loop(0, nk - 1, lambda k1, c: body(k1) or c, 0, unroll=min(FFT_UNROLL, nk - 1))
    body(nk - 1)


def _time_pitch(n2):
    return n2 + PITCH_PAD


def _spec_pitch(n2):
    return 2 * n2 + PITCH_PAD


def _dft_load_time(load_block, time_ref, n1, n2):
    def copy(s1, carry):
        time_ref[pl.ds(pl.multiple_of(s1 * _time_pitch(n2), 8), n2), :] = load_block(
            pl.ds(pl.multiple_of(s1 * n2, n2), n2))
        return carry

    lax.fori_loop(0, n1 // 2, copy, 0, unroll=min(FFT_UNROLL, n1 // 2))


def _dft_stage_a(time_ref, tt_ref, spec_ref, n1, n2):
    kp = tt_ref.shape[-1] // 2

    def stage_a(s2, carry):
        slab = time_ref[pl.ds(s2, n1 // 2, stride=_time_pitch(n2)), :]
        res = _bdot_tn(tt_ref[s2], slab)
        spec_ref[pl.ds(s2, kp, stride=_spec_pitch(n2)), :] = res[:kp]
        spec_ref[pl.ds(n2 + s2, kp, stride=_spec_pitch(n2)), :] = res[kp:]
        return carry

    lax.fori_loop(0, n2, stage_a, 0, unroll=2 * FFT_UNROLL)


def _aligned_rows(start, size, align):
    return pl.ds(start if isinstance(start, int) else pl.multiple_of(start, align), size)


def _spec_rows(k1, n2):
    return _aligned_rows(k1 * _spec_pitch(n2), 2 * n2, 8)


def _spectrum_kernel(hf_ref, hb_ref, tt_ref, fwd_ref, hr_ref, hi_ref, time_ref, spec_ref, *, n1, n2):
    inv_n = 1.0 / (n1 * n2)
    fwd = fwd_ref[...].astype(BF16)
    for sign, src_ref in ((1.0, hf_ref), (-1.0, hb_ref)):
        _dft_load_time(lambda rows: src_ref[rows, :], time_ref, n1, n2)
        _dft_stage_a(time_ref, tt_ref, spec_ref, n1, n2)

        def stage_b(k1):
            rows = _spec_rows(k1, n2)
            spec = jnp.dot(fwd, spec_ref[rows, :].astype(BF16), preferred_element_type=F32) * inv_n
            hrows = _aligned_rows(k1 * n2, n2, n2)
            if sign > 0:
                hr_ref[hrows, :] = spec[:n2]
                hi_ref[hrows, :] = spec[n2:]
            else:
                hr_ref[hrows, :] += spec[:n2]
                hi_ref[hrows, :] -= spec[n2:]

        _for_k1(stage_b, n1 // 2 + 1)


def _filter_spectrum(hf, hb, tables, n1, n2):
    tt, _, fwd, _ = tables
    l, c = hf.shape
    n = (n1 // 2 + 1) * n2
    chan = lambda rows: pl.BlockSpec((rows, LANES), lambda j: (0, j))
    return pl.pallas_call(
        functools.partial(_spectrum_kernel, n1=n1, n2=n2),
        grid=(c // LANES,),
        in_specs=[chan(l), chan(l), _const_spec(tt.shape), _const_spec(fwd.shape)],
        out_specs=[chan(n), chan(n)],
        out_shape=[jax.ShapeDtypeStruct((n, c), F32)] * 2,
        scratch_shapes=_fft_scratch(tt, n1, n2),
        compiler_params=_cparams(1),
        name="hy_spectrum",
    )(hf, hb, tt, fwd)


def _fft_scratch(tt, n1, n2):
    kp = tt.shape[-1] // 2
    return [pltpu.VMEM((n1 // 2 * _time_pitch(n2), LANES), F32), pltpu.VMEM((kp * _spec_pitch(n2), LANES), F32)]


def _fftconv_kernel(u_ref, gate_ref, hr_ref, hi_ref, d_ref, ttf_ref, tti_ref, fwd_ref, inv_ref, o_ref,
                    time_ref, spec_ref, *, n1, n2):
    n1h = n1 // 2
    kp = ttf_ref.shape[-1] // 2
    fwd = fwd_ref[...].astype(BF16)
    inv = inv_ref[...].astype(BF16)
    _dft_load_time(lambda rows: u_ref[0, rows, :], time_ref, n1, n2)
    _dft_stage_a(time_ref, ttf_ref, spec_ref, n1, n2)

    def stage_b(k1):
        rows = _spec_rows(k1, n2)
        spec = jnp.dot(fwd, spec_ref[rows, :].astype(BF16), preferred_element_type=F32)
        br, bi = spec[:n2], spec[n2:]
        hrows = _aligned_rows(k1 * n2, n2, n2)
        hr, hi = hr_ref[hrows, :], hi_ref[hrows, :]
        y = jnp.concatenate([br * hr - bi * hi, br * hi + bi * hr], axis=0)
        spec_ref[rows, :] = jnp.dot(inv, y.astype(BF16), preferred_element_type=F32)

    _for_k1(stage_b, n1h + 1)

    def stage_a_inv(s2, carry):
        g = jnp.concatenate([spec_ref[pl.ds(s2, kp, stride=_spec_pitch(n2)), :],
                             spec_ref[pl.ds(n2 + s2, kp, stride=_spec_pitch(n2)), :]], axis=0)
        time_ref[pl.ds(s2, n1h, stride=_time_pitch(n2)), :] = _bdot(tti_ref[s2], g)
        return carry

    lax.fori_loop(0, n2, stage_a_inv, 0, unroll=2 * FFT_UNROLL)

    def gate_rows(s1, carry):
        rows = pl.ds(pl.multiple_of(s1 * n2, n2), n2)
        conv = time_ref[pl.ds(pl.multiple_of(s1 * _time_pitch(n2), 8), n2), :]
        o_ref[0, rows, :] = gate_ref[0, rows, :] * (conv + d_ref[...] * u_ref[0, rows, :])
        return carry

    lax.fori_loop(0, n1h, gate_rows, 0, unroll=min(FFT_UNROLL, n1h))


def _fftconv_gate(u_arr, u_blk, gate_arr, gate_blk, hr, hi, h_blk, d, tables, n1, n2):
    tt_fwd, tt_inv, fwd, inv = tables
    b, l, _ = u_arr.shape
    n = hr.shape[0]
    c = d.shape[0]
    ncb = c // LANES
    seq = lambda off: pl.BlockSpec((1, l, LANES), lambda j, i: (i, 0, off + j))
    chan = lambda rows, off=0: pl.BlockSpec((rows, LANES), lambda j, i: (0, off + j))
    return pl.pallas_call(
        functools.partial(_fftconv_kernel, n1=n1, n2=n2),
        grid=(ncb, b),
        in_specs=[seq(u_blk), seq(gate_blk), chan(n, h_blk), chan(n, h_blk), chan(1),
                  _const_spec(tt_fwd.shape), _const_spec(tt_inv.shape), _const_spec(fwd.shape),
                  _const_spec(inv.shape)],
        out_specs=pl.BlockSpec((1, l, LANES), lambda j, i: (i, 0, j)),
        out_shape=jax.ShapeDtypeStruct((b, l, c), F32),
        scratch_shapes=_fft_scratch(tt_fwd, n1, n2),
        compiler_params=_cparams(2),
        name="hy_fftconv",
    )(u_arr, gate_arr, hr, hi, d.reshape(1, c), tt_fwd, tt_inv, fwd, inv)


def _dft_tables(l):
    n = 2 * l
    ang = 2.0 * np.pi * ((np.arange(l)[:, None] * np.arange(l)[None, :]) % n) / n
    cf = np.cos(ang)
    sf = -np.sin(ang)
    sf[0, :] = (-1.0) ** np.arange(l)
    fwd = np.concatenate([cf, sf], axis=0)
    return jnp.asarray(fwd, F32), jnp.asarray(fwd.T, F32)


def _dftconv_kernel(u_ref, gate_ref, p_ref, q_ref, r_ref, d_ref, fwd_ref, inv_ref, o_ref):
    u = u_ref[0]
    l = u.shape[0]
    spec = _bdot(fwd_ref[...], u)
    xr, xi = spec[:l], spec[l:]
    q = q_ref[...]
    y = jnp.concatenate([xr * p_ref[...] - xi * q, xr * q + xi * r_ref[...]], axis=0)
    conv = _bdot(inv_ref[...], y)
    o_ref[0] = gate_ref[0] * (conv + d_ref[...] * u)


def _dense_spectrum_kernel(hf_ref, hb_ref, fwd_ref, p_ref, q_ref, r_ref):
    l = hf_ref.shape[0]
    xf = _bdot(fwd_ref[...], hf_ref[...])
    xb = _bdot(fwd_ref[...], hb_ref[...])
    first = lax.broadcasted_iota(jnp.int32, (l, hf_ref.shape[1]), 0) == 0
    scale = jnp.where(first, 0.5 / l, 1.0 / l)
    re = (xf[:l] + xb[:l]) * scale
    p_ref[...] = re
    q_ref[...] = jnp.where(first, 0.0, (xf[l:] - xb[l:]) * scale)
    r_ref[...] = jnp.where(first, (xf[l:] + xb[l:]) * scale, re)


def _dense_filter_spectrum(hf, hb, tables):
    fwd, _ = tables
    out = jax.ShapeDtypeStruct(hf.shape, F32)
    return pl.pallas_call(
        _dense_spectrum_kernel,
        grid=(1,),
        in_specs=[_const_spec(hf.shape), _const_spec(hb.shape), _const_spec(fwd.shape)],
        out_specs=[_const_spec(hf.shape)] * 3,
        out_shape=[out] * 3,
        compiler_params=_cparams(1),
        name="hy_dense_spectrum",
    )(hf, hb, fwd)


def _dftconv_gate(u_arr, u_blk, gate_arr, gate_blk, p, q, r, h_blk, d, tables):
    fwd, inv = tables
    b, l, _ = u_arr.shape
    c = d.shape[0]
    seq = lambda off: pl.BlockSpec((1, l, c), lambda i: (i, 0, off))
    coef = pl.BlockSpec((l, c), lambda i: (0, h_blk))
    return pl.pallas_call(
        _dftconv_kernel,
        grid=(b,),
        in_specs=[seq(u_blk), seq(gate_blk), coef, coef, coef,
                  _const_spec((1, c)), _const_spec(fwd.shape), _const_spec(inv.shape)],
        out_specs=pl.BlockSpec((1, l, c), lambda i: (i, 0, 0)),
        out_shape=jax.ShapeDtypeStruct((b, l, c), F32),
        compiler_params=_cparams(1),
        name="hy_dftconv",
    )(u_arr, gate_arr, p, q, r, d.reshape(1, c), fwd, inv)


GDN_GF, GDN_GB, GDN_BF, GDN_BB = 32, 36, 40, 44

def _chunk_cumsum_mats(tl):
    r = np.arange(tl)
    same = (r[:, None] // CHUNK) == (r[None, :] // CHUNK)
    lower = same & (r[None, :] <= r[:, None])
    upper = same & (r[None, :] >= r[:, None])
    return jnp.asarray(lower, F32), jnp.asarray(upper, F32)


def _tri_cumsum(tri, x):
    hi = x.astype(BF16)
    rest = x - hi.astype(F32)
    mid = rest.astype(BF16)
    lo = (rest - mid.astype(F32)).astype(BF16)
    dot = lambda part: jnp.dot(tri, part, preferred_element_type=F32)
    return dot(hi) + dot(mid) + dot(lo)


def _gates_kernel(small_ref, w_ref, b_ref, nea_ref, dtb_ref, lo_ref, up_ref, bf_ref, bb_ref, gc_ref):
    small = small_ref[0]
    kw = bf_ref.shape[-1]
    log_decay = nea_ref[...] * jax.nn.softplus(small + dtb_ref[...])
    cums = []
    for d, tri_ref in enumerate((lo_ref, up_ref)):
        pre = jnp.dot(small[:, d * GLA_RANK:(d + 1) * GLA_RANK], w_ref[d], precision=HIGHEST,
                      preferred_element_type=F32) + b_ref[d]
        vals = jnp.concatenate([jax.nn.log_sigmoid(pre) * (1.0 / GLA_TAU), log_decay], axis=1)
        cums.append(_tri_cumsum(tri_ref[...].astype(BF16), vals))
    bf_ref[0] = cums[0][:, :kw]
    bb_ref[0] = cums[1][:, :kw]
    lane = lax.broadcasted_iota(jnp.int32, small.shape, 1)
    gc_ref[0] = jnp.where(lane < GDN_GB, cums[0][:, kw:], jnp.where(lane < GDN_BF, cums[1][:, kw:],
                                                                  jax.nn.sigmoid(small)))


def _gates(small, w_a2, b_a, a_log, dt_bias, tl):
    b, l, sw = small.shape
    kw = w_a2.shape[-1]
    nea = jnp.zeros((1, LANES), F32).at[0, GDN_GF:GDN_BF].set(-jnp.exp(a_log.reshape(-1)))
    dtb = jnp.zeros((1, LANES), F32).at[0, GDN_GF:GDN_BF].set(dt_bias.reshape(-1))
    lower, upper = _chunk_cumsum_mats(tl)
    tok = lambda w: pl.BlockSpec((1, tl, w), lambda i, j: (i, j, 0))
    return pl.pallas_call(
        _gates_kernel,
        grid=(b, l // tl),
        in_specs=[tok(sw), _const_spec(w_a2.shape), _const_spec((2, 1, kw)), _const_spec((1, LANES)),
                  _const_spec((1, LANES)), _const_spec((tl, tl)), _const_spec((tl, tl))],
        out_specs=[tok(kw), tok(kw), tok(LANES)],
        out_shape=[jax.ShapeDtypeStruct((b, l, kw), F32)] * 2 + [jax.ShapeDtypeStruct((b, l, LANES), F32)],
        compiler_params=_cparams(2),
        name="scan_gates",
    )(small, w_a2, b_a.reshape(2, 1, kw), nea, dtb, lower, upper)


def _gla_scan_kernel(qkf_ref, vf_ref, bf_ref, qkb_ref, vb_ref, bb_ref, s0f_ref, s0b_ref, hm_ref, vm_ref, bd_ref,
                     of_ref, ob_ref, sff_ref, sfb_ref, stf_ref, stb_ref, *, n_chunks, kw, q_scale):
    j = pl.program_id(1)

    @pl.when(j == 0)
    def _():
        stf_ref[...] = s0f_ref[0]
        stb_ref[...] = s0b_ref[0]

    ri = lax.broadcasted_iota(jnp.int32, (CHUNK, GLA_H * CHUNK), 0)
    ci = lax.broadcasted_iota(jnp.int32, (CHUNK, GLA_H * CHUNK), 1) % CHUNK
    hm = hm_ref[...]
    vm = vm_ref[...]
    bd = bd_ref[...]
    dirs = ((qkf_ref, vf_ref, bf_ref, of_ref, stf_ref, ri >= ci, CHUNK - 1, CHUNK // 2 - 1, False),
            (qkb_ref, vb_ref, bb_ref, ob_ref, stb_ref, ci >= ri, 0, CHUNK // 2, True))

    qts, kstacks, vstacks, keeps, qes, vs, kds, decs, sinks = [], [], [], [], [], [], [], [], []
    for qk_ref, v_ref, b_ref, o_ref, st_ref, keep, last, mid, rev in dirs:
        for c in range(n_chunks):
            rows = slice((n_chunks - 1 - c if rev else c) * CHUNK, (n_chunks - c if rev else c + 1) * CHUNK)
            qk = qk_ref[0, rows, :]
            q, k = qk[:, :kw] * q_scale, qk[:, kw:]
            v = v_ref[0, rows, :]
            bc = b_ref[0, rows, :]
            b_mid = bc[mid:mid + 1, :]
            b_last = bc[last:last + 1, :]
            kt = k * jnp.exp(b_mid - bc)
            qts.append(q * jnp.exp(bc - b_mid))
            kstacks.append(jnp.concatenate([kt * hm[h:h + 1, :] for h in range(GLA_H)], axis=0).astype(BF16))
            vstacks.append(jnp.concatenate([v * vm[h:h + 1, :] for h in range(GLA_H)], axis=0).astype(BF16))
            keeps.append(keep)
            qes.append((q * jnp.exp(bc)).astype(BF16))
            vs.append(v)
            kds.append(k * jnp.exp(b_last - bc))
            decs.append(jnp.exp(b_last))
            sinks.append((o_ref, rows))
    attns = [jnp.where(keep, _bdot_nt(qt, ks), 0.0) for keep, qt, ks in zip(keeps, qts, kstacks)]
    o_intra = [_bdot(a, vst) for a, vst in zip(attns, vstacks)]
    upds = [_bdot_tn(v, kd) * bd for v, kd in zip(vs, kds)]

    for d, (_, _, _, _, st_ref, _, _, _, _) in enumerate(dirs):
        st = st_ref[...]
        for c in range(n_chunks):
            i = d * n_chunks + c
            o_ref, rows = sinks[i]
            o_ref[0, rows, :] = o_intra[i] + _bdot_nt(qes[i], st)
            st = st * decs[i] + upds[i]
        st_ref[...] = st

    @pl.when(j == pl.num_programs(1) - 1)
    def _():
        sff_ref[0] = stf_ref[...]
        sfb_ref[0] = stb_ref[...]


def _gla_scan(qk, v, b_f, b_b, s0_f, s0_b, tl):
    b, l, kw2 = qk.shape
    kw, vw = kw2 // 2, v.shape[-1]
    dk, dv = kw // GLA_H, vw // GLA_H
    nblk = l // tl
    heads_k = np.arange(kw) // dk
    heads_v = np.arange(vw) // dv
    hm = jnp.asarray(np.arange(8)[:, None] == heads_k[None, :], F32)
    vm = jnp.asarray(np.arange(8)[:, None] == heads_v[None, :], F32)
    bd = jnp.asarray(heads_v[:, None] == heads_k[None, :], F32)
    fwd = lambda w: pl.BlockSpec((1, tl, w), lambda i, j: (i, j, 0))
    bwd = lambda w: pl.BlockSpec((1, tl, w), lambda i, j: (i, nblk - 1 - j, 0))
    state = pl.BlockSpec((1, vw, kw), lambda i, j: (i, 0, 0))
    return pl.pallas_call(
        functools.partial(_gla_scan_kernel, n_chunks=tl // CHUNK, kw=kw, q_scale=dk ** -0.5),
        grid=(b, nblk),
        in_specs=[fwd(kw2), fwd(vw), fwd(kw), bwd(kw2), bwd(vw), bwd(kw), state, state,
                  _const_spec(hm.shape), _const_spec(vm.shape), _const_spec(bd.shape)],
        out_specs=[fwd(vw), bwd(vw), state, state],
        out_shape=[jax.ShapeDtypeStruct((b, l, vw), F32)] * 2 + [jax.ShapeDtypeStruct((b, vw, kw), F32)] * 2,
        scratch_shapes=[pltpu.VMEM((vw, kw), F32)] * 2,
        compiler_params=_cparams(2),
        name="gla_scan",
    )(qk, v, b_f, qk, v, b_b, s0_f, s0_b, hm, vm, bd)


def _unit_tri_inverses(mats, eye, m16, m32, m64):
    diag = [a * m16 for a in mats]
    inv = [eye + a for a in diag]
    pw = [a.astype(BF16) for a in diag]
    for _ in range(3):
        pw = [_bdot(p, p).astype(BF16) for p in pw]
        inv = [t + _bdot(t, p) for t, p in zip(inv, pw)]
    for mask in (m32, m64):
        inv_b = [t.astype(BF16) for t in inv]
        mid = [_bdot(t, a * mask) for t, a in zip(inv_b, mats)]
        inv = [t + _bdot(m, tb) for t, m, tb in zip(inv, mid, inv_b)]
    return inv


def _gdn_scan_kernel(qf_ref, kf_ref, vf_ref, gcf_ref, qb_ref, kb_ref, vb_ref, gcb_ref,
                     s0f_ref, s0b_ref, of_ref, ob_ref, sff_ref, sfb_ref, sf_ref, sb_ref, *, tl):
    j = pl.program_id(1)

    @pl.when(j == 0)
    def _():
        sf_ref[...] = s0f_ref[0]
        sb_ref[...] = s0b_ref[0]

    nck = tl // CHUNK
    ri = lax.broadcasted_iota(jnp.int32, (tl, tl), 0)
    ci = lax.broadcasted_iota(jnp.int32, (tl, tl), 1)
    same = lambda n: (ri // n) == (ci // n)
    chunk = same(CHUNK)
    eye = (ri == ci).astype(F32)
    m16 = same(16).astype(F32)
    m32 = (same(32) & ~same(16)).astype(F32)
    m64 = (chunk & ~same(32)).astype(F32)
    dirs = ((qf_ref, kf_ref, vf_ref, gcf_ref, of_ref, sf_ref, ri > ci, ri >= ci, CHUNK - 1, GDN_GF, GDN_BF, False),
            (qb_ref, kb_ref, vb_ref, gcb_ref, ob_ref, sb_ref, ci > ri, ci >= ri, 0, GDN_GB, GDN_BB, True))

    qs, ks, vs, gcums, betas, stricts, incls, glasts, outs = [], [], [], [], [], [], [], [], []
    for q_ref, k_ref, v_ref, gc_ref, o_ref, s_ref, strict, incl, last, g_col, b_col, rev in dirs:
        gc = gc_ref[0]
        gct = gc.T
        for h in range(GDN_H):
            cols = slice(h * GDN_DP, (h + 1) * GDN_DP)
            gcum = gc[:, g_col + h:g_col + h + 1]
            decay = jnp.exp(jnp.minimum(gcum - gct[g_col + h:g_col + h + 1, :], 0.0))
            qs.append(q_ref[0, :, cols])
            ks.append(k_ref[0, :, cols])
            vs.append(v_ref[0, :, cols])
            gcums.append(gcum)
            betas.append(gc[:, b_col + h:b_col + h + 1])
            stricts.append(jnp.where(chunk & strict, decay, 0.0))
            incls.append(jnp.where(chunk & incl, decay, 0.0))
            glasts.append(jnp.concatenate(
                [jnp.broadcast_to(gcum[c * CHUNK + last:c * CHUNK + last + 1, :], (CHUNK, 1)) for c in range(nck)],
                axis=0))
            outs.append((o_ref, s_ref, h, cols, rev))
    n_pairs = len(qs)
    e_cols = [jnp.exp(g) for g in gcums]
    kbs = [k * b for k, b in zip(ks, betas)]
    mats = [-_bdot_nt(kb, k) * d for kb, k, d in zip(kbs, ks, stricts)]
    invs = _unit_tri_inverses(mats, eye, m16, m32, m64)
    wus = [_bdot(t, jnp.concatenate([kb * e, v * b], axis=1))
           for t, kb, e, v, b in zip(invs, kbs, e_cols, vs, betas)]
    attns = [_bdot_nt(q, k) * d for q, k, d in zip(qs, ks, incls)]
    awus = [_bdot(a, wu) for a, wu in zip(attns, wus)]
    q_effs = [(q * e - awu[:, :GDN_DP]).astype(BF16) for q, e, awu in zip(qs, e_cols, awus)]
    ws = [wu[:, :GDN_DP].astype(BF16) for wu in wus]
    us = [wu[:, GDN_DP:] for wu in wus]
    o_intra = [awu[:, GDN_DP:] for awu in awus]
    kds = [(k * jnp.exp(gl - g)).astype(BF16) for k, gl, g in zip(ks, glasts, gcums)]
    decs = [jnp.exp(gl) for gl in glasts]

    states = [s_ref[h] for (_, s_ref, h, _, _) in outs]
    for c in range(nck):
        rows = [slice((nck - 1 - c if rev else c) * CHUNK, (nck - c if rev else c + 1) * CHUNK)
                for (_, _, _, _, rev) in outs]
        prods = [jnp.dot(jnp.concatenate([q_effs[i][rows[i]], ws[i][rows[i]]], axis=0), states[i].astype(BF16),
                         preferred_element_type=F32) for i in range(n_pairs)]
        for i, (o_ref, _, _, cols, _) in enumerate(outs):
            o_ref[0, rows[i], cols] = o_intra[i][rows[i]] + prods[i][:CHUNK]
        v_news = [us[i][rows[i]] - prods[i][CHUNK:] for i in range(n_pairs)]
        states = [decs[i][rows[i]][0:1] * states[i] + _bdot_tn(kds[i][rows[i]], v_news[i]) for i in range(n_pairs)]
    for i, (_, s_ref, h, _, _) in enumerate(outs):
        s_ref[h] = states[i]

    @pl.when(j == pl.num_programs(1) - 1)
    def _():
        sff_ref[0] = sf_ref[...]
        sfb_ref[0] = sb_ref[...]


def _gdn_scan(q, k, v, gc, s0_f, s0_b, tl):
    b, l, width = q.shape
    nblk = l // tl
    fwd = lambda w: pl.BlockSpec((1, tl, w), lambda i, j: (i, j, 0))
    bwd = lambda w: pl.BlockSpec((1, tl, w), lambda i, j: (i, nblk - 1 - j, 0))
    state = pl.BlockSpec((1, GDN_H, GDN_DP, GDN_DP), lambda i, j: (i, 0, 0, 0))
    st_shape = jax.ShapeDtypeStruct((b, GDN_H, GDN_DP, GDN_DP), F32)
    return pl.pallas_call(
        functools.partial(_gdn_scan_kernel, tl=tl),
        grid=(b, nblk),
        in_specs=[fwd(width), fwd(width), fwd(width), fwd(LANES),
                  bwd(width), bwd(width), bwd(width), bwd(LANES), state, state],
        out_specs=[fwd(width), bwd(width), state, state],
        out_shape=[jax.ShapeDtypeStruct((b, l, width), F32)] * 2 + [st_shape] * 2,
        scratch_shapes=[pltpu.VMEM((GDN_H, GDN_DP, GDN_DP), F32)] * 2,
        compiler_params=_cparams(2),
        name="gdn_scan",
    )(q, k, v, gc, q, k, v, gc, s0_f, s0_b)


def _gated_head_norm(o, gate, g_norm, ones_bd, inv_d):
    sq = o * o
    hi = sq.astype(BF16)
    lo = (sq - hi.astype(F32)).astype(BF16)
    ms = (jnp.dot(hi, ones_bd, preferred_element_type=F32) + jnp.dot(lo, ones_bd, preferred_element_type=F32)) * inv_d
    return o * lax.rsqrt(ms + NORM_EPS) * g_norm * (gate * jax.nn.sigmoid(gate))


def _outproj_kernel(x_ref, g1_ref, hy_ref, glf_ref, glb_ref, glg_ref, gdf_ref, gdb_ref, gdg_ref,
                    gln_ref, gdn_ref, glm_ref, gdm_ref, why_ref, wgl_ref, wgd_ref, o_ref, *, gla_dv, gdn_d):
    y_gla = _gated_head_norm(glf_ref[0] + glb_ref[0], glg_ref[0], gln_ref[...], glm_ref[...], 1.0 / gla_dv)
    y_gdn = _gated_head_norm(gdf_ref[0] + gdb_ref[0], gdg_ref[0], gdn_ref[...], gdm_ref[...], 1.0 / gdn_d)
    acc = _bdot(hy_ref[0], why_ref[...]) + _bdot(y_gla, wgl_ref[...]) + _bdot(y_gdn, wgd_ref[...])
    o_ref[0] = x_ref[0] + g1_ref[0] * acc


def _outproj(x, g1, hy, gla_f, gla_b, gla_gate, gdn_f, gdn_b, gdn_gate, gla_g, gdn_g, w_hy, w_gla, w_gdn,
             gla_dv, gdn_d, tm):
    b, l, d = x.shape
    hyw, glw, gdw = hy.shape[-1], gla_f.shape[-1], gdn_f.shape[-1]
    gl_heads = np.arange(glw) // gla_dv
    gd_heads = np.arange(gdw) // GDN_DP
    gl_m = jnp.asarray(gl_heads[:, None] == gl_heads[None, :], BF16)
    gd_m = jnp.asarray(gd_heads[:, None] == gd_heads[None, :], BF16)
    tok = lambda w: pl.BlockSpec((1, tm, w), lambda i, j: (i, j, 0))
    return pl.pallas_call(
        functools.partial(_outproj_kernel, gla_dv=gla_dv, gdn_d=gdn_d),
        grid=(b, l // tm),
        in_specs=[tok(d), pl.BlockSpec((1, 1, d), lambda i, j: (i, 0, 0)), tok(hyw), tok(glw), tok(glw), tok(glw),
                  tok(gdw), tok(gdw), tok(gdw), _const_spec((1, glw)), _const_spec((1, gdw)),
                  _const_spec(gl_m.shape), _const_spec(gd_m.shape),
                  _const_spec(w_hy.shape), _const_spec(w_gla.shape), _const_spec(w_gdn.shape)],
        out_specs=tok(d),
        out_shape=jax.ShapeDtypeStruct(x.shape, F32),
        compiler_params=_cparams(2),
        name="outproj",
    )(x, g1, hy, gla_f, gla_b, gla_gate, gdn_f, gdn_b, gdn_gate, gla_g, gdn_g, gl_m, gd_m, w_hy, w_gla, w_gdn)


def _mlp_kernel(x_ref, sh_ref, sc_ref, g2_ref, ng_ref, w1_ref, w2_ref, fg_ref, o_ref, hn_ref, acc_ref, *, final_norm):
    kk = pl.program_id(2)

    @pl.when(kk == 0)
    def _():
        hn_ref[...] = _rms_modulate(x_ref[0], ng_ref[...], sh_ref[0], sc_ref[0]).astype(BF16)
        acc_ref[...] = jnp.zeros_like(acc_ref)

    hid = jnp.maximum(jnp.dot(hn_ref[...], w1_ref[...], preferred_element_type=F32), 0.0)
    acc_ref[...] += _bdot(hid * hid, w2_ref[...])

    @pl.when(kk == pl.num_programs(2) - 1)
    def _():
        y = x_ref[0] + g2_ref[0] * acc_ref[...]
        if final_norm:
            y = y * lax.rsqrt(jnp.mean(y * y, axis=-1, keepdims=True) + NORM_EPS) * fg_ref[...]
        o_ref[0] = y


def _mlp(x, shift, scale, gate, norm_g, w1, w2, final_g, final_norm, tm, th):
    b, l, d = x.shape
    dff = w1.shape[1]
    vec = pl.BlockSpec((1, 1, d), lambda i, j, k: (i, 0, 0))
    row = pl.BlockSpec((1, d), lambda i, j, k: (0, 0))
    return pl.pallas_call(
        functools.partial(_mlp_kernel, final_norm=final_norm),
        grid=(b, l // tm, dff // th),
        in_specs=[pl.BlockSpec((1, tm, d), lambda i, j, k: (i, j, 0)), vec, vec, vec, row,
                  pl.BlockSpec((d, th), lambda i, j, k: (0, k)), pl.BlockSpec((th, d), lambda i, j, k: (k, 0)), row],
        out_specs=pl.BlockSpec((1, tm, d), lambda i, j, k: (i, j, 0)),
        out_shape=jax.ShapeDtypeStruct(x.shape, F32),
        scratch_shapes=[pltpu.VMEM((tm, d), BF16), pltpu.VMEM((tm, d), F32)],
        compiler_params=_cparams(3),
        name="mlp",
    )(x, shift, scale, gate, norm_g.reshape(1, d), w1, w2, final_g.reshape(1, d))


def _pad_heads(w, n_heads, axis):
    shape = w.shape
    d = shape[axis] // n_heads
    w = w.reshape(shape[:axis] + (n_heads, d) + shape[axis + 1:])
    pad = [(0, 0)] * w.ndim
    pad[axis + 1] = (0, GDN_DP - d)
    w = jnp.pad(w, pad)
    return w.reshape(shape[:axis] + (n_heads * GDN_DP,) + shape[axis + 1:])


def kernel(x, c, ctx, c_ctx, norm1_g, norm2_g, w_mod, b_mod, w_in, w_out, hy_conv_w, hy_conv_b, hy_f_w1, hy_f_b1, hy_f_w2, hy_f_b2, hy_f_w3, hy_sin_freq, hy_d, gla_w_a2, gla_b_a, gla_norm_g, gdn_conv_w, gdn_a_log, gdn_dt_bias, gdn_norm_g, w_mlp1, w_mlp2, final_norm_g):
    batch, seq, d_model = x.shape
    ctx_len = ctx.shape[1]
    depth = w_mod.shape[0]
    hy_c = hy_d.shape[-1]
    gla_kw = gla_w_a2.shape[-1]
    gla_dv = gla_norm_g.shape[-1]
    gla_vw = GLA_H * gla_dv
    gdn_d = gdn_norm_g.shape[-1]
    gdn_w = GDN_H * gdn_d
    gdn_wp = GDN_H * GDN_DP

    n2 = LANES
    n1 = 2 * seq // n2
    fft_tabs = _fft_tables(n1, n2)
    dft_tabs = _dft_tables(ctx_len)
    emb_lat = _hy_embedding(seq, HY_EMB_PAD)
    emb_ctx = _hy_embedding(ctx_len, HY_EMB_PAD)
    hy_deltas = jnp.abs(jnp.linspace(math.log(HY_DECAY_TARGET) / HY_SLOW_PCT,
                                     math.log(HY_DECAY_TARGET) / HY_FAST_PCT, hy_c, dtype=F32))[None, :]

    pad_rows = -(batch + 1) % 8
    cc = jnp.concatenate([c, c_ctx[None, :], jnp.zeros((pad_rows, d_model), F32)], axis=0)
    mod = _modulation(cc, w_mod, b_mod)

    def mod_vecs(l, i):
        v = mod[l, :, i * d_model:(i + 1) * d_model]
        lat = v[:batch, None, :]
        cx = jnp.broadcast_to(v[batch][None, None, :], (batch, 1, d_model))
        return lat, cx

    zeros_gla = jnp.zeros((batch, gla_vw, gla_kw), F32)
    zeros_gdn = jnp.zeros((batch, GDN_H, GDN_DP, GDN_DP), F32)
    gla_bd_gain = jnp.tile(gla_norm_g, (1, GLA_H))

    for l in range(depth):
        with_ctx_out = l < depth - 1
        (sh1, csh1), (sc1, csc1), (g1, cg1), (sh2, csh2), (sc2, csc2), (g2, cg2) = [mod_vecs(l, i) for i in range(N_MOD)]

        wl = w_in[l]
        o_gla = 3 * hy_c
        o_gdn = o_gla + 2 * gla_kw + 2 * gla_vw + 2 * GLA_RANK
        o_gdn_gate = o_gdn + 3 * gdn_w
        o_gdn_small = o_gdn_gate + gdn_w
        w_small = jnp.concatenate([wl[:, o_gdn - 2 * GLA_RANK:o_gdn], wl[:, o_gdn_small:]], axis=1)
        w_small = jnp.pad(w_small, ((0, 0), (0, LANES - w_small.shape[1])))
        in_plain = [
            wl[:, o_gla:o_gla + 2 * gla_kw],
            wl[:, o_gla + 2 * gla_kw:o_gla + 2 * gla_kw + gla_vw],
            wl[:, o_gla + 2 * gla_kw + gla_vw:o_gla + 2 * gla_kw + 2 * gla_vw],
            _pad_heads(wl[:, o_gdn_gate:o_gdn_small], GDN_H, 1),
            w_small,
        ]
        in_plain = [w.astype(BF16) for w in in_plain]
        in_hy = wl[:, :o_gla].astype(BF16)
        in_gdn = _pad_heads(wl[:, o_gdn:o_gdn_gate], 3 * GDN_H, 1).astype(BF16)
        gdn_cw = _pad_heads(gdn_conv_w[l], 3 * GDN_H, 1)
        gdn_gain = _pad_heads(gdn_norm_g[l][None, :].repeat(GDN_H, 0).reshape(1, gdn_w), GDN_H, 1)
        wo = w_out[l]
        w_hy = wo[:hy_c].astype(BF16)
        w_gla = wo[hy_c:hy_c + gla_vw].astype(BF16)
        w_gdn = _pad_heads(wo[hy_c + gla_vw:], GDN_H, 0).astype(BF16)
        w1 = w_mlp1[l].astype(BF16)
        w2 = w_mlp2[l].astype(BF16)
        filt_args = (hy_f_w1[l], hy_f_b1[l], hy_f_w2[l], hy_f_b2[l], hy_f_w3[l], hy_sin_freq[l], hy_deltas)

        def mixer_parts(tokens, shift, scale, seg_len, s0_gla, s0_gdn):
            length = tokens.shape[1]
            tl = min(SCAN_ROWS, length)
            z_qk, z_v, z_gate, z_gdn_gate, z_small, u, q, k, v = _inproj(
                tokens, shift, scale, norm1_g[l], in_plain, in_hy, in_gdn, hy_conv_w[l], hy_conv_b[l], gdn_cw,
                seg_len, gdn_d ** -0.5, min(PROJ_ROWS, length))
            b_f, b_b, gc = _gates(z_small, gla_w_a2[l], gla_b_a[l], gdn_a_log[l], gdn_dt_bias[l], tl)
            gla_f, gla_b, gla_sf, gla_sb = _gla_scan(z_qk, z_v, b_f, b_b, s0_gla[0], s0_gla[1], tl)
            gdn_f, gdn_b, gdn_sf, gdn_sb = _gdn_scan(q, k, v, gc, s0_gdn[0], s0_gdn[1], tl)
            return u, (gla_f, gla_b, z_gate), (gdn_f, gdn_b, z_gdn_gate), (gla_sf, gla_sb), (gdn_sf, gdn_sb)

        def finish(tokens, gate1, hy, gla, gdn):
            return _outproj(tokens, gate1, hy, *gla, *gdn, gla_bd_gain[l][None, :], gdn_gain, w_hy, w_gla, w_gdn,
                            gla_dv, gdn_d, min(PROJ_ROWS, tokens.shape[1]))

        def mlp(tokens, shift, scale, gate2, final_norm):
            return _mlp(tokens, shift, scale, gate2, norm2_g[l], w1, w2, final_norm_g, final_norm,
                        min(MLP_ROWS, tokens.shape[1]), MLP_HIDDEN)

        u_c, gla_c, gdn_c, gla_s, gdn_s = mixer_parts(ctx, csh1, csc1, ctx_len,
                                                      (zeros_gla, zeros_gla), (zeros_gdn, zeros_gdn))
        u_l, gla_l, gdn_l, _, _ = mixer_parts(x, sh1, sc1, GRID_W, gla_s, gdn_s)
        h_re, h_im = _filter_spectrum(*_hy_filters(emb_lat, *filt_args, min(PROJ_ROWS, seq)), fft_tabs, n1, n2)
        nb = hy_c // LANES
        y1 = _fftconv_gate(u_l, 0, u_l, nb, h_re, h_im, 0, hy_d[l, 0], fft_tabs, n1, n2)
        hy_l = _fftconv_gate(y1, 0, u_l, 2 * nb, h_re, h_im, nb, hy_d[l, 1], fft_tabs, n1, n2)
        x = finish(x, g1, hy_l, gla_l, gdn_l)
        x = mlp(x, sh2, sc2, g2, not with_ctx_out)

        if with_ctx_out:
            coef = _dense_filter_spectrum(*_hy_filters(emb_ctx, *filt_args, ctx_len), dft_tabs)
            y1c = _dftconv_gate(u_c, 0, u_c, 1, *coef, 0, hy_d[l, 0], dft_tabs)
            hy_c_out = _dftconv_gate(y1c, 0, u_c, 2, *coef, 1, hy_d[l, 1], dft_tabs)
            ctx = finish(ctx, cg1, hy_c_out, gla_c, gdn_c)
            ctx = mlp(ctx, csh2, csc2, cg2, False)
    return x
```

```python
import functools
import math

import numpy as np
import jax
import jax.numpy as jnp
from jax import lax
from jax.experimental import pallas as pl
from jax.experimental.pallas import tpu as pltpu

F32 = jnp.float32
BF16 = jnp.bfloat16
HIGHEST = lax.Precision.HIGHEST

NORM_EPS = 1e-6
N_MOD = 6
GRID_W = 64
CHUNK = 64
LANES = 128

HY_ORDER = 2
HY_SHORT = 3
HY_EMB = 33
HY_EMB_PAD = 40
HY_DECAY_TARGET = 1e-2
HY_FAST_PCT = 0.3
HY_SLOW_PCT = 1.5

GLA_H = 4
GLA_RANK = 16
GLA_TAU = 16.0
GDN_H = 4
GDN_DP = LANES
PITCH_PAD = 8
FFT_UNROLL = 8

VMEM_LIMIT = 56 * 1024 * 1024

SCAN_ROWS = 4 * CHUNK
PROJ_ROWS = 512
MLP_ROWS = 1024
MLP_HIDDEN = 1024


def _cparams(n_grid):
    return pltpu.CompilerParams(dimension_semantics=("arbitrary",) * n_grid, vmem_limit_bytes=VMEM_LIMIT)


def _bdot(a, b):
    return jnp.dot(a.astype(BF16), b.astype(BF16), preferred_element_type=F32)


def _bdot_nt(a, b):
    return lax.dot_general(a.astype(BF16), b.astype(BF16), (((1,), (1,)), ((), ())), preferred_element_type=F32)


def _bdot_tn(a, b):
    return lax.dot_general(a.astype(BF16), b.astype(BF16), (((0,), (0,)), ((), ())), preferred_element_type=F32)


def _const_spec(shape):
    return pl.BlockSpec(shape, lambda *_: (0,) * len(shape))


def _mod_kernel(c_ref, w_ref, b_ref, o_ref):
    c = c_ref[...]
    o_ref[0] = _bdot(c * jax.nn.sigmoid(c), w_ref[0]) + b_ref[0]


def _modulation(cc, w_mod, b_mod):
    depth, d, n = w_mod.shape
    tn = n // 4
    return pl.pallas_call(
        _mod_kernel,
        grid=(depth, n // tn),
        in_specs=[pl.BlockSpec(cc.shape, lambda l, j: (0, 0)),
                  pl.BlockSpec((1, d, tn), lambda l, j: (l, 0, j)),
                  pl.BlockSpec((1, 1, tn), lambda l, j: (l, 0, j))],
        out_specs=pl.BlockSpec((1, cc.shape[0], tn), lambda l, j: (l, 0, j)),
        out_shape=jax.ShapeDtypeStruct((depth, cc.shape[0], n), F32),
        compiler_params=_cparams(2),
        name="modulation",
    )(cc, w_mod.astype(BF16), b_mod.reshape(depth, 1, n))


def _rms_modulate(x, g, shift, scale):
    y = x * lax.rsqrt(jnp.mean(x * x, axis=-1, keepdims=True) + NORM_EPS) * g
    return y * (1.0 + scale) + shift


def _inproj_kernel(x_ref, sh_ref, sc_ref, g_ref, w_hy_ref, w_gdn_ref, hy_cw_ref, hy_cb_ref, gdn_cw_ref, *refs,
                   seg_len, q_scale):
    n_plain = (len(refs) - 4) // 2
    w_refs, plain_refs = refs[:n_plain], refs[n_plain:2 * n_plain]
    u_ref, q_ref, k_ref, v_ref = refs[2 * n_plain:]
    hb = _rms_modulate(x_ref[0], g_ref[...], sh_ref[0], sc_ref[0]).astype(BF16)
    piece = 2 * LANES
    width = q_ref.shape[-1]
    z_hy = [jnp.dot(hb, w_hy_ref[:, c0:c0 + piece], preferred_element_type=F32)
            for c0 in range(0, u_ref.shape[-1], piece)]
    z_gdn = [jnp.dot(hb, w_gdn_ref[:, c0:c0 + piece], preferred_element_type=F32)
             for c0 in range(0, 3 * width, piece)]
    for w_ref, o_ref in zip(w_refs, plain_refs):
        o_ref[0] = jnp.dot(hb, w_ref[...], preferred_element_type=F32)
    for i, c0 in enumerate(range(0, u_ref.shape[-1], piece)):
        cols = slice(c0, c0 + piece)
        u_ref[0, :, cols] = _seg_conv(z_hy[i], hy_cw_ref[:, cols], seg_len) + hy_cb_ref[:, cols]
    for part, (o_ref, norm_scale) in enumerate(((q_ref, q_scale), (k_ref, 1.0), (v_ref, None))):
        for c0 in range(0, width, piece):
            cols = slice(part * width + c0, part * width + c0 + piece)
            y = _seg_conv(z_gdn[(part * width + c0) // piece], gdn_cw_ref[:, cols], seg_len)
            y = y * jax.nn.sigmoid(y)
            for h0 in range(0, piece, GDN_DP):
                t = y[:, h0:h0 + GDN_DP]
                if norm_scale is not None:
                    t = t * (lax.rsqrt(jnp.sum(t * t, axis=-1, keepdims=True) + NORM_EPS) * norm_scale)
                o_ref[0, :, c0 + h0:c0 + h0 + GDN_DP] = t


def _inproj(x, shift, scale, g, plain_weights, w_hy, w_gdn, hy_cw, hy_cb, gdn_cw, seg_len, q_scale, tm):
    b, l, d = x.shape
    hyw, gdw = w_hy.shape[1], w_gdn.shape[1] // 3
    vec = pl.BlockSpec((1, 1, d), lambda i, j: (i, 0, 0))
    widths = [w.shape[1] for w in plain_weights] + [hyw, gdw, gdw, gdw]
    return pl.pallas_call(
        functools.partial(_inproj_kernel, seg_len=seg_len, q_scale=q_scale),
        grid=(b, l // tm),
        in_specs=[pl.BlockSpec((1, tm, d), lambda i, j: (i, j, 0)), vec, vec, _const_spec((1, d)),
                  _const_spec(w_hy.shape), _const_spec(w_gdn.shape), _const_spec(hy_cw.shape),
                  _const_spec((1, hyw)), _const_spec(gdn_cw.shape)]
        + [_const_spec(w.shape) for w in plain_weights],
        out_specs=[pl.BlockSpec((1, tm, w), lambda i, j: (i, j, 0)) for w in widths],
        out_shape=[jax.ShapeDtypeStruct((b, l, w), F32) for w in widths],
        compiler_params=_cparams(2),
        name="inproj",
    )(x, shift, scale, g.reshape(1, d), w_hy, w_gdn, hy_cw, hy_cb.reshape(1, hyw), gdn_cw, *plain_weights)


def _seg_conv(z, w, seg_len):
    rows, width = z.shape
    taps = w.shape[0]
    half = taps // 2
    pos = lax.broadcasted_iota(jnp.int32, (rows, LANES), 0) % seg_len
    valid = {d: (pos >= -d) if d < 0 else (pos < seg_len - d) for d in range(-half, half + 1) if d != 0}
    tiles = []
    for c0 in range(0, width, LANES):
        cols = slice(c0, c0 + LANES)
        zc = z[:, cols]
        acc = zc * w[half:half + 1, cols]
        for k in range(taps):
            d = k - half
            if d != 0:
                shifted = pltpu.roll(zc, (-d) % rows, 0)
                acc = acc + jnp.where(valid[d], shifted, 0.0) * w[k:k + 1, cols]
        tiles.append(acc)
    return jnp.concatenate(tiles, axis=1)


def _hy_filter_kernel(z_ref, w1_ref, b1_ref, w2_ref, b2_ref, w3_ref, f_ref, dl_ref, hf_ref, hb_ref):
    hy_c = dl_ref.shape[1]
    z = z_ref[...]
    fdot = lambda a, b: jnp.dot(a, b, precision=HIGHEST, preferred_element_type=F32)
    h = jnp.sin(f_ref[0:1, :] * (fdot(z, w1_ref[...]) + b1_ref[...]))
    h = jnp.sin(f_ref[1:2, :] * (fdot(h, w2_ref[...]) + b2_ref[...]))
    h = fdot(h, w3_ref[...])
    window = jnp.exp(-z[:, 0:1] * dl_ref[...])
    row = lax.broadcasted_iota(jnp.int32, window.shape, 0) + pl.program_id(0) * z.shape[0]
    for o in range(HY_ORDER):
        base = 2 * o * hy_c
        hf_ref[:, o * hy_c:(o + 1) * hy_c] = h[:, base:base + hy_c] * window
        hb_ref[:, o * hy_c:(o + 1) * hy_c] = jnp.where(row > 0, h[:, base + hy_c:base + 2 * hy_c] * window, 0.0)


def _hy_filters(z, w1, b1, w2, b2, w3, sin_freq, deltas, tl):
    l, emb = z.shape
    fh = w2.shape[0]
    hy_c = deltas.shape[1]
    out = jax.ShapeDtypeStruct((l, HY_ORDER * hy_c), F32)
    rows = lambda w: pl.BlockSpec((tl, w), lambda i: (i, 0))
    return pl.pallas_call(
        _hy_filter_kernel,
        grid=(l // tl,),
        in_specs=[rows(emb), _const_spec((emb, fh)), _const_spec((1, fh)), _const_spec((fh, fh)), _const_spec((1, fh)),
                  _const_spec(w3.shape), _const_spec((2, fh)), _const_spec((1, hy_c))],
        out_specs=[rows(HY_ORDER * hy_c)] * 2,
        out_shape=[out, out],
        compiler_params=_cparams(1),
        name="hy_filters",
    )(z, jnp.pad(w1, ((0, emb - w1.shape[0]), (0, 0))), b1.reshape(1, fh), w2, b2.reshape(1, fh), w3, sin_freq, deltas)


def _hy_embedding(length, emb):
    t = jnp.linspace(0.0, 1.0, length, dtype=F32)[:, None]
    bands = (HY_EMB - 1) // 2
    f = jnp.linspace(1e-4, bands - 1, bands, dtype=F32)
    w = 2 * math.pi * jnp.arange(length, dtype=F32) / length
    ang = w[:, None] * f[None, :]
    z = jnp.concatenate([t, jnp.cos(ang), -jnp.sin(ang)], axis=-1)
    return jnp.pad(z, ((0, 0), (0, emb - z.shape[1])))


def _fft_tables(n1, n2):
    n = n1 * n2
    nk = n1 // 2 + 1
    kp = -(-nk // 8) * 8
    s1 = np.arange(n1 // 2)[None, :, None]
    s2 = np.arange(n2)[:, None, None]
    k1 = np.arange(kp)[None, None, :]
    theta = 2.0 * np.pi * ((k1 * (n2 * s1 + s2)) % n) / n
    tt_fwd = np.concatenate([np.cos(theta), -np.sin(theta)], axis=-1)
    weight = np.where(np.arange(kp) < nk, 2.0, 0.0)
    weight[0] = weight[nk - 1] = 1.0
    tt_inv = np.concatenate([np.cos(theta) * weight, -np.sin(theta) * weight], axis=-1)
    phi = 2.0 * np.pi * ((np.arange(n2)[:, None] * np.arange(n2)[None, :]) % n2) / n2
    c2, s2m = np.cos(phi), np.sin(phi)
    fwd = np.block([[c2, s2m], [-s2m, c2]])
    inv = np.block([[c2, -s2m], [s2m, c2]])
    return tuple(jnp.asarray(t, F32) for t in (tt_fwd, tt_inv, fwd, inv))


def _for_k1(body, nk):
    lax.fori_loop(0, nk - 1, lambda k1, c: body(k1) or c, 0, unroll=min(FFT_UNROLL, nk - 1))
    body(nk - 1)


def _time_pitch(n2):
    return n2 + PITCH_PAD


def _spec_pitch(n2):
    return 2 * n2 + PITCH_PAD


def _dft_load_time(load_block, time_ref, n1, n2):
    def copy(s1, carry):
        time_ref[pl.ds(pl.multiple_of(s1 * _time_pitch(n2), 8), n2), :] = load_block(
            pl.ds(pl.multiple_of(s1 * n2, n2), n2))
        return carry

    lax.fori_loop(0, n1 // 2, copy, 0, unroll=min(FFT_UNROLL, n1 // 2))


def _dft_stage_a(time_ref, tt_ref, spec_ref, n1, n2):
    kp = tt_ref.shape[-1] // 2

    def stage_a(s2, carry):
        slab = time_ref[pl.ds(s2, n1 // 2, stride=_time_pitch(n2)), :]
        res = _bdot_tn(tt_ref[s2], slab)
        spec_ref[pl.ds(s2, kp, stride=_spec_pitch(n2)), :] = res[:kp]
        spec_ref[pl.ds(n2 + s2, kp, stride=_spec_pitch(n2)), :] = res[kp:]
        return carry

    lax.fori_loop(0, n2, stage_a, 0, unroll=2 * FFT_UNROLL)


def _aligned_rows(start, size, align):
    return pl.ds(start if isinstance(start, int) else pl.multiple_of(start, align), size)


def _spec_rows(k1, n2):
    return _aligned_rows(k1 * _spec_pitch(n2), 2 * n2, 8)


def _spectrum_kernel(hf_ref, hb_ref, tt_ref, fwd_ref, hr_ref, hi_ref, time_ref, spec_ref, *, n1, n2):
    inv_n = 1.0 / (n1 * n2)
    fwd = fwd_ref[...].astype(BF16)
    for sign, src_ref in ((1.0, hf_ref), (-1.0, hb_ref)):
        _dft_load_time(lambda rows: src_ref[rows, :], time_ref, n1, n2)
        _dft_stage_a(time_ref, tt_ref, spec_ref, n1, n2)

        def stage_b(k1):
            rows = _spec_rows(k1, n2)
            spec = jnp.dot(fwd, spec_ref[rows, :].astype(BF16), preferred_element_type=F32) * inv_n
            hrows = _aligned_rows(k1 * n2, n2, n2)
            if sign > 0:
                hr_ref[hrows, :] = spec[:n2]
                hi_ref[hrows, :] = spec[n2:]
            else:
                hr_ref[hrows, :] += spec[:n2]
                hi_ref[hrows, :] -= spec[n2:]

        _for_k1(stage_b, n1 // 2 + 1)


def _filter_spectrum(hf, hb, tables, n1, n2):
    tt, _, fwd, _ = tables
    l, c = hf.shape
    n = (n1 // 2 + 1) * n2
    chan = lambda rows: pl.BlockSpec((rows, LANES), lambda j: (0, j))
    return pl.pallas_call(
        functools.partial(_spectrum_kernel, n1=n1, n2=n2),
        grid=(c // LANES,),
        in_specs=[chan(l), chan(l), _const_spec(tt.shape), _const_spec(fwd.shape)],
        out_specs=[chan(n), chan(n)],
        out_shape=[jax.ShapeDtypeStruct((n, c), F32)] * 2,
        scratch_shapes=_fft_scratch(tt, n1, n2),
        compiler_params=_cparams(1),
        name="hy_spectrum",
    )(hf, hb, tt, fwd)


def _fft_scratch(tt, n1, n2):
    kp = tt.shape[-1] // 2
    return [pltpu.VMEM((n1 // 2 * _time_pitch(n2), LANES), F32), pltpu.VMEM((kp * _spec_pitch(n2), LANES), F32)]


def _fftconv_kernel(u_ref, gate_ref, hr_ref, hi_ref, d_ref, ttf_ref, tti_ref, fwd_ref, inv_ref, o_ref,
                    time_ref, spec_ref, *, n1, n2):
    n1h = n1 // 2
    kp = ttf_ref.shape[-1] // 2
    fwd = fwd_ref[...].astype(BF16)
    inv = inv_ref[...].astype(BF16)
    _dft_load_time(lambda rows: u_ref[0, rows, :], time_ref, n1, n2)
    _dft_stage_a(time_ref, ttf_ref, spec_ref, n1, n2)

    def stage_b(k1s):
        rows = [_spec_rows(k1, n2) for k1 in k1s]
        slab = jnp.concatenate([spec_ref[r, :].astype(BF16) for r in rows], axis=1)
        spec = jnp.dot(fwd, slab, preferred_element_type=F32)
        br, bi = spec[:n2], spec[n2:]
        hrows = [_aligned_rows(k1 * n2, n2, n2) for k1 in k1s]
        hr = jnp.concatenate([hr_ref[r, :] for r in hrows], axis=1)
        hi = jnp.concatenate([hi_ref[r, :] for r in hrows], axis=1)
        y = jnp.concatenate([br * hr - bi * hi, br * hi + bi * hr], axis=0)
        g = jnp.dot(inv, y.astype(BF16), preferred_element_type=F32)
        for i, r in enumerate(rows):
            spec_ref[r, :] = g[:, i * LANES:(i + 1) * LANES]

    lax.fori_loop(0, n1h // 2, lambda kk, c: stage_b((2 * kk, 2 * kk + 1)) or c, 0,
                  unroll=min(FFT_UNROLL // 2, n1h // 2))
    stage_b((n1h,))

    def stage_a_inv(s2, carry):
        g = jnp.concatenate([spec_ref[pl.ds(s2, kp, stride=_spec_pitch(n2)), :],
                             spec_ref[pl.ds(n2 + s2, kp, stride=_spec_pitch(n2)), :]], axis=0)
        time_ref[pl.ds(s2, n1h, stride=_time_pitch(n2)), :] = _bdot(tti_ref[s2], g)
        return carry

    lax.fori_loop(0, n2, stage_a_inv, 0, unroll=2 * FFT_UNROLL)

    def gate_rows(s1, carry):
        rows = pl.ds(pl.multiple_of(s1 * n2, n2), n2)
        conv = time_ref[pl.ds(pl.multiple_of(s1 * _time_pitch(n2), 8), n2), :]
        o_ref[0, rows, :] = gate_ref[0, rows, :] * (conv + d_ref[...] * u_ref[0, rows, :])
        return carry

    lax.fori_loop(0, n1h, gate_rows, 0, unroll=min(FFT_UNROLL, n1h))


def _fftconv_gate(u_arr, u_blk, gate_arr, gate_blk, hr, hi, h_blk, d, tables, n1, n2):
    tt_fwd, tt_inv, fwd, inv = tables
    b, l, _ = u_arr.shape
    n = hr.shape[0]
    c = d.shape[0]
    ncb = c // LANES
    seq = lambda off: pl.BlockSpec((1, l, LANES), lambda j, i: (i, 0, off + j))
    chan = lambda rows, off=0: pl.BlockSpec((rows, LANES), lambda j, i: (0, off + j))
    return pl.pallas_call(
        functools.partial(_fftconv_kernel, n1=n1, n2=n2),
        grid=(ncb, b),
        in_specs=[seq(u_blk), seq(gate_blk), chan(n, h_blk), chan(n, h_blk), chan(1),
                  _const_spec(tt_fwd.shape), _const_spec(tt_inv.shape), _const_spec(fwd.shape),
                  _const_spec(inv.shape)],
        out_specs=pl.BlockSpec((1, l, LANES), lambda j, i: (i, 0, j)),
        out_shape=jax.ShapeDtypeStruct((b, l, c), F32),
        scratch_shapes=_fft_scratch(tt_fwd, n1, n2),
        compiler_params=_cparams(2),
        name="hy_fftconv",
    )(u_arr, gate_arr, hr, hi, d.reshape(1, c), tt_fwd, tt_inv, fwd, inv)


def _dft_tables(l):
    n = 2 * l
    ang = 2.0 * np.pi * ((np.arange(l)[:, None] * np.arange(l)[None, :]) % n) / n
    cf = np.cos(ang)
    sf = -np.sin(ang)
    sf[0, :] = (-1.0) ** np.arange(l)
    fwd = np.concatenate([cf, sf], axis=0)
    return jnp.asarray(fwd, F32), jnp.asarray(fwd.T, F32)


def _dftconv_kernel(u_ref, gate_ref, p_ref, q_ref, r_ref, d_ref, fwd_ref, inv_ref, o_ref):
    u = u_ref[0]
    l = u.shape[0]
    spec = _bdot(fwd_ref[...], u)
    xr, xi = spec[:l], spec[l:]
    q = q_ref[...]
    y = jnp.concatenate([xr * p_ref[...] - xi * q, xr * q + xi * r_ref[...]], axis=0)
    conv = _bdot(inv_ref[...], y)
    o_ref[0] = gate_ref[0] * (conv + d_ref[...] * u)


def _dense_spectrum_kernel(hf_ref, hb_ref, fwd_ref, p_ref, q_ref, r_ref):
    l = hf_ref.shape[0]
    xf = _bdot(fwd_ref[...], hf_ref[...])
    xb = _bdot(fwd_ref[...], hb_ref[...])
    first = lax.broadcasted_iota(jnp.int32, (l, hf_ref.shape[1]), 0) == 0
    scale = jnp.where(first, 0.5 / l, 1.0 / l)
    re = (xf[:l] + xb[:l]) * scale
    p_ref[...] = re
    q_ref[...] = jnp.where(first, 0.0, (xf[l:] - xb[l:]) * scale)
    r_ref[...] = jnp.where(first, (xf[l:] + xb[l:]) * scale, re)


def _dense_filter_spectrum(hf, hb, tables):
    fwd, _ = tables
    out = jax.ShapeDtypeStruct(hf.shape, F32)
    return pl.pallas_call(
        _dense_spectrum_kernel,
        grid=(1,),
        in_specs=[_const_spec(hf.shape), _const_spec(hb.shape), _const_spec(fwd.shape)],
        out_specs=[_const_spec(hf.shape)] * 3,
        out_shape=[out] * 3,
        compiler_params=_cparams(1),
        name="hy_dense_spectrum",
    )(hf, hb, fwd)


def _dftconv_gate(u_arr, u_blk, gate_arr, gate_blk, p, q, r, h_blk, d, tables):
    fwd, inv = tables
    b, l, _ = u_arr.shape
    c = d.shape[0]
    seq = lambda off: pl.BlockSpec((1, l, c), lambda i: (i, 0, off))
    coef = pl.BlockSpec((l, c), lambda i: (0, h_blk))
    return pl.pallas_call(
        _dftconv_kernel,
        grid=(b,),
        in_specs=[seq(u_blk), seq(gate_blk), coef, coef, coef,
                  _const_spec((1, c)), _const_spec(fwd.shape), _const_spec(inv.shape)],
        out_specs=pl.BlockSpec((1, l, c), lambda i: (i, 0, 0)),
        out_shape=jax.ShapeDtypeStruct((b, l, c), F32),
        compiler_params=_cparams(1),
        name="hy_dftconv",
    )(u_arr, gate_arr, p, q, r, d.reshape(1, c), fwd, inv)


GDN_GF, GDN_GB, GDN_BF, GDN_BB = 32, 36, 40, 44

def _chunk_cumsum_mats(tl):
    r = np.arange(tl)
    same = (r[:, None] // CHUNK) == (r[None, :] // CHUNK)
    lower = same & (r[None, :] <= r[:, None])
    upper = same & (r[None, :] >= r[:, None])
    return jnp.asarray(lower, F32), jnp.asarray(upper, F32)


def _tri_cumsum(tri, x):
    hi = x.astype(BF16)
    rest = x - hi.astype(F32)
    mid = rest.astype(BF16)
    lo = (rest - mid.astype(F32)).astype(BF16)
    dot = lambda part: jnp.dot(tri, part, preferred_element_type=F32)
    return dot(hi) + dot(mid) + dot(lo)


def _gates_kernel(small_ref, w_ref, b_ref, nea_ref, dtb_ref, lo_ref, up_ref, bf_ref, bb_ref, gc_ref):
    small = small_ref[0]
    kw = bf_ref.shape[-1]
    log_decay = nea_ref[...] * jax.nn.softplus(small + dtb_ref[...])
    cums = []
    for d, tri_ref in enumerate((lo_ref, up_ref)):
        pre = jnp.dot(small[:, d * GLA_RANK:(d + 1) * GLA_RANK], w_ref[d], precision=HIGHEST,
                      preferred_element_type=F32) + b_ref[d]
        vals = jnp.concatenate([jax.nn.log_sigmoid(pre) * (1.0 / GLA_TAU), log_decay], axis=1)
        cums.append(_tri_cumsum(tri_ref[...].astype(BF16), vals))
    bf_ref[0] = cums[0][:, :kw]
    bb_ref[0] = cums[1][:, :kw]
    lane = lax.broadcasted_iota(jnp.int32, small.shape, 1)
    gc_ref[0] = jnp.where(lane < GDN_GB, cums[0][:, kw:], jnp.where(lane < GDN_BF, cums[1][:, kw:],
                                                                  jax.nn.sigmoid(small)))


def _gates(small, w_a2, b_a, a_log, dt_bias, tl):
    b, l, sw = small.shape
    kw = w_a2.shape[-1]
    nea = jnp.zeros((1, LANES), F32).at[0, GDN_GF:GDN_BF].set(-jnp.exp(a_log.reshape(-1)))
    dtb = jnp.zeros((1, LANES), F32).at[0, GDN_GF:GDN_BF].set(dt_bias.reshape(-1))
    lower, upper = _chunk_cumsum_mats(tl)
    tok = lambda w: pl.BlockSpec((1, tl, w), lambda i, j: (i, j, 0))
    return pl.pallas_call(
        _gates_kernel,
        grid=(b, l // tl),
        in_specs=[tok(sw), _const_spec(w_a2.shape), _const_spec((2, 1, kw)), _const_spec((1, LANES)),
                  _const_spec((1, LANES)), _const_spec((tl, tl)), _const_spec((tl, tl))],
        out_specs=[tok(kw), tok(kw), tok(LANES)],
        out_shape=[jax.ShapeDtypeStruct((b, l, kw), F32)] * 2 + [jax.ShapeDtypeStruct((b, l, LANES), F32)],
        compiler_params=_cparams(2),
        name="scan_gates",
    )(small, w_a2, b_a.reshape(2, 1, kw), nea, dtb, lower, upper)


def _gla_scan_kernel(qkf_ref, vf_ref, bf_ref, qkb_ref, vb_ref, bb_ref, s0f_ref, s0b_ref, hm_ref, vm_ref, bd_ref,
                     of_ref, ob_ref, sff_ref, sfb_ref, stf_ref, stb_ref, *, n_chunks, kw, q_scale):
    j = pl.program_id(1)

    @pl.when(j == 0)
    def _():
        stf_ref[...] = s0f_ref[0]
        stb_ref[...] = s0b_ref[0]

    ri = lax.broadcasted_iota(jnp.int32, (CHUNK, GLA_H * CHUNK), 0)
    ci = lax.broadcasted_iota(jnp.int32, (CHUNK, GLA_H * CHUNK), 1) % CHUNK
    hm = hm_ref[...]
    vm = vm_ref[...]
    bd = bd_ref[...]
    dirs = ((qkf_ref, vf_ref, bf_ref, of_ref, stf_ref, ri >= ci, CHUNK - 1, CHUNK // 2 - 1, False),
            (qkb_ref, vb_ref, bb_ref, ob_ref, stb_ref, ci >= ri, 0, CHUNK // 2, True))

    qts, kstacks, vstacks, keeps, qes, vs, kds, decs, sinks = [], [], [], [], [], [], [], [], []
    for qk_ref, v_ref, b_ref, o_ref, st_ref, keep, last, mid, rev in dirs:
        for c in range(n_chunks):
            rows = slice((n_chunks - 1 - c if rev else c) * CHUNK, (n_chunks - c if rev else c + 1) * CHUNK)
            qk = qk_ref[0, rows, :]
            q, k = qk[:, :kw] * q_scale, qk[:, kw:]
            v = v_ref[0, rows, :]
            bc = b_ref[0, rows, :]
            b_mid = bc[mid:mid + 1, :]
            b_last = bc[last:last + 1, :]
            kt = k * jnp.exp(b_mid - bc)
            qts.append(q * jnp.exp(bc - b_mid))
            kstacks.append(jnp.concatenate([kt * hm[h:h + 1, :] for h in range(GLA_H)], axis=0).astype(BF16))
            vstacks.append(jnp.concatenate([v * vm[h:h + 1, :] for h in range(GLA_H)], axis=0).astype(BF16))
            keeps.append(keep)
            qes.append((q * jnp.exp(bc)).astype(BF16))
            vs.append(v)
            kds.append(k * jnp.exp(b_last - bc))
            decs.append(jnp.exp(b_last))
            sinks.append((o_ref, rows))
    attns = [jnp.where(keep, _bdot_nt(qt, ks), 0.0) for keep, qt, ks in zip(keeps, qts, kstacks)]
    o_intra = [_bdot(a, vst) for a, vst in zip(attns, vstacks)]
    upds = [_bdot_tn(v, kd) * bd for v, kd in zip(vs, kds)]

    for d, (_, _, _, _, st_ref, _, _, _, _) in enumerate(dirs):
        st = st_ref[...]
        for c in range(n_chunks):
            i = d * n_chunks + c
            o_ref, rows = sinks[i]
            o_ref[0, rows, :] = o_intra[i] + _bdot_nt(qes[i], st)
            st = st * decs[i] + upds[i]
        st_ref[...] = st

    @pl.when(j == pl.num_programs(1) - 1)
    def _():
        sff_ref[0] = stf_ref[...]
        sfb_ref[0] = stb_ref[...]


def _gla_scan(qk, v, b_f, b_b, s0_f, s0_b, tl):
    b, l, kw2 = qk.shape
    kw, vw = kw2 // 2, v.shape[-1]
    dk, dv = kw // GLA_H, vw // GLA_H
    nblk = l // tl
    heads_k = np.arange(kw) // dk
    heads_v = np.arange(vw) // dv
    hm = jnp.asarray(np.arange(8)[:, None] == heads_k[None, :], F32)
    vm = jnp.asarray(np.arange(8)[:, None] == heads_v[None, :], F32)
    bd = jnp.asarray(heads_v[:, None] == heads_k[None, :], F32)
    fwd = lambda w: pl.BlockSpec((1, tl, w), lambda i, j: (i, j, 0))
    bwd = lambda w: pl.BlockSpec((1, tl, w), lambda i, j: (i, nblk - 1 - j, 0))
    state = pl.BlockSpec((1, vw, kw), lambda i, j: (i, 0, 0))
    return pl.pallas_call(
        functools.partial(_gla_scan_kernel, n_chunks=tl // CHUNK, kw=kw, q_scale=dk ** -0.5),
        grid=(b, nblk),
        in_specs=[fwd(kw2), fwd(vw), fwd(kw), bwd(kw2), bwd(vw), bwd(kw), state, state,
                  _const_spec(hm.shape), _const_spec(vm.shape), _const_spec(bd.shape)],
        out_specs=[fwd(vw), bwd(vw), state, state],
        out_shape=[jax.ShapeDtypeStruct((b, l, vw), F32)] * 2 + [jax.ShapeDtypeStruct((b, vw, kw), F32)] * 2,
        scratch_shapes=[pltpu.VMEM((vw, kw), F32)] * 2,
        compiler_params=_cparams(2),
        name="gla_scan",
    )(qk, v, b_f, qk, v, b_b, s0_f, s0_b, hm, vm, bd)


def _unit_tri_inverses(mats, eye, m16, m32, m64):
    diag = [a * m16 for a in mats]
    inv = [eye + a for a in diag]
    pw = [a.astype(BF16) for a in diag]
    for _ in range(3):
        pw = [_bdot(p, p).astype(BF16) for p in pw]
        inv = [t + _bdot(t, p) for t, p in zip(inv, pw)]
    for mask in (m32, m64):
        inv_b = [t.astype(BF16) for t in inv]
        mid = [_bdot(t, a * mask) for t, a in zip(inv_b, mats)]
        inv = [t + _bdot(m, tb) for t, m, tb in zip(inv, mid, inv_b)]
    return inv


def _gdn_scan_kernel(qf_ref, kf_ref, vf_ref, gcf_ref, qb_ref, kb_ref, vb_ref, gcb_ref,
                     s0f_ref, s0b_ref, of_ref, ob_ref, sff_ref, sfb_ref, sf_ref, sb_ref, *, tl):
    j = pl.program_id(1)

    @pl.when(j == 0)
    def _():
        sf_ref[...] = s0f_ref[0]
        sb_ref[...] = s0b_ref[0]

    nck = tl // CHUNK
    ri = lax.broadcasted_iota(jnp.int32, (tl, tl), 0)
    ci = lax.broadcasted_iota(jnp.int32, (tl, tl), 1)
    same = lambda n: (ri // n) == (ci // n)
    chunk = same(CHUNK)
    eye = (ri == ci).astype(F32)
    m16 = same(16).astype(F32)
    m32 = (same(32) & ~same(16)).astype(F32)
    m64 = (chunk & ~same(32)).astype(F32)
    dirs = ((qf_ref, kf_ref, vf_ref, gcf_ref, of_ref, sf_ref, ri > ci, ri >= ci, CHUNK - 1, GDN_GF, GDN_BF, False),
            (qb_ref, kb_ref, vb_ref, gcb_ref, ob_ref, sb_ref, ci > ri, ci >= ri, 0, GDN_GB, GDN_BB, True))

    qs, ks, vs, gcums, betas, stricts, incls, glasts, outs = [], [], [], [], [], [], [], [], []
    for q_ref, k_ref, v_ref, gc_ref, o_ref, s_ref, strict, incl, last, g_col, b_col, rev in dirs:
        gc = gc_ref[0]
        gct = gc.T
        for h in range(GDN_H):
            cols = slice(h * GDN_DP, (h + 1) * GDN_DP)
            gcum = gc[:, g_col + h:g_col + h + 1]
            decay = jnp.exp(jnp.minimum(gcum - gct[g_col + h:g_col + h + 1, :], 0.0))
            qs.append(q_ref[0, :, cols])
            ks.append(k_ref[0, :, cols])
            vs.append(v_ref[0, :, cols])
            gcums.append(gcum)
            betas.append(gc[:, b_col + h:b_col + h + 1])
            stricts.append(jnp.where(chunk & strict, decay, 0.0))
            incls.append(jnp.where(chunk & incl, decay, 0.0))
            glasts.append(jnp.concatenate(
                [jnp.broadcast_to(gcum[c * CHUNK + last:c * CHUNK + last + 1, :], (CHUNK, 1)) for c in range(nck)],
                axis=0))
            outs.append((o_ref, s_ref, h, cols, rev))
    n_pairs = len(qs)
    e_cols = [jnp.exp(g) for g in gcums]
    kbs = [k * b for k, b in zip(ks, betas)]
    mats = [-_bdot_nt(kb, k) * d for kb, k, d in zip(kbs, ks, stricts)]
    invs = _unit_tri_inverses(mats, eye, m16, m32, m64)
    wus = [_bdot(t, jnp.concatenate([kb * e, v * b], axis=1))
           for t, kb, e, v, b in zip(invs, kbs, e_cols, vs, betas)]
    attns = [_bdot_nt(q, k) * d for q, k, d in zip(qs, ks, incls)]
    awus = [_bdot(a, wu) for a, wu in zip(attns, wus)]
    q_effs = [(q * e - awu[:, :GDN_DP]).astype(BF16) for q, e, awu in zip(qs, e_cols, awus)]
    ws = [wu[:, :GDN_DP].astype(BF16) for wu in wus]
    us = [wu[:, GDN_DP:] for wu in wus]
    o_intra = [awu[:, GDN_DP:] for awu in awus]
    kds = [(k * jnp.exp(gl - g)).astype(BF16) for k, gl, g in zip(ks, glasts, gcums)]
    decs = [jnp.exp(gl) for gl in glasts]

    states = [s_ref[h] for (_, s_ref, h, _, _) in outs]
    for c in range(nck):
        rows = [slice((nck - 1 - c if rev else c) * CHUNK, (nck - c if rev else c + 1) * CHUNK)
                for (_, _, _, _, rev) in outs]
        prods = [jnp.dot(jnp.concatenate([q_effs[i][rows[i]], ws[i][rows[i]]], axis=0), states[i].astype(BF16),
                         preferred_element_type=F32) for i in range(n_pairs)]
        for i, (o_ref, _, _, cols, _) in enumerate(outs):
            o_ref[0, rows[i], cols] = o_intra[i][rows[i]] + prods[i][:CHUNK]
        v_news = [us[i][rows[i]] - prods[i][CHUNK:] for i in range(n_pairs)]
        states = [decs[i][rows[i]][0:1] * states[i] + _bdot_tn(kds[i][rows[i]], v_news[i]) for i in range(n_pairs)]
    for i, (_, s_ref, h, _, _) in enumerate(outs):
        s_ref[h] = states[i]

    @pl.when(j == pl.num_programs(1) - 1)
    def _():
        sff_ref[0] = sf_ref[...]
        sfb_ref[0] = sb_ref[...]


def _gdn_scan(q, k, v, gc, s0_f, s0_b, tl):
    b, l, width = q.shape
    nblk = l // tl
    fwd = lambda w: pl.BlockSpec((1, tl, w), lambda i, j: (i, j, 0))
    bwd = lambda w: pl.BlockSpec((1, tl, w), lambda i, j: (i, nblk - 1 - j, 0))
    state = pl.BlockSpec((1, GDN_H, GDN_DP, GDN_DP), lambda i, j: (i, 0, 0, 0))
    st_shape = jax.ShapeDtypeStruct((b, GDN_H, GDN_DP, GDN_DP), F32)
    return pl.pallas_call(
        functools.partial(_gdn_scan_kernel, tl=tl),
        grid=(b, nblk),
        in_specs=[fwd(width), fwd(width), fwd(width), fwd(LANES),
                  bwd(width), bwd(width), bwd(width), bwd(LANES), state, state],
        out_specs=[fwd(width), bwd(width), state, state],
        out_shape=[jax.ShapeDtypeStruct((b, l, width), F32)] * 2 + [st_shape] * 2,
        scratch_shapes=[pltpu.VMEM((GDN_H, GDN_DP, GDN_DP), F32)] * 2,
        compiler_params=_cparams(2),
        name="gdn_scan",
    )(q, k, v, gc, q, k, v, gc, s0_f, s0_b)


def _gated_head_norm(o, gate, g_norm, ones_bd, inv_d):
    sq = o * o
    hi = sq.astype(BF16)
    lo = (sq - hi.astype(F32)).astype(BF16)
    ms = (jnp.dot(hi, ones_bd, preferred_element_type=F32) + jnp.dot(lo, ones_bd, preferred_element_type=F32)) * inv_d
    return o * lax.rsqrt(ms + NORM_EPS) * g_norm * (gate * jax.nn.sigmoid(gate))


def _outproj_kernel(x_ref, g1_ref, hy_ref, glf_ref, glb_ref, glg_ref, gdf_ref, gdb_ref, gdg_ref,
                    gln_ref, gdn_ref, glm_ref, gdm_ref, why_ref, wgl_ref, wgd_ref, o_ref, *, gla_dv, gdn_d):
    y_gla = _gated_head_norm(glf_ref[0] + glb_ref[0], glg_ref[0], gln_ref[...], glm_ref[...], 1.0 / gla_dv)
    y_gdn = _gated_head_norm(gdf_ref[0] + gdb_ref[0], gdg_ref[0], gdn_ref[...], gdm_ref[...], 1.0 / gdn_d)
    acc = _bdot(hy_ref[0], why_ref[...]) + _bdot(y_gla, wgl_ref[...]) + _bdot(y_gdn, wgd_ref[...])
    o_ref[0] = x_ref[0] + g1_ref[0] * acc


def _outproj(x, g1, hy, gla_f, gla_b, gla_gate, gdn_f, gdn_b, gdn_gate, gla_g, gdn_g, w_hy, w_gla, w_gdn,
             gla_dv, gdn_d, tm):
    b, l, d = x.shape
    hyw, glw, gdw = hy.shape[-1], gla_f.shape[-1], gdn_f.shape[-1]
    gl_heads = np.arange(glw) // gla_dv
    gd_heads = np.arange(gdw) // GDN_DP
    gl_m = jnp.asarray(gl_heads[:, None] == gl_heads[None, :], BF16)
    gd_m = jnp.asarray(gd_heads[:, None] == gd_heads[None, :], BF16)
    tok = lambda w: pl.BlockSpec((1, tm, w), lambda i, j: (i, j, 0))
    return pl.pallas_call(
        functools.partial(_outproj_kernel, gla_dv=gla_dv, gdn_d=gdn_d),
        grid=(b, l // tm),
        in_specs=[tok(d), pl.BlockSpec((1, 1, d), lambda i, j: (i, 0, 0)), tok(hyw), tok(glw), tok(glw), tok(glw),
                  tok(gdw), tok(gdw), tok(gdw), _const_spec((1, glw)), _const_spec((1, gdw)),
                  _const_spec(gl_m.shape), _const_spec(gd_m.shape),
                  _const_spec(w_hy.shape), _const_spec(w_gla.shape), _const_spec(w_gdn.shape)],
        out_specs=tok(d),
        out_shape=jax.ShapeDtypeStruct(x.shape, F32),
        compiler_params=_cparams(2),
        name="outproj",
    )(x, g1, hy, gla_f, gla_b, gla_gate, gdn_f, gdn_b, gdn_gate, gla_g, gdn_g, gl_m, gd_m, w_hy, w_gla, w_gdn)


def _mlp_kernel(x_ref, sh_ref, sc_ref, g2_ref, ng_ref, w1_ref, w2_ref, fg_ref, o_ref, hn_ref, acc_ref, *, final_norm):
    kk = pl.program_id(2)

    @pl.when(kk == 0)
    def _():
        hn_ref[...] = _rms_modulate(x_ref[0], ng_ref[...], sh_ref[0], sc_ref[0]).astype(BF16)
        acc_ref[...] = jnp.zeros_like(acc_ref)

    hid = jnp.maximum(jnp.dot(hn_ref[...], w1_ref[...], preferred_element_type=F32), 0.0)
    acc_ref[...] += _bdot(hid * hid, w2_ref[...])

    @pl.when(kk == pl.num_programs(2) - 1)
    def _():
        y = x_ref[0] + g2_ref[0] * acc_ref[...]
        if final_norm:
            y = y * lax.rsqrt(jnp.mean(y * y, axis=-1, keepdims=True) + NORM_EPS) * fg_ref[...]
        o_ref[0] = y


def _mlp(x, shift, scale, gate, norm_g, w1, w2, final_g, final_norm, tm, th):
    b, l, d = x.shape
    dff = w1.shape[1]
    vec = pl.BlockSpec((1, 1, d), lambda i, j, k: (i, 0, 0))
    row = pl.BlockSpec((1, d), lambda i, j, k: (0, 0))
    return pl.pallas_call(
        functools.partial(_mlp_kernel, final_norm=final_norm),
        grid=(b, l // tm, dff // th),
        in_specs=[pl.BlockSpec((1, tm, d), lambda i, j, k: (i, j, 0)), vec, vec, vec, row,
                  pl.BlockSpec((d, th), lambda i, j, k: (0, k)), pl.BlockSpec((th, d), lambda i, j, k: (k, 0)), row],
        out_specs=pl.BlockSpec((1, tm, d), lambda i, j, k: (i, j, 0)),
        out_shape=jax.ShapeDtypeStruct(x.shape, F32),
        scratch_shapes=[pltpu.VMEM((tm, d), BF16), pltpu.VMEM((tm, d), F32)],
        compiler_params=_cparams(3),
        name="mlp",
    )(x, shift, scale, gate, norm_g.reshape(1, d), w1, w2, final_g.reshape(1, d))


def _pad_heads(w, n_heads, axis):
    shape = w.shape
    d = shape[axis] // n_heads
    w = w.reshape(shape[:axis] + (n_heads, d) + shape[axis + 1:])
    pad = [(0, 0)] * w.ndim
    pad[axis + 1] = (0, GDN_DP - d)
    w = jnp.pad(w, pad)
    return w.reshape(shape[:axis] + (n_heads * GDN_DP,) + shape[axis + 1:])


def kernel(x, c, ctx, c_ctx, norm1_g, norm2_g, w_mod, b_mod, w_in, w_out, hy_conv_w, hy_conv_b, hy_f_w1, hy_f_b1, hy_f_w2, hy_f_b2, hy_f_w3, hy_sin_freq, hy_d, gla_w_a2, gla_b_a, gla_norm_g, gdn_conv_w, gdn_a_log, gdn_dt_bias, gdn_norm_g, w_mlp1, w_mlp2, final_norm_g):
    batch, seq, d_model = x.shape
    ctx_len = ctx.shape[1]
    depth = w_mod.shape[0]
    hy_c = hy_d.shape[-1]
    gla_kw = gla_w_a2.shape[-1]
    gla_dv = gla_norm_g.shape[-1]
    gla_vw = GLA_H * gla_dv
    gdn_d = gdn_norm_g.shape[-1]
    gdn_w = GDN_H * gdn_d
    gdn_wp = GDN_H * GDN_DP

    n2 = LANES
    n1 = 2 * seq // n2
    fft_tabs = _fft_tables(n1, n2)
    dft_tabs = _dft_tables(ctx_len)
    emb_lat = _hy_embedding(seq, HY_EMB_PAD)
    emb_ctx = _hy_embedding(ctx_len, HY_EMB_PAD)
    hy_deltas = jnp.abs(jnp.linspace(math.log(HY_DECAY_TARGET) / HY_SLOW_PCT,
                                     math.log(HY_DECAY_TARGET) / HY_FAST_PCT, hy_c, dtype=F32))[None, :]

    pad_rows = -(batch + 1) % 8
    cc = jnp.concatenate([c, c_ctx[None, :], jnp.zeros((pad_rows, d_model), F32)], axis=0)
    mod = _modulation(cc, w_mod, b_mod)

    def mod_vecs(l, i):
        v = mod[l, :, i * d_model:(i + 1) * d_model]
        lat = v[:batch, None, :]
        cx = jnp.broadcast_to(v[batch][None, None, :], (batch, 1, d_model))
        return lat, cx

    zeros_gla = jnp.zeros((batch, gla_vw, gla_kw), F32)
    zeros_gdn = jnp.zeros((batch, GDN_H, GDN_DP, GDN_DP), F32)
    gla_bd_gain = jnp.tile(gla_norm_g, (1, GLA_H))

    for l in range(depth):
        with_ctx_out = l < depth - 1
        (sh1, csh1), (sc1, csc1), (g1, cg1), (sh2, csh2), (sc2, csc2), (g2, cg2) = [mod_vecs(l, i) for i in range(N_MOD)]

        wl = w_in[l]
        o_gla = 3 * hy_c
        o_gdn = o_gla + 2 * gla_kw + 2 * gla_vw + 2 * GLA_RANK
        o_gdn_gate = o_gdn + 3 * gdn_w
        o_gdn_small = o_gdn_gate + gdn_w
        w_small = jnp.concatenate([wl[:, o_gdn - 2 * GLA_RANK:o_gdn], wl[:, o_gdn_small:]], axis=1)
        w_small = jnp.pad(w_small, ((0, 0), (0, LANES - w_small.shape[1])))
        in_plain = [
            wl[:, o_gla:o_gla + 2 * gla_kw],
            wl[:, o_gla + 2 * gla_kw:o_gla + 2 * gla_kw + gla_vw],
            wl[:, o_gla + 2 * gla_kw + gla_vw:o_gla + 2 * gla_kw + 2 * gla_vw],
            _pad_heads(wl[:, o_gdn_gate:o_gdn_small], GDN_H, 1),
            w_small,
        ]
        in_plain = [w.astype(BF16) for w in in_plain]
        in_hy = wl[:, :o_gla].astype(BF16)
        in_gdn = _pad_heads(wl[:, o_gdn:o_gdn_gate], 3 * GDN_H, 1).astype(BF16)
        gdn_cw = _pad_heads(gdn_conv_w[l], 3 * GDN_H, 1)
        gdn_gain = _pad_heads(gdn_norm_g[l][None, :].repeat(GDN_H, 0).reshape(1, gdn_w), GDN_H, 1)
        wo = w_out[l]
        w_hy = wo[:hy_c].astype(BF16)
        w_gla = wo[hy_c:hy_c + gla_vw].astype(BF16)
        w_gdn = _pad_heads(wo[hy_c + gla_vw:], GDN_H, 0).astype(BF16)
        w1 = w_mlp1[l].astype(BF16)
        w2 = w_mlp2[l].astype(BF16)
        filt_args = (hy_f_w1[l], hy_f_b1[l], hy_f_w2[l], hy_f_b2[l], hy_f_w3[l], hy_sin_freq[l], hy_deltas)

        def mixer_parts(tokens, shift, scale, seg_len, s0_gla, s0_gdn):
            length = tokens.shape[1]
            tl = min(SCAN_ROWS, length)
            z_qk, z_v, z_gate, z_gdn_gate, z_small, u, q, k, v = _inproj(
                tokens, shift, scale, norm1_g[l], in_plain, in_hy, in_gdn, hy_conv_w[l], hy_conv_b[l], gdn_cw,
                seg_len, gdn_d ** -0.5, min(PROJ_ROWS, length))
            b_f, b_b, gc = _gates(z_small, gla_w_a2[l], gla_b_a[l], gdn_a_log[l], gdn_dt_bias[l], tl)
            gla_f, gla_b, gla_sf, gla_sb = _gla_scan(z_qk, z_v, b_f, b_b, s0_gla[0], s0_gla[1], tl)
            gdn_f, gdn_b, gdn_sf, gdn_sb = _gdn_scan(q, k, v, gc, s0_gdn[0], s0_gdn[1], tl)
            return u, (gla_f, gla_b, z_gate), (gdn_f, gdn_b, z_gdn_gate), (gla_sf, gla_sb), (gdn_sf, gdn_sb)

        def finish(tokens, gate1, hy, gla, gdn):
            return _outproj(tokens, gate1, hy, *gla, *gdn, gla_bd_gain[l][None, :], gdn_gain, w_hy, w_gla, w_gdn,
                            gla_dv, gdn_d, min(PROJ_ROWS, tokens.shape[1]))

        def mlp(tokens, shift, scale, gate2, final_norm):
            return _mlp(tokens, shift, scale, gate2, norm2_g[l], w1, w2, final_norm_g, final_norm,
                        min(MLP_ROWS, tokens.shape[1]), MLP_HIDDEN)

        u_c, gla_c, gdn_c, gla_s, gdn_s = mixer_parts(ctx, csh1, csc1, ctx_len,
                                                      (zeros_gla, zeros_gla), (zeros_gdn, zeros_gdn))
        u_l, gla_l, gdn_l, _, _ = mixer_parts(x, sh1, sc1, GRID_W, gla_s, gdn_s)
        h_re, h_im = _filter_spectrum(*_hy_filters(emb_lat, *filt_args, min(PROJ_ROWS, seq)), fft_tabs, n1, n2)
        nb = hy_c // LANES
        y1 = _fftconv_gate(u_l, 0, u_l, nb, h_re, h_im, 0, hy_d[l, 0], fft_tabs, n1, n2)
        hy_l = _fftconv_gate(y1, 0, u_l, 2 * nb, h_re, h_im, nb, hy_d[l, 1], fft_tabs, n1, n2)
        x = finish(x, g1, hy_l, gla_l, gdn_l)
        x = mlp(x, sh2, sc2, g2, not with_ctx_out)

        if with_ctx_out:
            coef = _dense_filter_spectrum(*_hy_filters(emb_ctx, *filt_args, ctx_len), dft_tabs)
            y1c = _dftconv_gate(u_c, 0, u_c, 1, *coef, 0, hy_d[l, 0], dft_tabs)
            hy_c_out = _dftconv_gate(y1c, 0, u_c, 2, *coef, 1, hy_d[l, 1], dft_tabs)
            ctx = finish(ctx, cg1, hy_c_out, gla_c, gdn_c)
            ctx = mlp(ctx, csh2, csc2, cg2, False)
    return x
```

```python
import functools
import math

import numpy as np
import jax
import jax.numpy as jnp
from jax import lax
from jax.experimental import pallas as pl
from jax.experimental.pallas import tpu as pltpu

F32 = jnp.float32
BF16 = jnp.bfloat16
HIGHEST = lax.Precision.HIGHEST

NORM_EPS = 1e-6
N_MOD = 6
GRID_W = 64
CHUNK = 64
LANES = 128

HY_ORDER = 2
HY_SHORT = 3
HY_EMB = 33
HY_EMB_PAD = 40
HY_DECAY_TARGET = 1e-2
HY_FAST_PCT = 0.3
HY_SLOW_PCT = 1.5

GLA_H = 4
GLA_RANK = 16
GLA_TAU = 16.0
GDN_H = 4
GDN_DP = LANES
PITCH_PAD = 8
FFT_UNROLL = 8

VMEM_LIMIT = 56 * 1024 * 1024

SCAN_ROWS = 4 * CHUNK
PROJ_ROWS = 512
MLP_ROWS = 1024
MLP_HIDDEN = 1024
GDN_BATCH_ROWS = 2


def _cparams(n_grid):
    return pltpu.CompilerParams(dimension_semantics=("arbitrary",) * n_grid, vmem_limit_bytes=VMEM_LIMIT)


def _bdot(a, b):
    return jnp.dot(a.astype(BF16), b.astype(BF16), preferred_element_type=F32)


def _bdot_nt(a, b):
    return lax.dot_general(a.astype(BF16), b.astype(BF16), (((1,), (1,)), ((), ())), preferred_element_type=F32)


def _bdot_tn(a, b):
    return lax.dot_general(a.astype(BF16), b.astype(BF16), (((0,), (0,)), ((), ())), preferred_element_type=F32)


def _const_spec(shape):
    return pl.BlockSpec(shape, lambda *_: (0,) * len(shape))


def _mod_kernel(c_ref, w_ref, b_ref, o_ref):
    c = c_ref[...]
    o_ref[0] = _bdot(c * jax.nn.sigmoid(c), w_ref[0]) + b_ref[0]


def _modulation(cc, w_mod, b_mod):
    depth, d, n = w_mod.shape
    tn = n // 4
    return pl.pallas_call(
        _mod_kernel,
        grid=(depth, n // tn),
        in_specs=[pl.BlockSpec(cc.shape, lambda l, j: (0, 0)),
                  pl.BlockSpec((1, d, tn), lambda l, j: (l, 0, j)),
                  pl.BlockSpec((1, 1, tn), lambda l, j: (l, 0, j))],
        out_specs=pl.BlockSpec((1, cc.shape[0], tn), lambda l, j: (l, 0, j)),
        out_shape=jax.ShapeDtypeStruct((depth, cc.shape[0], n), F32),
        compiler_params=_cparams(2),
        name="modulation",
    )(cc, w_mod.astype(BF16), b_mod.reshape(depth, 1, n))


def _rms_modulate(x, g, shift, scale):
    y = x * lax.rsqrt(jnp.mean(x * x, axis=-1, keepdims=True) + NORM_EPS) * g
    return y * (1.0 + scale) + shift


def _inproj_kernel(x_ref, sh_ref, sc_ref, g_ref, w_hy_ref, w_gdn_ref, hy_cw_ref, hy_cb_ref, gdn_cw_ref, *refs,
                   seg_len, q_scale):
    n_plain = (len(refs) - 4) // 2
    w_refs, plain_refs = refs[:n_plain], refs[n_plain:2 * n_plain]
    u_ref, q_ref, k_ref, v_ref = refs[2 * n_plain:]
    hb = _rms_modulate(x_ref[0], g_ref[...], sh_ref[0], sc_ref[0]).astype(BF16)
    piece = 2 * LANES
    width = q_ref.shape[-1]
    z_hy = [jnp.dot(hb, w_hy_ref[:, c0:c0 + piece], preferred_element_type=F32)
            for c0 in range(0, u_ref.shape[-1], piece)]
    z_gdn = [jnp.dot(hb, w_gdn_ref[:, c0:c0 + piece], preferred_element_type=F32)
             for c0 in range(0, 3 * width, piece)]
    for w_ref, o_ref in zip(w_refs, plain_refs):
        o_ref[0] = jnp.dot(hb, w_ref[...], preferred_element_type=F32)
    for i, c0 in enumerate(range(0, u_ref.shape[-1], piece)):
        cols = slice(c0, c0 + piece)
        u_ref[0, :, cols] = _seg_conv(z_hy[i], hy_cw_ref[:, cols], seg_len) + hy_cb_ref[:, cols]
    for part, (o_ref, norm_scale) in enumerate(((q_ref, q_scale), (k_ref, 1.0), (v_ref, None))):
        for c0 in range(0, width, piece):
            cols = slice(part * width + c0, part * width + c0 + piece)
            y = _seg_conv(z_gdn[(part * width + c0) // piece], gdn_cw_ref[:, cols], seg_len)
            y = y * jax.nn.sigmoid(y)
            for h0 in range(0, piece, GDN_DP):
                t = y[:, h0:h0 + GDN_DP]
                if norm_scale is not None:
                    t = t * (lax.rsqrt(jnp.sum(t * t, axis=-1, keepdims=True) + NORM_EPS) * norm_scale)
                o_ref[0, :, c0 + h0:c0 + h0 + GDN_DP] = t


def _inproj(x, shift, scale, g, plain_weights, w_hy, w_gdn, hy_cw, hy_cb, gdn_cw, seg_len, q_scale, tm):
    b, l, d = x.shape
    hyw, gdw = w_hy.shape[1], w_gdn.shape[1] // 3
    vec = pl.BlockSpec((1, 1, d), lambda i, j: (i, 0, 0))
    widths = [w.shape[1] for w in plain_weights] + [hyw, gdw, gdw, gdw]
    return pl.pallas_call(
        functools.partial(_inproj_kernel, seg_len=seg_len, q_scale=q_scale),
        grid=(b, l // tm),
        in_specs=[pl.BlockSpec((1, tm, d), lambda i, j: (i, j, 0)), vec, vec, _const_spec((1, d)),
                  _const_spec(w_hy.shape), _const_spec(w_gdn.shape), _const_spec(hy_cw.shape),
                  _const_spec((1, hyw)), _const_spec(gdn_cw.shape)]
        + [_const_spec(w.shape) for w in plain_weights],
        out_specs=[pl.BlockSpec((1, tm, w), lambda i, j: (i, j, 0)) for w in widths],
        out_shape=[jax.ShapeDtypeStruct((b, l, w), F32) for w in widths],
        compiler_params=_cparams(2),
        name="inproj",
    )(x, shift, scale, g.reshape(1, d), w_hy, w_gdn, hy_cw, hy_cb.reshape(1, hyw), gdn_cw, *plain_weights)


def _seg_conv(z, w, seg_len):
    rows, width = z.shape
    taps = w.shape[0]
    half = taps // 2
    pos = lax.broadcasted_iota(jnp.int32, (rows, LANES), 0) % seg_len
    valid = {d: (pos >= -d) if d < 0 else (pos < seg_len - d) for d in range(-half, half + 1) if d != 0}
    tiles = []
    for c0 in range(0, width, LANES):
        cols = slice(c0, c0 + LANES)
        zc = z[:, cols]
        acc = zc * w[half:half + 1, cols]
        for k in range(taps):
            d = k - half
            if d != 0:
                shifted = pltpu.roll(zc, (-d) % rows, 0)
                acc = acc + jnp.where(valid[d], shifted, 0.0) * w[k:k + 1, cols]
        tiles.append(acc)
    return jnp.concatenate(tiles, axis=1)


def _hy_filter_kernel(z_ref, w1_ref, b1_ref, w2_ref, b2_ref, w3_ref, f_ref, dl_ref, hf_ref, hb_ref):
    hy_c = dl_ref.shape[1]
    z = z_ref[...]
    fdot = lambda a, b: jnp.dot(a, b, precision=HIGHEST, preferred_element_type=F32)
    h = jnp.sin(f_ref[0:1, :] * (fdot(z, w1_ref[...]) + b1_ref[...]))
    h = jnp.sin(f_ref[1:2, :] * (fdot(h, w2_ref[...]) + b2_ref[...]))
    h = fdot(h, w3_ref[...])
    window = jnp.exp(-z[:, 0:1] * dl_ref[...])
    row = lax.broadcasted_iota(jnp.int32, window.shape, 0) + pl.program_id(0) * z.shape[0]
    for o in range(HY_ORDER):
        base = 2 * o * hy_c
        hf_ref[:, o * hy_c:(o + 1) * hy_c] = h[:, base:base + hy_c] * window
        hb_ref[:, o * hy_c:(o + 1) * hy_c] = jnp.where(row > 0, h[:, base + hy_c:base + 2 * hy_c] * window, 0.0)


def _hy_filters(z, w1, b1, w2, b2, w3, sin_freq, deltas, tl):
    l, emb = z.shape
    fh = w2.shape[0]
    hy_c = deltas.shape[1]
    out = jax.ShapeDtypeStruct((l, HY_ORDER * hy_c), F32)
    rows = lambda w: pl.BlockSpec((tl, w), lambda i: (i, 0))
    return pl.pallas_call(
        _hy_filter_kernel,
        grid=(l // tl,),
        in_specs=[rows(emb), _const_spec((emb, fh)), _const_spec((1, fh)), _const_spec((fh, fh)), _const_spec((1, fh)),
                  _const_spec(w3.shape), _const_spec((2, fh)), _const_spec((1, hy_c))],
        out_specs=[rows(HY_ORDER * hy_c)] * 2,
        out_shape=[out, out],
        compiler_params=_cparams(1),
        name="hy_filters",
    )(z, jnp.pad(w1, ((0, emb - w1.shape[0]), (0, 0))), b1.reshape(1, fh), w2, b2.reshape(1, fh), w3, sin_freq, deltas)


def _hy_embedding(length, emb):
    t = jnp.linspace(0.0, 1.0, length, dtype=F32)[:, None]
    bands = (HY_EMB - 1) // 2
    f = jnp.linspace(1e-4, bands - 1, bands, dtype=F32)
    w = 2 * math.pi * jnp.arange(length, dtype=F32) / length
    ang = w[:, None] * f[None, :]
    z = jnp.concatenate([t, jnp.cos(ang), -jnp.sin(ang)], axis=-1)
    return jnp.pad(z, ((0, 0), (0, emb - z.shape[1])))


def _fft_tables(n1, n2):
    n = n1 * n2
    nk = n1 // 2 + 1
    kp = -(-nk // 8) * 8
    s1 = np.arange(n1 // 2)[None, :, None]
    s2 = np.arange(n2)[:, None, None]
    k1 = np.arange(kp)[None, None, :]
    theta = 2.0 * np.pi * ((k1 * (n2 * s1 + s2)) % n) / n
    tt_fwd = np.concatenate([np.cos(theta), -np.sin(theta)], axis=-1)
    weight = np.where(np.arange(kp) < nk, 2.0, 0.0)
    weight[0] = weight[nk - 1] = 1.0
    tt_inv = np.concatenate([np.cos(theta) * weight, -np.sin(theta) * weight], axis=-1)
    phi = 2.0 * np.pi * ((np.arange(n2)[:, None] * np.arange(n2)[None, :]) % n2) / n2
    c2, s2m = np.cos(phi), np.sin(phi)
    fwd = np.block([[c2, s2m], [-s2m, c2]])
    inv = np.block([[c2, -s2m], [s2m, c2]])
    return tuple(jnp.asarray(t, F32) for t in (tt_fwd, tt_inv, fwd, inv))


def _for_k1(body, nk):
    lax.fori_loop(0, nk - 1, lambda k1, c: body(k1) or c, 0, unroll=min(FFT_UNROLL, nk - 1))
    body(nk - 1)


def _time_pitch(n2):
    return n2 + PITCH_PAD


def _spec_pitch(n2):
    return 2 * n2 + PITCH_PAD


def _dft_load_time(load_block, time_ref, n1, n2):
    def copy(s1, carry):
        time_ref[pl.ds(pl.multiple_of(s1 * _time_pitch(n2), 8), n2), :] = load_block(
            pl.ds(pl.multiple_of(s1 * n2, n2), n2))
        return carry

    lax.fori_loop(0, n1 // 2, copy, 0, unroll=min(FFT_UNROLL, n1 // 2))


def _dft_stage_a(time_ref, tt_ref, spec_ref, n1, n2):
    kp = tt_ref.shape[-1] // 2

    def stage_a(s2, carry):
        slab = time_ref[pl.ds(s2, n1 // 2, stride=_time_pitch(n2)), :]
        res = _bdot_tn(tt_ref[s2], slab)
        spec_ref[pl.ds(s2, kp, stride=_spec_pitch(n2)), :] = res[:kp]
        spec_ref[pl.ds(n2 + s2, kp, stride=_spec_pitch(n2)), :] = res[kp:]
        return carry

    lax.fori_loop(0, n2, stage_a, 0, unroll=2 * FFT_UNROLL)


def _aligned_rows(start, size, align):
    return pl.ds(start if isinstance(start, int) else pl.multiple_of(start, align), size)


def _spec_rows(k1, n2):
    return _aligned_rows(k1 * _spec_pitch(n2), 2 * n2, 8)


def _spectrum_kernel(hf_ref, hb_ref, tt_ref, fwd_ref, hr_ref, hi_ref, time_ref, spec_ref, *, n1, n2):
    inv_n = 1.0 / (n1 * n2)
    fwd = fwd_ref[...].astype(BF16)
    for sign, src_ref in ((1.0, hf_ref), (-1.0, hb_ref)):
        _dft_load_time(lambda rows: src_ref[rows, :], time_ref, n1, n2)
        _dft_stage_a(time_ref, tt_ref, spec_ref, n1, n2)

        def stage_b(k1):
            rows = _spec_rows(k1, n2)
            spec = jnp.dot(fwd, spec_ref[rows, :].astype(BF16), preferred_element_type=F32) * inv_n
            hrows = _aligned_rows(k1 * n2, n2, n2)
            if sign > 0:
                hr_ref[hrows, :] = spec[:n2]
                hi_ref[hrows, :] = spec[n2:]
            else:
                hr_ref[hrows, :] += spec[:n2]
                hi_ref[hrows, :] -= spec[n2:]

        _for_k1(stage_b, n1 // 2 + 1)


def _filter_spectrum(hf, hb, tables, n1, n2):
    tt, _, fwd, _ = tables
    l, c = hf.shape
    n = (n1 // 2 + 1) * n2
    chan = lambda rows: pl.BlockSpec((rows, LANES), lambda j: (0, j))
    return pl.pallas_call(
        functools.partial(_spectrum_kernel, n1=n1, n2=n2),
        grid=(c // LANES,),
        in_specs=[chan(l), chan(l), _const_spec(tt.shape), _const_spec(fwd.shape)],
        out_specs=[chan(n), chan(n)],
        out_shape=[jax.ShapeDtypeStruct((n, c), F32)] * 2,
        scratch_shapes=_fft_scratch(tt, n1, n2),
        compiler_params=_cparams(1),
        name="hy_spectrum",
    )(hf, hb, tt, fwd)


def _fft_scratch(tt, n1, n2):
    kp = tt.shape[-1] // 2
    return [pltpu.VMEM((n1 // 2 * _time_pitch(n2), LANES), F32), pltpu.VMEM((kp * _spec_pitch(n2), LANES), F32)]


def _fftconv_kernel(u_ref, gate_ref, hr_ref, hi_ref, d_ref, ttf_ref, tti_ref, fwd_ref, inv_ref, o_ref,
                    time_ref, spec_ref, *, n1, n2):
    n1h = n1 // 2
    kp = ttf_ref.shape[-1] // 2
    fwd = fwd_ref[...].astype(BF16)
    inv = inv_ref[...].astype(BF16)
    _dft_load_time(lambda rows: u_ref[0, rows, :], time_ref, n1, n2)
    _dft_stage_a(time_ref, ttf_ref, spec_ref, n1, n2)

    def stage_b(k1s):
        rows = [_spec_rows(k1, n2) for k1 in k1s]
        slab = jnp.concatenate([spec_ref[r, :].astype(BF16) for r in rows], axis=1)
        spec = jnp.dot(fwd, slab, preferred_element_type=F32)
        br, bi = spec[:n2], spec[n2:]
        hrows = [_aligned_rows(k1 * n2, n2, n2) for k1 in k1s]
        hr = jnp.concatenate([hr_ref[r, :] for r in hrows], axis=1)
        hi = jnp.concatenate([hi_ref[r, :] for r in hrows], axis=1)
        y = jnp.concatenate([br * hr - bi * hi, br * hi + bi * hr], axis=0)
        g = jnp.dot(inv, y.astype(BF16), preferred_element_type=F32)
        for i, r in enumerate(rows):
            spec_ref[r, :] = g[:, i * LANES:(i + 1) * LANES]

    lax.fori_loop(0, n1h // 2, lambda kk, c: stage_b((2 * kk, 2 * kk + 1)) or c, 0,
                  unroll=min(FFT_UNROLL // 2, n1h // 2))
    stage_b((n1h,))

    def stage_a_inv(s2, carry):
        g = jnp.concatenate([spec_ref[pl.ds(s2, kp, stride=_spec_pitch(n2)), :],
                             spec_ref[pl.ds(n2 + s2, kp, stride=_spec_pitch(n2)), :]], axis=0)
        time_ref[pl.ds(s2, n1h, stride=_time_pitch(n2)), :] = _bdot(tti_ref[s2], g)
        return carry

    lax.fori_loop(0, n2, stage_a_inv, 0, unroll=2 * FFT_UNROLL)

    def gate_rows(s1, carry):
        rows = pl.ds(pl.multiple_of(s1 * n2, n2), n2)
        conv = time_ref[pl.ds(pl.multiple_of(s1 * _time_pitch(n2), 8), n2), :]
        o_ref[0, rows, :] = gate_ref[0, rows, :] * (conv + d_ref[...] * u_ref[0, rows, :])
        return carry

    lax.fori_loop(0, n1h, gate_rows, 0, unroll=min(FFT_UNROLL, n1h))


def _fftconv_gate(u_arr, u_blk, gate_arr, gate_blk, hr, hi, h_blk, d, tables, n1, n2):
    tt_fwd, tt_inv, fwd, inv = tables
    b, l, _ = u_arr.shape
    n = hr.shape[0]
    c = d.shape[0]
    ncb = c // LANES
    seq = lambda off: pl.BlockSpec((1, l, LANES), lambda j, i: (i, 0, off + j))
    chan = lambda rows, off=0: pl.BlockSpec((rows, LANES), lambda j, i: (0, off + j))
    return pl.pallas_call(
        functools.partial(_fftconv_kernel, n1=n1, n2=n2),
        grid=(ncb, b),
        in_specs=[seq(u_blk), seq(gate_blk), chan(n, h_blk), chan(n, h_blk), chan(1),
                  _const_spec(tt_fwd.shape), _const_spec(tt_inv.shape), _const_spec(fwd.shape),
                  _const_spec(inv.shape)],
        out_specs=pl.BlockSpec((1, l, LANES), lambda j, i: (i, 0, j)),
        out_shape=jax.ShapeDtypeStruct((b, l, c), F32),
        scratch_shapes=_fft_scratch(tt_fwd, n1, n2),
        compiler_params=_cparams(2),
        name="hy_fftconv",
    )(u_arr, gate_arr, hr, hi, d.reshape(1, c), tt_fwd, tt_inv, fwd, inv)


def _dft_tables(l):
    n = 2 * l
    ang = 2.0 * np.pi * ((np.arange(l)[:, None] * np.arange(l)[None, :]) % n) / n
    cf = np.cos(ang)
    sf = -np.sin(ang)
    sf[0, :] = (-1.0) ** np.arange(l)
    fwd = np.concatenate([cf, sf], axis=0)
    return jnp.asarray(fwd, F32), jnp.asarray(fwd.T, F32)


def _dftconv_kernel(u_ref, gate_ref, p_ref, q_ref, r_ref, d_ref, fwd_ref, inv_ref, o_ref):
    u = u_ref[0]
    l = u.shape[0]
    spec = _bdot(fwd_ref[...], u)
    xr, xi = spec[:l], spec[l:]
    q = q_ref[...]
    y = jnp.concatenate([xr * p_ref[...] - xi * q, xr * q + xi * r_ref[...]], axis=0)
    conv = _bdot(inv_ref[...], y)
    o_ref[0] = gate_ref[0] * (conv + d_ref[...] * u)


def _dense_spectrum_kernel(hf_ref, hb_ref, fwd_ref, p_ref, q_ref, r_ref):
    l = hf_ref.shape[0]
    xf = _bdot(fwd_ref[...], hf_ref[...])
    xb = _bdot(fwd_ref[...], hb_ref[...])
    first = lax.broadcasted_iota(jnp.int32, (l, hf_ref.shape[1]), 0) == 0
    scale = jnp.where(first, 0.5 / l, 1.0 / l)
    re = (xf[:l] + xb[:l]) * scale
    p_ref[...] = re
    q_ref[...] = jnp.where(first, 0.0, (xf[l:] - xb[l:]) * scale)
    r_ref[...] = jnp.where(first, (xf[l:] + xb[l:]) * scale, re)


def _dense_filter_spectrum(hf, hb, tables):
    fwd, _ = tables
    out = jax.ShapeDtypeStruct(hf.shape, F32)
    return pl.pallas_call(
        _dense_spectrum_kernel,
        grid=(1,),
        in_specs=[_const_spec(hf.shape), _const_spec(hb.shape), _const_spec(fwd.shape)],
        out_specs=[_const_spec(hf.shape)] * 3,
        out_shape=[out] * 3,
        compiler_params=_cparams(1),
        name="hy_dense_spectrum",
    )(hf, hb, fwd)


def _dftconv_gate(u_arr, u_blk, gate_arr, gate_blk, p, q, r, h_blk, d, tables):
    fwd, inv = tables
    b, l, _ = u_arr.shape
    c = d.shape[0]
    seq = lambda off: pl.BlockSpec((1, l, c), lambda i: (i, 0, off))
    coef = pl.BlockSpec((l, c), lambda i: (0, h_blk))
    return pl.pallas_call(
        _dftconv_kernel,
        grid=(b,),
        in_specs=[seq(u_blk), seq(gate_blk), coef, coef, coef,
                  _const_spec((1, c)), _const_spec(fwd.shape), _const_spec(inv.shape)],
        out_specs=pl.BlockSpec((1, l, c), lambda i: (i, 0, 0)),
        out_shape=jax.ShapeDtypeStruct((b, l, c), F32),
        compiler_params=_cparams(1),
        name="hy_dftconv",
    )(u_arr, gate_arr, p, q, r, d.reshape(1, c), fwd, inv)


GDN_GF, GDN_GB, GDN_BF, GDN_BB = 32, 36, 40, 44

def _chunk_cumsum_mats(tl):
    r = np.arange(tl)
    same = (r[:, None] // CHUNK) == (r[None, :] // CHUNK)
    lower = same & (r[None, :] <= r[:, None])
    upper = same & (r[None, :] >= r[:, None])
    return jnp.asarray(lower, F32), jnp.asarray(upper, F32)


def _tri_cumsum(tri, x):
    hi = x.astype(BF16)
    rest = x - hi.astype(F32)
    mid = rest.astype(BF16)
    lo = (rest - mid.astype(F32)).astype(BF16)
    dot = lambda part: jnp.dot(tri, part, preferred_element_type=F32)
    return dot(hi) + dot(mid) + dot(lo)


def _gates_kernel(small_ref, w_ref, b_ref, nea_ref, dtb_ref, lo_ref, up_ref, bf_ref, bb_ref, gc_ref):
    small = small_ref[0]
    kw = bf_ref.shape[-1]
    log_decay = nea_ref[...] * jax.nn.softplus(small + dtb_ref[...])
    cums = []
    for d, tri_ref in enumerate((lo_ref, up_ref)):
        pre = jnp.dot(small[:, d * GLA_RANK:(d + 1) * GLA_RANK], w_ref[d], precision=HIGHEST,
                      preferred_element_type=F32) + b_ref[d]
        vals = jnp.concatenate([jax.nn.log_sigmoid(pre) * (1.0 / GLA_TAU), log_decay], axis=1)
        cums.append(_tri_cumsum(tri_ref[...].astype(BF16), vals))
    bf_ref[0] = cums[0][:, :kw]
    bb_ref[0] = cums[1][:, :kw]
    lane = lax.broadcasted_iota(jnp.int32, small.shape, 1)
    gc_ref[0] = jnp.where(lane < GDN_GB, cums[0][:, kw:], jnp.where(lane < GDN_BF, cums[1][:, kw:],
                                                                  jax.nn.sigmoid(small)))


def _gates(small, w_a2, b_a, a_log, dt_bias, tl):
    b, l, sw = small.shape
    kw = w_a2.shape[-1]
    nea = jnp.zeros((1, LANES), F32).at[0, GDN_GF:GDN_BF].set(-jnp.exp(a_log.reshape(-1)))
    dtb = jnp.zeros((1, LANES), F32).at[0, GDN_GF:GDN_BF].set(dt_bias.reshape(-1))
    lower, upper = _chunk_cumsum_mats(tl)
    tok = lambda w: pl.BlockSpec((1, tl, w), lambda i, j: (i, j, 0))
    return pl.pallas_call(
        _gates_kernel,
        grid=(b, l // tl),
        in_specs=[tok(sw), _const_spec(w_a2.shape), _const_spec((2, 1, kw)), _const_spec((1, LANES)),
                  _const_spec((1, LANES)), _const_spec((tl, tl)), _const_spec((tl, tl))],
        out_specs=[tok(kw), tok(kw), tok(LANES)],
        out_shape=[jax.ShapeDtypeStruct((b, l, kw), F32)] * 2 + [jax.ShapeDtypeStruct((b, l, LANES), F32)],
        compiler_params=_cparams(2),
        name="scan_gates",
    )(small, w_a2, b_a.reshape(2, 1, kw), nea, dtb, lower, upper)


def _gla_scan_kernel(qkf_ref, vf_ref, bf_ref, qkb_ref, vb_ref, bb_ref, s0f_ref, s0b_ref, hm_ref, vm_ref, bd_ref,
                     of_ref, ob_ref, sff_ref, sfb_ref, stf_ref, stb_ref, *, n_chunks, kw, q_scale):
    j = pl.program_id(1)

    @pl.when(j == 0)
    def _():
        stf_ref[...] = s0f_ref[0]
        stb_ref[...] = s0b_ref[0]

    ri = lax.broadcasted_iota(jnp.int32, (CHUNK, GLA_H * CHUNK), 0)
    ci = lax.broadcasted_iota(jnp.int32, (CHUNK, GLA_H * CHUNK), 1) % CHUNK
    hm = hm_ref[...]
    vm = vm_ref[...]
    bd = bd_ref[...]
    dirs = ((qkf_ref, vf_ref, bf_ref, of_ref, stf_ref, ri >= ci, CHUNK - 1, CHUNK // 2 - 1, False),
            (qkb_ref, vb_ref, bb_ref, ob_ref, stb_ref, ci >= ri, 0, CHUNK // 2, True))

    qts, kstacks, vstacks, keeps, qes, vs, kds, decs, sinks = [], [], [], [], [], [], [], [], []
    for qk_ref, v_ref, b_ref, o_ref, st_ref, keep, last, mid, rev in dirs:
        for c in range(n_chunks):
            rows = slice((n_chunks - 1 - c if rev else c) * CHUNK, (n_chunks - c if rev else c + 1) * CHUNK)
            qk = qk_ref[0, rows, :]
            q, k = qk[:, :kw] * q_scale, qk[:, kw:]
            v = v_ref[0, rows, :]
            bc = b_ref[0, rows, :]
            b_mid = bc[mid:mid + 1, :]
            b_last = bc[last:last + 1, :]
            kt = k * jnp.exp(b_mid - bc)
            qts.append(q * jnp.exp(bc - b_mid))
            kstacks.append(jnp.concatenate([kt * hm[h:h + 1, :] for h in range(GLA_H)], axis=0).astype(BF16))
            vstacks.append(jnp.concatenate([v * vm[h:h + 1, :] for h in range(GLA_H)], axis=0).astype(BF16))
            keeps.append(keep)
            qes.append((q * jnp.exp(bc)).astype(BF16))
            vs.append(v)
            kds.append(k * jnp.exp(b_last - bc))
            decs.append(jnp.exp(b_last))
            sinks.append((o_ref, rows))
    attns = [jnp.where(keep, _bdot_nt(qt, ks), 0.0) for keep, qt, ks in zip(keeps, qts, kstacks)]
    o_intra = [_bdot(a, vst) for a, vst in zip(attns, vstacks)]
    upds = [_bdot_tn(v, kd) * bd for v, kd in zip(vs, kds)]

    for d, (_, _, _, _, st_ref, _, _, _, _) in enumerate(dirs):
        st = st_ref[...]
        for c in range(n_chunks):
            i = d * n_chunks + c
            o_ref, rows = sinks[i]
            o_ref[0, rows, :] = o_intra[i] + _bdot_nt(qes[i], st)
            st = st * decs[i] + upds[i]
        st_ref[...] = st

    @pl.when(j == pl.num_programs(1) - 1)
    def _():
        sff_ref[0] = stf_ref[...]
        sfb_ref[0] = stb_ref[...]


def _gla_scan(qk, v, b_f, b_b, s0_f, s0_b, tl):
    b, l, kw2 = qk.shape
    kw, vw = kw2 // 2, v.shape[-1]
    dk, dv = kw // GLA_H, vw // GLA_H
    nblk = l // tl
    heads_k = np.arange(kw) // dk
    heads_v = np.arange(vw) // dv
    hm = jnp.asarray(np.arange(8)[:, None] == heads_k[None, :], F32)
    vm = jnp.asarray(np.arange(8)[:, None] == heads_v[None, :], F32)
    bd = jnp.asarray(heads_v[:, None] == heads_k[None, :], F32)
    fwd = lambda w: pl.BlockSpec((1, tl, w), lambda i, j: (i, j, 0))
    bwd = lambda w: pl.BlockSpec((1, tl, w), lambda i, j: (i, nblk - 1 - j, 0))
    state = pl.BlockSpec((1, vw, kw), lambda i, j: (i, 0, 0))
    return pl.pallas_call(
        functools.partial(_gla_scan_kernel, n_chunks=tl // CHUNK, kw=kw, q_scale=dk ** -0.5),
        grid=(b, nblk),
        in_specs=[fwd(kw2), fwd(vw), fwd(kw), bwd(kw2), bwd(vw), bwd(kw), state, state,
                  _const_spec(hm.shape), _const_spec(vm.shape), _const_spec(bd.shape)],
        out_specs=[fwd(vw), bwd(vw), state, state],
        out_shape=[jax.ShapeDtypeStruct((b, l, vw), F32)] * 2 + [jax.ShapeDtypeStruct((b, vw, kw), F32)] * 2,
        scratch_shapes=[pltpu.VMEM((vw, kw), F32)] * 2,
        compiler_params=_cparams(2),
        name="gla_scan",
    )(qk, v, b_f, qk, v, b_b, s0_f, s0_b, hm, vm, bd)


def _unit_tri_inverses(mats, eye, m16, m32, m64):
    diag = [a * m16 for a in mats]
    inv = [eye + a for a in diag]
    pw = [a.astype(BF16) for a in diag]
    for _ in range(3):
        pw = [_bdot(p, p).astype(BF16) for p in pw]
        inv = [t + _bdot(t, p) for t, p in zip(inv, pw)]
    for mask in (m32, m64):
        inv_b = [t.astype(BF16) for t in inv]
        mid = [_bdot(t, a * mask) for t, a in zip(inv_b, mats)]
        inv = [t + _bdot(m, tb) for t, m, tb in zip(inv, mid, inv_b)]
    return inv


def _gdn_scan_kernel(qf_ref, kf_ref, vf_ref, gcf_ref, qb_ref, kb_ref, vb_ref, gcb_ref,
                     s0f_ref, s0b_ref, of_ref, ob_ref, sff_ref, sfb_ref, sf_ref, sb_ref, *, tl):
    j = pl.program_id(1)

    @pl.when(j == 0)
    def _():
        sf_ref[...] = s0f_ref[...]
        sb_ref[...] = s0b_ref[...]

    for bi in range(qf_ref.shape[0]):
        _gdn_block(bi, qf_ref, kf_ref, vf_ref, gcf_ref, qb_ref, kb_ref, vb_ref, gcb_ref, of_ref, ob_ref,
                   sf_ref, sb_ref, tl)

    @pl.when(j == pl.num_programs(1) - 1)
    def _():
        sff_ref[...] = sf_ref[...]
        sfb_ref[...] = sb_ref[...]


def _gdn_block(bi, qf_ref, kf_ref, vf_ref, gcf_ref, qb_ref, kb_ref, vb_ref, gcb_ref, of_ref, ob_ref,
               sf_ref, sb_ref, tl):
    nck = tl // CHUNK
    ri = lax.broadcasted_iota(jnp.int32, (tl, tl), 0)
    ci = lax.broadcasted_iota(jnp.int32, (tl, tl), 1)
    same = lambda n: (ri // n) == (ci // n)
    chunk = same(CHUNK)
    eye = (ri == ci).astype(F32)
    m16 = same(16).astype(F32)
    m32 = (same(32) & ~same(16)).astype(F32)
    m64 = (chunk & ~same(32)).astype(F32)
    dirs = ((qf_ref, kf_ref, vf_ref, gcf_ref, of_ref, sf_ref, ri > ci, ri >= ci, CHUNK - 1, GDN_GF, GDN_BF, False),
            (qb_ref, kb_ref, vb_ref, gcb_ref, ob_ref, sb_ref, ci > ri, ci >= ri, 0, GDN_GB, GDN_BB, True))

    qs, ks, vs, gcums, betas, stricts, incls, glasts, outs = [], [], [], [], [], [], [], [], []
    for q_ref, k_ref, v_ref, gc_ref, o_ref, s_ref, strict, incl, last, g_col, b_col, rev in dirs:
        gc = gc_ref[bi]
        gct = gc.T
        for h in range(GDN_H):
            cols = slice(h * GDN_DP, (h + 1) * GDN_DP)
            gcum = gc[:, g_col + h:g_col + h + 1]
            decay = jnp.exp(jnp.minimum(gcum - gct[g_col + h:g_col + h + 1, :], 0.0))
            qs.append(q_ref[bi, :, cols])
            ks.append(k_ref[bi, :, cols])
            vs.append(v_ref[bi, :, cols])
            gcums.append(gcum)
            betas.append(gc[:, b_col + h:b_col + h + 1])
            stricts.append(jnp.where(chunk & strict, decay, 0.0))
            incls.append(jnp.where(chunk & incl, decay, 0.0))
            glasts.append(jnp.concatenate(
                [jnp.broadcast_to(gcum[c * CHUNK + last:c * CHUNK + last + 1, :], (CHUNK, 1)) for c in range(nck)],
                axis=0))
            outs.append((o_ref, s_ref, h, cols, rev))
    n_pairs = len(qs)
    e_cols = [jnp.exp(g) for g in gcums]
    kbs = [k * b for k, b in zip(ks, betas)]
    mats = [-_bdot_nt(kb, k) * d for kb, k, d in zip(kbs, ks, stricts)]
    invs = _unit_tri_inverses(mats, eye, m16, m32, m64)
    wus = [_bdot(t, jnp.concatenate([kb * e, v * b], axis=1))
           for t, kb, e, v, b in zip(invs, kbs, e_cols, vs, betas)]
    attns = [_bdot_nt(q, k) * d for q, k, d in zip(qs, ks, incls)]
    awus = [_bdot(a, wu) for a, wu in zip(attns, wus)]
    q_effs = [(q * e - awu[:, :GDN_DP]).astype(BF16) for q, e, awu in zip(qs, e_cols, awus)]
    ws = [wu[:, :GDN_DP].astype(BF16) for wu in wus]
    us = [wu[:, GDN_DP:] for wu in wus]
    o_intra = [awu[:, GDN_DP:] for awu in awus]
    kds = [(k * jnp.exp(gl - g)).astype(BF16) for k, gl, g in zip(ks, glasts, gcums)]
    decs = [jnp.exp(gl) for gl in glasts]

    states = [s_ref[bi, h] for (_, s_ref, h, _, _) in outs]
    for c in range(nck):
        rows = [slice((nck - 1 - c if rev else c) * CHUNK, (nck - c if rev else c + 1) * CHUNK)
                for (_, _, _, _, rev) in outs]
        prods = [jnp.dot(jnp.concatenate([q_effs[i][rows[i]], ws[i][rows[i]]], axis=0), states[i].astype(BF16),
                         preferred_element_type=F32) for i in range(n_pairs)]
        for i, (o_ref, _, _, cols, _) in enumerate(outs):
            o_ref[bi, rows[i], cols] = o_intra[i][rows[i]] + prods[i][:CHUNK]
        v_news = [us[i][rows[i]] - prods[i][CHUNK:] for i in range(n_pairs)]
        states = [decs[i][rows[i]][0:1] * states[i] + _bdot_tn(kds[i][rows[i]], v_news[i]) for i in range(n_pairs)]
    for i, (_, s_ref, h, _, _) in enumerate(outs):
        s_ref[bi, h] = states[i]


def _gdn_scan(q, k, v, gc, s0_f, s0_b, tl):
    b, l, width = q.shape
    nblk = l // tl
    nb = GDN_BATCH_ROWS if b % GDN_BATCH_ROWS == 0 else 1
    fwd = lambda w: pl.BlockSpec((nb, tl, w), lambda i, j: (i, j, 0))
    bwd = lambda w: pl.BlockSpec((nb, tl, w), lambda i, j: (i, nblk - 1 - j, 0))
    state = pl.BlockSpec((nb, GDN_H, GDN_DP, GDN_DP), lambda i, j: (i, 0, 0, 0))
    st_shape = jax.ShapeDtypeStruct((b, GDN_H, GDN_DP, GDN_DP), F32)
    return pl.pallas_call(
        functools.partial(_gdn_scan_kernel, tl=tl),
        grid=(b // nb, nblk),
        in_specs=[fwd(width), fwd(width), fwd(width), fwd(LANES),
                  bwd(width), bwd(width), bwd(width), bwd(LANES), state, state],
        out_specs=[fwd(width), bwd(width), state, state],
        out_shape=[jax.ShapeDtypeStruct((b, l, width), F32)] * 2 + [st_shape] * 2,
        scratch_shapes=[pltpu.VMEM((nb, GDN_H, GDN_DP, GDN_DP), F32)] * 2,
        compiler_params=_cparams(2),
        name="gdn_scan",
    )(q, k, v, gc, q, k, v, gc, s0_f, s0_b)


def _gated_head_norm(o, gate, g_norm, ones_bd, inv_d):
    sq = o * o
    hi = sq.astype(BF16)
    lo = (sq - hi.astype(F32)).astype(BF16)
    ms = (jnp.dot(hi, ones_bd, preferred_element_type=F32) + jnp.dot(lo, ones_bd, preferred_element_type=F32)) * inv_d
    return o * lax.rsqrt(ms + NORM_EPS) * g_norm * (gate * jax.nn.sigmoid(gate))


def _outproj_kernel(x_ref, g1_ref, hy_ref, glf_ref, glb_ref, glg_ref, gdf_ref, gdb_ref, gdg_ref,
                    gln_ref, gdn_ref, glm_ref, gdm_ref, why_ref, wgl_ref, wgd_ref, o_ref, *, gla_dv, gdn_d):
    y_gla = _gated_head_norm(glf_ref[0] + glb_ref[0], glg_ref[0], gln_ref[...], glm_ref[...], 1.0 / gla_dv)
    y_gdn = _gated_head_norm(gdf_ref[0] + gdb_ref[0], gdg_ref[0], gdn_ref[...], gdm_ref[...], 1.0 / gdn_d)
    acc = _bdot(hy_ref[0], why_ref[...]) + _bdot(y_gla, wgl_ref[...]) + _bdot(y_gdn, wgd_ref[...])
    o_ref[0] = x_ref[0] + g1_ref[0] * acc


def _outproj(x, g1, hy, gla_f, gla_b, gla_gate, gdn_f, gdn_b, gdn_gate, gla_g, gdn_g, w_hy, w_gla, w_gdn,
             gla_dv, gdn_d, tm):
    b, l, d = x.shape
    hyw, glw, gdw = hy.shape[-1], gla_f.shape[-1], gdn_f.shape[-1]
    gl_heads = np.arange(glw) // gla_dv
    gd_heads = np.arange(gdw) // GDN_DP
    gl_m = jnp.asarray(gl_heads[:, None] == gl_heads[None, :], BF16)
    gd_m = jnp.asarray(gd_heads[:, None] == gd_heads[None, :], BF16)
    tok = lambda w: pl.BlockSpec((1, tm, w), lambda i, j: (i, j, 0))
    return pl.pallas_call(
        functools.partial(_outproj_kernel, gla_dv=gla_dv, gdn_d=gdn_d),
        grid=(b, l // tm),
        in_specs=[tok(d), pl.BlockSpec((1, 1, d), lambda i, j: (i, 0, 0)), tok(hyw), tok(glw), tok(glw), tok(glw),
                  tok(gdw), tok(gdw), tok(gdw), _const_spec((1, glw)), _const_spec((1, gdw)),
                  _const_spec(gl_m.shape), _const_spec(gd_m.shape),
                  _const_spec(w_hy.shape), _const_spec(w_gla.shape), _const_spec(w_gdn.shape)],
        out_specs=tok(d),
        out_shape=jax.ShapeDtypeStruct(x.shape, F32),
        compiler_params=_cparams(2),
        name="outproj",
    )(x, g1, hy, gla_f, gla_b, gla_gate, gdn_f, gdn_b, gdn_gate, gla_g, gdn_g, gl_m, gd_m, w_hy, w_gla, w_gdn)


def _mlp_kernel(x_ref, sh_ref, sc_ref, g2_ref, ng_ref, w1_ref, w2_ref, fg_ref, o_ref, hn_ref, acc_ref, *, final_norm):
    kk = pl.program_id(2)

    @pl.when(kk == 0)
    def _():
        hn_ref[...] = _rms_modulate(x_ref[0], ng_ref[...], sh_ref[0], sc_ref[0]).astype(BF16)
        acc_ref[...] = jnp.zeros_like(acc_ref)

    hid = jnp.maximum(jnp.dot(hn_ref[...], w1_ref[...], preferred_element_type=F32), 0.0)
    acc_ref[...] += _bdot(hid * hid, w2_ref[...])

    @pl.when(kk == pl.num_programs(2) - 1)
    def _():
        y = x_ref[0] + g2_ref[0] * acc_ref[...]
        if final_norm:
            y = y * lax.rsqrt(jnp.mean(y * y, axis=-1, keepdims=True) + NORM_EPS) * fg_ref[...]
        o_ref[0] = y


def _mlp(x, shift, scale, gate, norm_g, w1, w2, final_g, final_norm, tm, th):
    b, l, d = x.shape
    dff = w1.shape[1]
    vec = pl.BlockSpec((1, 1, d), lambda i, j, k: (i, 0, 0))
    row = pl.BlockSpec((1, d), lambda i, j, k: (0, 0))
    return pl.pallas_call(
        functools.partial(_mlp_kernel, final_norm=final_norm),
        grid=(b, l // tm, dff // th),
        in_specs=[pl.BlockSpec((1, tm, d), lambda i, j, k: (i, j, 0)), vec, vec, vec, row,
                  pl.BlockSpec((d, th), lambda i, j, k: (0, k)), pl.BlockSpec((th, d), lambda i, j, k: (k, 0)), row],
        out_specs=pl.BlockSpec((1, tm, d), lambda i, j, k: (i, j, 0)),
        out_shape=jax.ShapeDtypeStruct(x.shape, F32),
        scratch_shapes=[pltpu.VMEM((tm, d), BF16), pltpu.VMEM((tm, d), F32)],
        compiler_params=_cparams(3),
        name="mlp",
    )(x, shift, scale, gate, norm_g.reshape(1, d), w1, w2, final_g.reshape(1, d))


def _pad_heads(w, n_heads, axis):
    shape = w.shape
    d = shape[axis] // n_heads
    w = w.reshape(shape[:axis] + (n_heads, d) + shape[axis + 1:])
    pad = [(0, 0)] * w.ndim
    pad[axis + 1] = (0, GDN_DP - d)
    w = jnp.pad(w, pad)
    return w.reshape(shape[:axis] + (n_heads * GDN_DP,) + shape[axis + 1:])


def kernel(x, c, ctx, c_ctx, norm1_g, norm2_g, w_mod, b_mod, w_in, w_out, hy_conv_w, hy_conv_b, hy_f_w1, hy_f_b1, hy_f_w2, hy_f_b2, hy_f_w3, hy_sin_freq, hy_d, gla_w_a2, gla_b_a, gla_norm_g, gdn_conv_w, gdn_a_log, gdn_dt_bias, gdn_norm_g, w_mlp1, w_mlp2, final_norm_g):
    batch, seq, d_model = x.shape
    ctx_len = ctx.shape[1]
    depth = w_mod.shape[0]
    hy_c = hy_d.shape[-1]
    gla_kw = gla_w_a2.shape[-1]
    gla_dv = gla_norm_g.shape[-1]
    gla_vw = GLA_H * gla_dv
    gdn_d = gdn_norm_g.shape[-1]
    gdn_w = GDN_H * gdn_d
    gdn_wp = GDN_H * GDN_DP

    n2 = LANES
    n1 = 2 * seq // n2
    fft_tabs = _fft_tables(n1, n2)
    dft_tabs = _dft_tables(ctx_len)
    emb_lat = _hy_embedding(seq, HY_EMB_PAD)
    emb_ctx = _hy_embedding(ctx_len, HY_EMB_PAD)
    hy_deltas = jnp.abs(jnp.linspace(math.log(HY_DECAY_TARGET) / HY_SLOW_PCT,
                                     math.log(HY_DECAY_TARGET) / HY_FAST_PCT, hy_c, dtype=F32))[None, :]

    pad_rows = -(batch + 1) % 8
    cc = jnp.concatenate([c, c_ctx[None, :], jnp.zeros((pad_rows, d_model), F32)], axis=0)
    mod = _modulation(cc, w_mod, b_mod)

    def mod_vecs(l, i):
        v = mod[l, :, i * d_model:(i + 1) * d_model]
        lat = v[:batch, None, :]
        cx = jnp.broadcast_to(v[batch][None, None, :], (batch, 1, d_model))
        return lat, cx

    zeros_gla = jnp.zeros((batch, gla_vw, gla_kw), F32)
    zeros_gdn = jnp.zeros((batch, GDN_H, GDN_DP, GDN_DP), F32)
    gla_bd_gain = jnp.tile(gla_norm_g, (1, GLA_H))

    for l in range(depth):
        with_ctx_out = l < depth - 1
        (sh1, csh1), (sc1, csc1), (g1, cg1), (sh2, csh2), (sc2, csc2), (g2, cg2) = [mod_vecs(l, i) for i in range(N_MOD)]

        wl = w_in[l]
        o_gla = 3 * hy_c
        o_gdn = o_gla + 2 * gla_kw + 2 * gla_vw + 2 * GLA_RANK
        o_gdn_gate = o_gdn + 3 * gdn_w
        o_gdn_small = o_gdn_gate + gdn_w
        w_small = jnp.concatenate([wl[:, o_gdn - 2 * GLA_RANK:o_gdn], wl[:, o_gdn_small:]], axis=1)
        w_small = jnp.pad(w_small, ((0, 0), (0, LANES - w_small.shape[1])))
        in_plain = [
            wl[:, o_gla:o_gla + 2 * gla_kw],
            wl[:, o_gla + 2 * gla_kw:o_gla + 2 * gla_kw + gla_vw],
            wl[:, o_gla + 2 * gla_kw + gla_vw:o_gla + 2 * gla_kw + 2 * gla_vw],
            _pad_heads(wl[:, o_gdn_gate:o_gdn_small], GDN_H, 1),
            w_small,
        ]
        in_plain = [w.astype(BF16) for w in in_plain]
        in_hy = wl[:, :o_gla].astype(BF16)
        in_gdn = _pad_heads(wl[:, o_gdn:o_gdn_gate], 3 * GDN_H, 1).astype(BF16)
        gdn_cw = _pad_heads(gdn_conv_w[l], 3 * GDN_H, 1)
        gdn_gain = _pad_heads(gdn_norm_g[l][None, :].repeat(GDN_H, 0).reshape(1, gdn_w), GDN_H, 1)
        wo = w_out[l]
        w_hy = wo[:hy_c].astype(BF16)
        w_gla = wo[hy_c:hy_c + gla_vw].astype(BF16)
        w_gdn = _pad_heads(wo[hy_c + gla_vw:], GDN_H, 0).astype(BF16)
        w1 = w_mlp1[l].astype(BF16)
        w2 = w_mlp2[l].astype(BF16)
        filt_args = (hy_f_w1[l], hy_f_b1[l], hy_f_w2[l], hy_f_b2[l], hy_f_w3[l], hy_sin_freq[l], hy_deltas)

        def mixer_parts(tokens, shift, scale, seg_len, s0_gla, s0_gdn):
            length = tokens.shape[1]
            tl = min(SCAN_ROWS, length)
            z_qk, z_v, z_gate, z_gdn_gate, z_small, u, q, k, v = _inproj(
                tokens, shift, scale, norm1_g[l], in_plain, in_hy, in_gdn, hy_conv_w[l], hy_conv_b[l], gdn_cw,
                seg_len, gdn_d ** -0.5, min(PROJ_ROWS, length))
            b_f, b_b, gc = _gates(z_small, gla_w_a2[l], gla_b_a[l], gdn_a_log[l], gdn_dt_bias[l], tl)
            gla_f, gla_b, gla_sf, gla_sb = _gla_scan(z_qk, z_v, b_f, b_b, s0_gla[0], s0_gla[1], tl)
            gdn_f, gdn_b, gdn_sf, gdn_sb = _gdn_scan(q, k, v, gc, s0_gdn[0], s0_gdn[1], tl)
            return u, (gla_f, gla_b, z_gate), (gdn_f, gdn_b, z_gdn_gate), (gla_sf, gla_sb), (gdn_sf, gdn_sb)

        def finish(tokens, gate1, hy, gla, gdn):
            return _outproj(tokens, gate1, hy, *gla, *gdn, gla_bd_gain[l][None, :], gdn_gain, w_hy, w_gla, w_gdn,
                            gla_dv, gdn_d, min(PROJ_ROWS, tokens.shape[1]))

        def mlp(tokens, shift, scale, gate2, final_norm):
            return _mlp(tokens, shift, scale, gate2, norm2_g[l], w1, w2, final_norm_g, final_norm,
                        min(MLP_ROWS, tokens.shape[1]), MLP_HIDDEN)

        u_c, gla_c, gdn_c, gla_s, gdn_s = mixer_parts(ctx, csh1, csc1, ctx_len,
                                                      (zeros_gla, zeros_gla), (zeros_gdn, zeros_gdn))
        u_l, gla_l, gdn_l, _, _ = mixer_parts(x, sh1, sc1, GRID_W, gla_s, gdn_s)
        h_re, h_im = _filter_spectrum(*_hy_filters(emb_lat, *filt_args, min(PROJ_ROWS, seq)), fft_tabs, n1, n2)
        nb = hy_c // LANES
        y1 = _fftconv_gate(u_l, 0, u_l, nb, h_re, h_im, 0, hy_d[l, 0], fft_tabs, n1, n2)
        hy_l = _fftconv_gate(y1, 0, u_l, 2 * nb, h_re, h_im, nb, hy_d[l, 1], fft_tabs, n1, n2)
        x = finish(x, g1, hy_l, gla_l, gdn_l)
        x = mlp(x, sh2, sc2, g2, not with_ctx_out)

        if with_ctx_out:
            coef = _dense_filter_spectrum(*_hy_filters(emb_ctx, *filt_args, ctx_len), dft_tabs)
            y1c = _dftconv_gate(u_c, 0, u_c, 1, *coef, 0, hy_d[l, 0], dft_tabs)
            hy_c_out = _dftconv_gate(y1c, 0, u_c, 2, *coef, 1, hy_d[l, 1], dft_tabs)
            ctx = finish(ctx, cg1, hy_c_out, gla_c, gdn_c)
            ctx = mlp(ctx, csh2, csc2, cg2, False)
    return x
```

```python
import functools
import math

import numpy as np
import jax
import jax.numpy as jnp
from jax import lax
from jax.experimental import pallas as pl
from jax.experimental.pallas import tpu as pltpu

F32 = jnp.float32
BF16 = jnp.bfloat16
HIGHEST = lax.Precision.HIGHEST

NORM_EPS = 1e-6
N_MOD = 6
GRID_W = 64
CHUNK = 64
LANES = 128

HY_ORDER = 2
HY_SHORT = 3
HY_EMB = 33
HY_EMB_PAD = 40
HY_DECAY_TARGET = 1e-2
HY_FAST_PCT = 0.3
HY_SLOW_PCT = 1.5

GLA_H = 4
GLA_RANK = 16
GLA_TAU = 16.0
GDN_H = 4
GDN_DP = LANES
PITCH_PAD = 8
FFT_UNROLL = 8

VMEM_LIMIT = 56 * 1024 * 1024

SCAN_ROWS = 4 * CHUNK
PROJ_ROWS = 512
MLP_ROWS = 1024
MLP_HIDDEN = 1024
GDN_BATCH_ROWS = 2


def _cparams(n_grid):
    return pltpu.CompilerParams(dimension_semantics=("arbitrary",) * n_grid, vmem_limit_bytes=VMEM_LIMIT)


def _bdot(a, b):
    return jnp.dot(a.astype(BF16), b.astype(BF16), preferred_element_type=F32)


def _bdot_nt(a, b):
    return lax.dot_general(a.astype(BF16), b.astype(BF16), (((1,), (1,)), ((), ())), preferred_element_type=F32)


def _bdot_tn(a, b):
    return lax.dot_general(a.astype(BF16), b.astype(BF16), (((0,), (0,)), ((), ())), preferred_element_type=F32)


def _const_spec(shape):
    return pl.BlockSpec(shape, lambda *_: (0,) * len(shape))


def _mod_kernel(c_ref, w_ref, b_ref, o_ref):
    c = c_ref[...]
    o_ref[0] = _bdot(c * jax.nn.sigmoid(c), w_ref[0]) + b_ref[0]


def _modulation(cc, w_mod, b_mod):
    depth, d, n = w_mod.shape
    tn = n // 4
    return pl.pallas_call(
        _mod_kernel,
        grid=(depth, n // tn),
        in_specs=[pl.BlockSpec(cc.shape, lambda l, j: (0, 0)),
                  pl.BlockSpec((1, d, tn), lambda l, j: (l, 0, j)),
                  pl.BlockSpec((1, 1, tn), lambda l, j: (l, 0, j))],
        out_specs=pl.BlockSpec((1, cc.shape[0], tn), lambda l, j: (l, 0, j)),
        out_shape=jax.ShapeDtypeStruct((depth, cc.shape[0], n), F32),
        compiler_params=_cparams(2),
        name="modulation",
    )(cc, w_mod.astype(BF16), b_mod.reshape(depth, 1, n))


def _rms_modulate(x, g, shift, scale):
    y = x * lax.rsqrt(jnp.mean(x * x, axis=-1, keepdims=True) + NORM_EPS) * g
    return y * (1.0 + scale) + shift


def _inproj_kernel(x_ref, sh_ref, sc_ref, g_ref, w_hy_ref, w_gdn_ref, hy_cw_ref, hy_cb_ref, gdn_cw_ref, *refs,
                   seg_len, q_scale):
    n_plain = (len(refs) - 4) // 2
    w_refs, plain_refs = refs[:n_plain], refs[n_plain:2 * n_plain]
    u_ref, q_ref, k_ref, v_ref = refs[2 * n_plain:]
    hb = _rms_modulate(x_ref[0], g_ref[...], sh_ref[0], sc_ref[0]).astype(BF16)
    piece = 2 * LANES
    width = q_ref.shape[-1]
    z_hy = [jnp.dot(hb, w_hy_ref[:, c0:c0 + piece], preferred_element_type=F32)
            for c0 in range(0, u_ref.shape[-1], piece)]
    z_gdn = [jnp.dot(hb, w_gdn_ref[:, c0:c0 + piece], preferred_element_type=F32)
             for c0 in range(0, 3 * width, piece)]
    for w_ref, o_ref in zip(w_refs, plain_refs):
        o_ref[0] = jnp.dot(hb, w_ref[...], preferred_element_type=F32)
    for i, c0 in enumerate(range(0, u_ref.shape[-1], piece)):
        cols = slice(c0, c0 + piece)
        u_ref[0, :, cols] = _seg_conv(z_hy[i], hy_cw_ref[:, cols], seg_len) + hy_cb_ref[:, cols]
    for part, (o_ref, norm_scale) in enumerate(((q_ref, q_scale), (k_ref, 1.0), (v_ref, None))):
        for c0 in range(0, width, piece):
            cols = slice(part * width + c0, part * width + c0 + piece)
            y = _seg_conv(z_gdn[(part * width + c0) // piece], gdn_cw_ref[:, cols], seg_len)
            y = y * jax.nn.sigmoid(y)
            for h0 in range(0, piece, GDN_DP):
                t = y[:, h0:h0 + GDN_DP]
                if norm_scale is not None:
                    t = t * (lax.rsqrt(jnp.sum(t * t, axis=-1, keepdims=True) + NORM_EPS) * norm_scale)
                o_ref[0, :, c0 + h0:c0 + h0 + GDN_DP] = t


def _inproj(x, shift, scale, g, plain_weights, w_hy, w_gdn, hy_cw, hy_cb, gdn_cw, seg_len, q_scale, tm):
    b, l, d = x.shape
    hyw, gdw = w_hy.shape[1], w_gdn.shape[1] // 3
    vec = pl.BlockSpec((1, 1, d), lambda i, j: (i, 0, 0))
    widths = [w.shape[1] for w in plain_weights] + [hyw, gdw, gdw, gdw]
    return pl.pallas_call(
        functools.partial(_inproj_kernel, seg_len=seg_len, q_scale=q_scale),
        grid=(b, l // tm),
        in_specs=[pl.BlockSpec((1, tm, d), lambda i, j: (i, j, 0)), vec, vec, _const_spec((1, d)),
                  _const_spec(w_hy.shape), _const_spec(w_gdn.shape), _const_spec(hy_cw.shape),
                  _const_spec((1, hyw)), _const_spec(gdn_cw.shape)]
        + [_const_spec(w.shape) for w in plain_weights],
        out_specs=[pl.BlockSpec((1, tm, w), lambda i, j: (i, j, 0)) for w in widths],
        out_shape=[jax.ShapeDtypeStruct((b, l, w), F32) for w in widths],
        compiler_params=_cparams(2),
        name="inproj",
    )(x, shift, scale, g.reshape(1, d), w_hy, w_gdn, hy_cw, hy_cb.reshape(1, hyw), gdn_cw, *plain_weights)


def _seg_conv(z, w, seg_len):
    rows, width = z.shape
    taps = w.shape[0]
    half = taps // 2
    pos = lax.broadcasted_iota(jnp.int32, (rows, LANES), 0) % seg_len
    valid = {d: (pos >= -d) if d < 0 else (pos < seg_len - d) for d in range(-half, half + 1) if d != 0}
    tiles = []
    for c0 in range(0, width, LANES):
        cols = slice(c0, c0 + LANES)
        zc = z[:, cols]
        acc = zc * w[half:half + 1, cols]
        for k in range(taps):
            d = k - half
            if d != 0:
                shifted = pltpu.roll(zc, (-d) % rows, 0)
                acc = acc + jnp.where(valid[d], shifted, 0.0) * w[k:k + 1, cols]
        tiles.append(acc)
    return jnp.concatenate(tiles, axis=1)


def _hy_filter_kernel(z_ref, w1_ref, b1_ref, w2_ref, b2_ref, w3_ref, f_ref, dl_ref, hf_ref, hb_ref):
    hy_c = dl_ref.shape[1]
    z = z_ref[...]
    fdot = lambda a, b: jnp.dot(a, b, precision=HIGHEST, preferred_element_type=F32)
    h = jnp.sin(f_ref[0:1, :] * (fdot(z, w1_ref[...]) + b1_ref[...]))
    h = jnp.sin(f_ref[1:2, :] * (fdot(h, w2_ref[...]) + b2_ref[...]))
    h = fdot(h, w3_ref[...])
    window = jnp.exp(-z[:, 0:1] * dl_ref[...])
    row = lax.broadcasted_iota(jnp.int32, window.shape, 0) + pl.program_id(0) * z.shape[0]
    for o in range(HY_ORDER):
        base = 2 * o * hy_c
        hf_ref[:, o * hy_c:(o + 1) * hy_c] = h[:, base:base + hy_c] * window
        hb_ref[:, o * hy_c:(o + 1) * hy_c] = jnp.where(row > 0, h[:, base + hy_c:base + 2 * hy_c] * window, 0.0)


def _hy_filters(z, w1, b1, w2, b2, w3, sin_freq, deltas, tl):
    l, emb = z.shape
    fh = w2.shape[0]
    hy_c = deltas.shape[1]
    out = jax.ShapeDtypeStruct((l, HY_ORDER * hy_c), F32)
    rows = lambda w: pl.BlockSpec((tl, w), lambda i: (i, 0))
    return pl.pallas_call(
        _hy_filter_kernel,
        grid=(l // tl,),
        in_specs=[rows(emb), _const_spec((emb, fh)), _const_spec((1, fh)), _const_spec((fh, fh)), _const_spec((1, fh)),
                  _const_spec(w3.shape), _const_spec((2, fh)), _const_spec((1, hy_c))],
        out_specs=[rows(HY_ORDER * hy_c)] * 2,
        out_shape=[out, out],
        compiler_params=_cparams(1),
        name="hy_filters",
    )(z, jnp.pad(w1, ((0, emb - w1.shape[0]), (0, 0))), b1.reshape(1, fh), w2, b2.reshape(1, fh), w3, sin_freq, deltas)


def _hy_embedding(length, emb):
    t = jnp.linspace(0.0, 1.0, length, dtype=F32)[:, None]
    bands = (HY_EMB - 1) // 2
    f = jnp.linspace(1e-4, bands - 1, bands, dtype=F32)
    w = 2 * math.pi * jnp.arange(length, dtype=F32) / length
    ang = w[:, None] * f[None, :]
    z = jnp.concatenate([t, jnp.cos(ang), -jnp.sin(ang)], axis=-1)
    return jnp.pad(z, ((0, 0), (0, emb - z.shape[1])))


def _fft_tables(n1, n2):
    n = n1 * n2
    nk = n1 // 2 + 1
    kp = -(-nk // 8) * 8
    s1 = np.arange(n1 // 2)[None, :, None]
    s2 = np.arange(n2)[:, None, None]
    k1 = np.arange(kp)[None, None, :]
    theta = 2.0 * np.pi * ((k1 * (n2 * s1 + s2)) % n) / n
    tt_fwd = np.concatenate([np.cos(theta), -np.sin(theta)], axis=-1)
    weight = np.where(np.arange(kp) < nk, 2.0, 0.0)
    weight[0] = weight[nk - 1] = 1.0
    tt_inv = np.concatenate([np.cos(theta) * weight, -np.sin(theta) * weight], axis=-1)
    phi = 2.0 * np.pi * ((np.arange(n2)[:, None] * np.arange(n2)[None, :]) % n2) / n2
    c2, s2m = np.cos(phi), np.sin(phi)
    fwd = np.block([[c2, s2m], [-s2m, c2]])
    inv = np.block([[c2, -s2m], [s2m, c2]])
    return tuple(jnp.asarray(t, F32) for t in (tt_fwd, tt_inv, fwd, inv))


def _for_k1(body, nk):
    lax.fori_loop(0, nk - 1, lambda k1, c: body(k1) or c, 0, unroll=min(FFT_UNROLL, nk - 1))
    body(nk - 1)


def _time_pitch(n2):
    return n2 + PITCH_PAD


def _spec_pitch(n2):
    return 2 * n2 + PITCH_PAD


def _dft_load_time(load_block, time_ref, n1, n2):
    def copy(s1, carry):
        time_ref[pl.ds(pl.multiple_of(s1 * _time_pitch(n2), 8), n2), :] = load_block(
            pl.ds(pl.multiple_of(s1 * n2, n2), n2))
        return carry

    lax.fori_loop(0, n1 // 2, copy, 0, unroll=min(FFT_UNROLL, n1 // 2))


def _dft_stage_a(time_ref, tt_ref, spec_ref, n1, n2):
    kp = tt_ref.shape[-1] // 2

    def stage_a(s2, carry):
        slab = time_ref[pl.ds(s2, n1 // 2, stride=_time_pitch(n2)), :]
        res = _bdot_tn(tt_ref[s2], slab)
        spec_ref[pl.ds(s2, kp, stride=_spec_pitch(n2)), :] = res[:kp]
        spec_ref[pl.ds(n2 + s2, kp, stride=_spec_pitch(n2)), :] = res[kp:]
        return carry

    lax.fori_loop(0, n2, stage_a, 0, unroll=4 * FFT_UNROLL)


def _aligned_rows(start, size, align):
    return pl.ds(start if isinstance(start, int) else pl.multiple_of(start, align), size)


def _spec_rows(k1, n2):
    return _aligned_rows(k1 * _spec_pitch(n2), 2 * n2, 8)


def _spectrum_kernel(hf_ref, hb_ref, tt_ref, fwd_ref, hr_ref, hi_ref, time_ref, spec_ref, *, n1, n2):
    inv_n = 1.0 / (n1 * n2)
    fwd = fwd_ref[...].astype(BF16)
    for sign, src_ref in ((1.0, hf_ref), (-1.0, hb_ref)):
        _dft_load_time(lambda rows: src_ref[rows, :], time_ref, n1, n2)
        _dft_stage_a(time_ref, tt_ref, spec_ref, n1, n2)

        def stage_b(k1):
            rows = _spec_rows(k1, n2)
            spec = jnp.dot(fwd, spec_ref[rows, :].astype(BF16), preferred_element_type=F32) * inv_n
            hrows = _aligned_rows(k1 * n2, n2, n2)
            if sign > 0:
                hr_ref[hrows, :] = spec[:n2]
                hi_ref[hrows, :] = spec[n2:]
            else:
                hr_ref[hrows, :] += spec[:n2]
                hi_ref[hrows, :] -= spec[n2:]

        _for_k1(stage_b, n1 // 2 + 1)


def _filter_spectrum(hf, hb, tables, n1, n2):
    tt, _, fwd, _ = tables
    l, c = hf.shape
    n = (n1 // 2 + 1) * n2
    chan = lambda rows: pl.BlockSpec((rows, LANES), lambda j: (0, j))
    return pl.pallas_call(
        functools.partial(_spectrum_kernel, n1=n1, n2=n2),
        grid=(c // LANES,),
        in_specs=[chan(l), chan(l), _const_spec(tt.shape), _const_spec(fwd.shape)],
        out_specs=[chan(n), chan(n)],
        out_shape=[jax.ShapeDtypeStruct((n, c), F32)] * 2,
        scratch_shapes=_fft_scratch(tt, n1, n2),
        compiler_params=_cparams(1),
        name="hy_spectrum",
    )(hf, hb, tt, fwd)


def _fft_scratch(tt, n1, n2):
    kp = tt.shape[-1] // 2
    return [pltpu.VMEM((n1 // 2 * _time_pitch(n2), LANES), F32), pltpu.VMEM((kp * _spec_pitch(n2), LANES), F32)]


def _fftconv_kernel(u_ref, gate_ref, hr_ref, hi_ref, d_ref, ttf_ref, tti_ref, fwd_ref, inv_ref, o_ref,
                    time_ref, spec_ref, *, n1, n2):
    n1h = n1 // 2
    kp = ttf_ref.shape[-1] // 2
    fwd = fwd_ref[...].astype(BF16)
    inv = inv_ref[...].astype(BF16)
    _dft_load_time(lambda rows: u_ref[0, rows, :], time_ref, n1, n2)
    _dft_stage_a(time_ref, ttf_ref, spec_ref, n1, n2)

    def stage_b(k1s):
        rows = [_spec_rows(k1, n2) for k1 in k1s]
        slab = jnp.concatenate([spec_ref[r, :].astype(BF16) for r in rows], axis=1)
        spec = jnp.dot(fwd, slab, preferred_element_type=F32)
        br, bi = spec[:n2], spec[n2:]
        hrows = [_aligned_rows(k1 * n2, n2, n2) for k1 in k1s]
        hr = jnp.concatenate([hr_ref[r, :] for r in hrows], axis=1)
        hi = jnp.concatenate([hi_ref[r, :] for r in hrows], axis=1)
        y = jnp.concatenate([br * hr - bi * hi, br * hi + bi * hr], axis=0)
        g = jnp.dot(inv, y.astype(BF16), preferred_element_type=F32)
        for i, r in enumerate(rows):
            spec_ref[r, :] = g[:, i * LANES:(i + 1) * LANES]

    lax.fori_loop(0, n1h // 2, lambda kk, c: stage_b((2 * kk, 2 * kk + 1)) or c, 0,
                  unroll=min(FFT_UNROLL // 2, n1h // 2))
    stage_b((n1h,))

    def stage_a_inv(s2, carry):
        g = jnp.concatenate([spec_ref[pl.ds(s2, kp, stride=_spec_pitch(n2)), :],
                             spec_ref[pl.ds(n2 + s2, kp, stride=_spec_pitch(n2)), :]], axis=0)
        time_ref[pl.ds(s2, n1h, stride=_time_pitch(n2)), :] = _bdot(tti_ref[s2], g)
        return carry

    lax.fori_loop(0, n2, stage_a_inv, 0, unroll=4 * FFT_UNROLL)

    def gate_rows(s1, carry):
        rows = pl.ds(pl.multiple_of(s1 * n2, n2), n2)
        conv = time_ref[pl.ds(pl.multiple_of(s1 * _time_pitch(n2), 8), n2), :]
        o_ref[0, rows, :] = gate_ref[0, rows, :] * (conv + d_ref[...] * u_ref[0, rows, :])
        return carry

    lax.fori_loop(0, n1h, gate_rows, 0, unroll=min(FFT_UNROLL, n1h))


def _fftconv_gate(u_arr, u_blk, gate_arr, gate_blk, hr, hi, h_blk, d, tables, n1, n2):
    tt_fwd, tt_inv, fwd, inv = tables
    b, l, _ = u_arr.shape
    n = hr.shape[0]
    c = d.shape[0]
    ncb = c // LANES
    seq = lambda off: pl.BlockSpec((1, l, LANES), lambda j, i: (i, 0, off + j))
    chan = lambda rows, off=0: pl.BlockSpec((rows, LANES), lambda j, i: (0, off + j))
    return pl.pallas_call(
        functools.partial(_fftconv_kernel, n1=n1, n2=n2),
        grid=(ncb, b),
        in_specs=[seq(u_blk), seq(gate_blk), chan(n, h_blk), chan(n, h_blk), chan(1),
                  _const_spec(tt_fwd.shape), _const_spec(tt_inv.shape), _const_spec(fwd.shape),
                  _const_spec(inv.shape)],
        out_specs=pl.BlockSpec((1, l, LANES), lambda j, i: (i, 0, j)),
        out_shape=jax.ShapeDtypeStruct((b, l, c), F32),
        scratch_shapes=_fft_scratch(tt_fwd, n1, n2),
        compiler_params=_cparams(2),
        name="hy_fftconv",
    )(u_arr, gate_arr, hr, hi, d.reshape(1, c), tt_fwd, tt_inv, fwd, inv)


def _dft_tables(l):
    n = 2 * l
    ang = 2.0 * np.pi * ((np.arange(l)[:, None] * np.arange(l)[None, :]) % n) / n
    cf = np.cos(ang)
    sf = -np.sin(ang)
    sf[0, :] = (-1.0) ** np.arange(l)
    fwd = np.concatenate([cf, sf], axis=0)
    return jnp.asarray(fwd, F32), jnp.asarray(fwd.T, F32)


def _dftconv_kernel(u_ref, gate_ref, p_ref, q_ref, r_ref, d_ref, fwd_ref, inv_ref, o_ref):
    u = u_ref[0]
    l = u.shape[0]
    spec = _bdot(fwd_ref[...], u)
    xr, xi = spec[:l], spec[l:]
    q = q_ref[...]
    y = jnp.concatenate([xr * p_ref[...] - xi * q, xr * q + xi * r_ref[...]], axis=0)
    conv = _bdot(inv_ref[...], y)
    o_ref[0] = gate_ref[0] * (conv + d_ref[...] * u)


def _dense_spectrum_kernel(hf_ref, hb_ref, fwd_ref, p_ref, q_ref, r_ref):
    l = hf_ref.shape[0]
    xf = _bdot(fwd_ref[...], hf_ref[...])
    xb = _bdot(fwd_ref[...], hb_ref[...])
    first = lax.broadcasted_iota(jnp.int32, (l, hf_ref.shape[1]), 0) == 0
    scale = jnp.where(first, 0.5 / l, 1.0 / l)
    re = (xf[:l] + xb[:l]) * scale
    p_ref[...] = re
    q_ref[...] = jnp.where(first, 0.0, (xf[l:] - xb[l:]) * scale)
    r_ref[...] = jnp.where(first, (xf[l:] + xb[l:]) * scale, re)


def _dense_filter_spectrum(hf, hb, tables):
    fwd, _ = tables
    out = jax.ShapeDtypeStruct(hf.shape, F32)
    return pl.pallas_call(
        _dense_spectrum_kernel,
        grid=(1,),
        in_specs=[_const_spec(hf.shape), _const_spec(hb.shape), _const_spec(fwd.shape)],
        out_specs=[_const_spec(hf.shape)] * 3,
        out_shape=[out] * 3,
        compiler_params=_cparams(1),
        name="hy_dense_spectrum",
    )(hf, hb, fwd)


def _dftconv_gate(u_arr, u_blk, gate_arr, gate_blk, p, q, r, h_blk, d, tables):
    fwd, inv = tables
    b, l, _ = u_arr.shape
    c = d.shape[0]
    seq = lambda off: pl.BlockSpec((1, l, c), lambda i: (i, 0, off))
    coef = pl.BlockSpec((l, c), lambda i: (0, h_blk))
    return pl.pallas_call(
        _dftconv_kernel,
        grid=(b,),
        in_specs=[seq(u_blk), seq(gate_blk), coef, coef, coef,
                  _const_spec((1, c)), _const_spec(fwd.shape), _const_spec(inv.shape)],
        out_specs=pl.BlockSpec((1, l, c), lambda i: (i, 0, 0)),
        out_shape=jax.ShapeDtypeStruct((b, l, c), F32),
        compiler_params=_cparams(1),
        name="hy_dftconv",
    )(u_arr, gate_arr, p, q, r, d.reshape(1, c), fwd, inv)


GDN_GF, GDN_GB, GDN_BF, GDN_BB = 32, 36, 40, 44

def _chunk_cumsum_mats(tl):
    r = np.arange(tl)
    same = (r[:, None] // CHUNK) == (r[None, :] // CHUNK)
    lower = same & (r[None, :] <= r[:, None])
    upper = same & (r[None, :] >= r[:, None])
    return jnp.asarray(lower, F32), jnp.asarray(upper, F32)


def _tri_cumsum(tri, x):
    hi = x.astype(BF16)
    rest = x - hi.astype(F32)
    mid = rest.astype(BF16)
    lo = (rest - mid.astype(F32)).astype(BF16)
    dot = lambda part: jnp.dot(tri, part, preferred_element_type=F32)
    return dot(hi) + dot(mid) + dot(lo)


def _gates_kernel(small_ref, w_ref, b_ref, nea_ref, dtb_ref, lo_ref, up_ref, bf_ref, bb_ref, gc_ref):
    small = small_ref[0]
    kw = bf_ref.shape[-1]
    log_decay = nea_ref[...] * jax.nn.softplus(small + dtb_ref[...])
    cums = []
    for d, tri_ref in enumerate((lo_ref, up_ref)):
        pre = jnp.dot(small[:, d * GLA_RANK:(d + 1) * GLA_RANK], w_ref[d], precision=HIGHEST,
                      preferred_element_type=F32) + b_ref[d]
        vals = jnp.concatenate([jax.nn.log_sigmoid(pre) * (1.0 / GLA_TAU), log_decay], axis=1)
        cums.append(_tri_cumsum(tri_ref[...].astype(BF16), vals))
    bf_ref[0] = cums[0][:, :kw]
    bb_ref[0] = cums[1][:, :kw]
    lane = lax.broadcasted_iota(jnp.int32, small.shape, 1)
    gc_ref[0] = jnp.where(lane < GDN_GB, cums[0][:, kw:], jnp.where(lane < GDN_BF, cums[1][:, kw:],
                                                                  jax.nn.sigmoid(small)))


def _gates(small, w_a2, b_a, a_log, dt_bias, tl):
    b, l, sw = small.shape
    kw = w_a2.shape[-1]
    nea = jnp.zeros((1, LANES), F32).at[0, GDN_GF:GDN_BF].set(-jnp.exp(a_log.reshape(-1)))
    dtb = jnp.zeros((1, LANES), F32).at[0, GDN_GF:GDN_BF].set(dt_bias.reshape(-1))
    lower, upper = _chunk_cumsum_mats(tl)
    tok = lambda w: pl.BlockSpec((1, tl, w), lambda i, j: (i, j, 0))
    return pl.pallas_call(
        _gates_kernel,
        grid=(b, l // tl),
        in_specs=[tok(sw), _const_spec(w_a2.shape), _const_spec((2, 1, kw)), _const_spec((1, LANES)),
                  _const_spec((1, LANES)), _const_spec((tl, tl)), _const_spec((tl, tl))],
        out_specs=[tok(kw), tok(kw), tok(LANES)],
        out_shape=[jax.ShapeDtypeStruct((b, l, kw), F32)] * 2 + [jax.ShapeDtypeStruct((b, l, LANES), F32)],
        compiler_params=_cparams(2),
        name="scan_gates",
    )(small, w_a2, b_a.reshape(2, 1, kw), nea, dtb, lower, upper)


def _gla_scan_kernel(qkf_ref, vf_ref, bf_ref, qkb_ref, vb_ref, bb_ref, s0f_ref, s0b_ref, hm_ref, vm_ref, bd_ref,
                     of_ref, ob_ref, sff_ref, sfb_ref, stf_ref, stb_ref, *, n_chunks, kw, q_scale):
    j = pl.program_id(1)

    @pl.when(j == 0)
    def _():
        stf_ref[...] = s0f_ref[0]
        stb_ref[...] = s0b_ref[0]

    ri = lax.broadcasted_iota(jnp.int32, (CHUNK, GLA_H * CHUNK), 0)
    ci = lax.broadcasted_iota(jnp.int32, (CHUNK, GLA_H * CHUNK), 1) % CHUNK
    hm = hm_ref[...]
    vm = vm_ref[...]
    bd = bd_ref[...]
    dirs = ((qkf_ref, vf_ref, bf_ref, of_ref, stf_ref, ri >= ci, CHUNK - 1, CHUNK // 2 - 1, False),
            (qkb_ref, vb_ref, bb_ref, ob_ref, stb_ref, ci >= ri, 0, CHUNK // 2, True))

    qts, kstacks, vstacks, keeps, qes, vs, kds, decs, sinks = [], [], [], [], [], [], [], [], []
    for qk_ref, v_ref, b_ref, o_ref, st_ref, keep, last, mid, rev in dirs:
        for c in range(n_chunks):
            rows = slice((n_chunks - 1 - c if rev else c) * CHUNK, (n_chunks - c if rev else c + 1) * CHUNK)
            qk = qk_ref[0, rows, :]
            q, k = qk[:, :kw] * q_scale, qk[:, kw:]
            v = v_ref[0, rows, :]
            bc = b_ref[0, rows, :]
            b_mid = bc[mid:mid + 1, :]
            b_last = bc[last:last + 1, :]
            kt = k * jnp.exp(b_mid - bc)
            qts.append(q * jnp.exp(bc - b_mid))
            kstacks.append(jnp.concatenate([kt * hm[h:h + 1, :] for h in range(GLA_H)], axis=0).astype(BF16))
            vstacks.append(jnp.concatenate([v * vm[h:h + 1, :] for h in range(GLA_H)], axis=0).astype(BF16))
            keeps.append(keep)
            qes.append((q * jnp.exp(bc)).astype(BF16))
            vs.append(v)
            kds.append(k * jnp.exp(b_last - bc))
            decs.append(jnp.exp(b_last))
            sinks.append((o_ref, rows))
    attns = [jnp.where(keep, _bdot_nt(qt, ks), 0.0) for keep, qt, ks in zip(keeps, qts, kstacks)]
    o_intra = [_bdot(a, vst) for a, vst in zip(attns, vstacks)]
    upds = [_bdot_tn(v, kd) * bd for v, kd in zip(vs, kds)]

    for d, (_, _, _, _, st_ref, _, _, _, _) in enumerate(dirs):
        st = st_ref[...]
        for c in range(n_chunks):
            i = d * n_chunks + c
            o_ref, rows = sinks[i]
            o_ref[0, rows, :] = o_intra[i] + _bdot_nt(qes[i], st)
            st = st * decs[i] + upds[i]
        st_ref[...] = st

    @pl.when(j == pl.num_programs(1) - 1)
    def _():
        sff_ref[0] = stf_ref[...]
        sfb_ref[0] = stb_ref[...]


def _gla_scan(qk, v, b_f, b_b, s0_f, s0_b, tl):
    b, l, kw2 = qk.shape
    kw, vw = kw2 // 2, v.shape[-1]
    dk, dv = kw // GLA_H, vw // GLA_H
    nblk = l // tl
    heads_k = np.arange(kw) // dk
    heads_v = np.arange(vw) // dv
    hm = jnp.asarray(np.arange(8)[:, None] == heads_k[None, :], F32)
    vm = jnp.asarray(np.arange(8)[:, None] == heads_v[None, :], F32)
    bd = jnp.asarray(heads_v[:, None] == heads_k[None, :], F32)
    fwd = lambda w: pl.BlockSpec((1, tl, w), lambda i, j: (i, j, 0))
    bwd = lambda w: pl.BlockSpec((1, tl, w), lambda i, j: (i, nblk - 1 - j, 0))
    state = pl.BlockSpec((1, vw, kw), lambda i, j: (i, 0, 0))
    return pl.pallas_call(
        functools.partial(_gla_scan_kernel, n_chunks=tl // CHUNK, kw=kw, q_scale=dk ** -0.5),
        grid=(b, nblk),
        in_specs=[fwd(kw2), fwd(vw), fwd(kw), bwd(kw2), bwd(vw), bwd(kw), state, state,
                  _const_spec(hm.shape), _const_spec(vm.shape), _const_spec(bd.shape)],
        out_specs=[fwd(vw), bwd(vw), state, state],
        out_shape=[jax.ShapeDtypeStruct((b, l, vw), F32)] * 2 + [jax.ShapeDtypeStruct((b, vw, kw), F32)] * 2,
        scratch_shapes=[pltpu.VMEM((vw, kw), F32)] * 2,
        compiler_params=_cparams(2),
        name="gla_scan",
    )(qk, v, b_f, qk, v, b_b, s0_f, s0_b, hm, vm, bd)


def _unit_tri_inverses(mats, eye, m16, m32, m64):
    diag = [a * m16 for a in mats]
    inv = [eye + a for a in diag]
    pw = [a.astype(BF16) for a in diag]
    for _ in range(3):
        pw = [_bdot(p, p).astype(BF16) for p in pw]
        inv = [t + _bdot(t, p) for t, p in zip(inv, pw)]
    for mask in (m32, m64):
        inv_b = [t.astype(BF16) for t in inv]
        mid = [_bdot(t, a * mask) for t, a in zip(inv_b, mats)]
        inv = [t + _bdot(m, tb) for t, m, tb in zip(inv, mid, inv_b)]
    return inv


def _gdn_scan_kernel(qf_ref, kf_ref, vf_ref, gcf_ref, qb_ref, kb_ref, vb_ref, gcb_ref,
                     s0f_ref, s0b_ref, of_ref, ob_ref, sff_ref, sfb_ref, sf_ref, sb_ref, *, tl):
    j = pl.program_id(1)

    @pl.when(j == 0)
    def _():
        sf_ref[...] = s0f_ref[...]
        sb_ref[...] = s0b_ref[...]

    for bi in range(qf_ref.shape[0]):
        _gdn_block(bi, qf_ref, kf_ref, vf_ref, gcf_ref, qb_ref, kb_ref, vb_ref, gcb_ref, of_ref, ob_ref,
                   sf_ref, sb_ref, tl)

    @pl.when(j == pl.num_programs(1) - 1)
    def _():
        sff_ref[...] = sf_ref[...]
        sfb_ref[...] = sb_ref[...]


def _gdn_block(bi, qf_ref, kf_ref, vf_ref, gcf_ref, qb_ref, kb_ref, vb_ref, gcb_ref, of_ref, ob_ref,
               sf_ref, sb_ref, tl):
    nck = tl // CHUNK
    ri = lax.broadcasted_iota(jnp.int32, (tl, tl), 0)
    ci = lax.broadcasted_iota(jnp.int32, (tl, tl), 1)
    same = lambda n: (ri // n) == (ci // n)
    chunk = same(CHUNK)
    eye = (ri == ci).astype(F32)
    m16 = same(16).astype(F32)
    m32 = (same(32) & ~same(16)).astype(F32)
    m64 = (chunk & ~same(32)).astype(F32)
    dirs = ((qf_ref, kf_ref, vf_ref, gcf_ref, of_ref, sf_ref, ri > ci, ri >= ci, CHUNK - 1, GDN_GF, GDN_BF, False),
            (qb_ref, kb_ref, vb_ref, gcb_ref, ob_ref, sb_ref, ci > ri, ci >= ri, 0, GDN_GB, GDN_BB, True))

    qs, ks, vs, gcums, betas, stricts, incls, glasts, outs = [], [], [], [], [], [], [], [], []
    for q_ref, k_ref, v_ref, gc_ref, o_ref, s_ref, strict, incl, last, g_col, b_col, rev in dirs:
        gc = gc_ref[bi]
        gct = gc.T
        for h in range(GDN_H):
            cols = slice(h * GDN_DP, (h + 1) * GDN_DP)
            gcum = gc[:, g_col + h:g_col + h + 1]
            decay = jnp.exp(jnp.minimum(gcum - gct[g_col + h:g_col + h + 1, :], 0.0))
            qs.append(q_ref[bi, :, cols])
            ks.append(k_ref[bi, :, cols])
            vs.append(v_ref[bi, :, cols])
            gcums.append(gcum)
            betas.append(gc[:, b_col + h:b_col + h + 1])
            stricts.append(jnp.where(chunk & strict, decay, 0.0))
            incls.append(jnp.where(chunk & incl, decay, 0.0))
            glasts.append(jnp.concatenate(
                [jnp.broadcast_to(gcum[c * CHUNK + last:c * CHUNK + last + 1, :], (CHUNK, 1)) for c in range(nck)],
                axis=0))
            outs.append((o_ref, s_ref, h, cols, rev))
    n_pairs = len(qs)
    e_cols = [jnp.exp(g) for g in gcums]
    kbs = [k * b for k, b in zip(ks, betas)]
    mats = [-_bdot_nt(kb, k) * d for kb, k, d in zip(kbs, ks, stricts)]
    invs = _unit_tri_inverses(mats, eye, m16, m32, m64)
    wus = [_bdot(t, jnp.concatenate([kb * e, v * b], axis=1))
           for t, kb, e, v, b in zip(invs, kbs, e_cols, vs, betas)]
    attns = [_bdot_nt(q, k) * d for q, k, d in zip(qs, ks, incls)]
    awus = [_bdot(a, wu) for a, wu in zip(attns, wus)]
    q_effs = [(q * e - awu[:, :GDN_DP]).astype(BF16) for q, e, awu in zip(qs, e_cols, awus)]
    ws = [wu[:, :GDN_DP].astype(BF16) for wu in wus]
    us = [wu[:, GDN_DP:] for wu in wus]
    o_intra = [awu[:, GDN_DP:] for awu in awus]
    kds = [(k * jnp.exp(gl - g)).astype(BF16) for k, gl, g in zip(ks, glasts, gcums)]
    decs = [jnp.exp(gl) for gl in glasts]

    states = [s_ref[bi, h] for (_, s_ref, h, _, _) in outs]
    for c in range(nck):
        rows = [slice((nck - 1 - c if rev else c) * CHUNK, (nck - c if rev else c + 1) * CHUNK)
                for (_, _, _, _, rev) in outs]
        prods = [jnp.dot(jnp.concatenate([q_effs[i][rows[i]], ws[i][rows[i]]], axis=0), states[i].astype(BF16),
                         preferred_element_type=F32) for i in range(n_pairs)]
        for i, (o_ref, _, _, cols, _) in enumerate(outs):
            o_ref[bi, rows[i], cols] = o_intra[i][rows[i]] + prods[i][:CHUNK]
        v_news = [us[i][rows[i]] - prods[i][CHUNK:] for i in range(n_pairs)]
        states = [decs[i][rows[i]][0:1] * states[i] + _bdot_tn(kds[i][rows[i]], v_news[i]) for i in range(n_pairs)]
    for i, (_, s_ref, h, _, _) in enumerate(outs):
        s_ref[bi, h] = states[i]


def _gdn_scan(q, k, v, gc, s0_f, s0_b, tl):
    b, l, width = q.shape
    nblk = l // tl
    nb = GDN_BATCH_ROWS if b % GDN_BATCH_ROWS == 0 else 1
    fwd = lambda w: pl.BlockSpec((nb, tl, w), lambda i, j: (i, j, 0))
    bwd = lambda w: pl.BlockSpec((nb, tl, w), lambda i, j: (i, nblk - 1 - j, 0))
    state = pl.BlockSpec((nb, GDN_H, GDN_DP, GDN_DP), lambda i, j: (i, 0, 0, 0))
    st_shape = jax.ShapeDtypeStruct((b, GDN_H, GDN_DP, GDN_DP), F32)
    return pl.pallas_call(
        functools.partial(_gdn_scan_kernel, tl=tl),
        grid=(b // nb, nblk),
        in_specs=[fwd(width), fwd(width), fwd(width), fwd(LANES),
                  bwd(width), bwd(width), bwd(width), bwd(LANES), state, state],
        out_specs=[fwd(width), bwd(width), state, state],
        out_shape=[jax.ShapeDtypeStruct((b, l, width), F32)] * 2 + [st_shape] * 2,
        scratch_shapes=[pltpu.VMEM((nb, GDN_H, GDN_DP, GDN_DP), F32)] * 2,
        compiler_params=_cparams(2),
        name="gdn_scan",
    )(q, k, v, gc, q, k, v, gc, s0_f, s0_b)


def _gated_head_norm(o, gate, g_norm, ones_bd, inv_d):
    sq = o * o
    hi = sq.astype(BF16)
    lo = (sq - hi.astype(F32)).astype(BF16)
    ms = (jnp.dot(hi, ones_bd, preferred_element_type=F32) + jnp.dot(lo, ones_bd, preferred_element_type=F32)) * inv_d
    return o * lax.rsqrt(ms + NORM_EPS) * g_norm * (gate * jax.nn.sigmoid(gate))


def _outproj_kernel(x_ref, g1_ref, hy_ref, glf_ref, glb_ref, glg_ref, gdf_ref, gdb_ref, gdg_ref,
                    gln_ref, gdn_ref, glm_ref, gdm_ref, why_ref, wgl_ref, wgd_ref, o_ref, *, gla_dv, gdn_d):
    y_gla = _gated_head_norm(glf_ref[0] + glb_ref[0], glg_ref[0], gln_ref[...], glm_ref[...], 1.0 / gla_dv)
    y_gdn = _gated_head_norm(gdf_ref[0] + gdb_ref[0], gdg_ref[0], gdn_ref[...], gdm_ref[...], 1.0 / gdn_d)
    acc = _bdot(hy_ref[0], why_ref[...]) + _bdot(y_gla, wgl_ref[...]) + _bdot(y_gdn, wgd_ref[...])
    o_ref[0] = x_ref[0] + g1_ref[0] * acc


def _outproj(x, g1, hy, gla_f, gla_b, gla_gate, gdn_f, gdn_b, gdn_gate, gla_g, gdn_g, w_hy, w_gla, w_gdn,
             gla_dv, gdn_d, tm):
    b, l, d = x.shape
    hyw, glw, gdw = hy.shape[-1], gla_f.shape[-1], gdn_f.shape[-1]
    gl_heads = np.arange(glw) // gla_dv
    gd_heads = np.arange(gdw) // GDN_DP
    gl_m = jnp.asarray(gl_heads[:, None] == gl_heads[None, :], BF16)
    gd_m = jnp.asarray(gd_heads[:, None] == gd_heads[None, :], BF16)
    tok = lambda w: pl.BlockSpec((1, tm, w), lambda i, j: (i, j, 0))
    return pl.pallas_call(
        functools.partial(_outproj_kernel, gla_dv=gla_dv, gdn_d=gdn_d),
        grid=(b, l // tm),
        in_specs=[tok(d), pl.BlockSpec((1, 1, d), lambda i, j: (i, 0, 0)), tok(hyw), tok(glw), tok(glw), tok(glw),
                  tok(gdw), tok(gdw), tok(gdw), _const_spec((1, glw)), _const_spec((1, gdw)),
                  _const_spec(gl_m.shape), _const_spec(gd_m.shape),
                  _const_spec(w_hy.shape), _const_spec(w_gla.shape), _const_spec(w_gdn.shape)],
        out_specs=tok(d),
        out_shape=jax.ShapeDtypeStruct(x.shape, F32),
        compiler_params=_cparams(2),
        name="outproj",
    )(x, g1, hy, gla_f, gla_b, gla_gate, gdn_f, gdn_b, gdn_gate, gla_g, gdn_g, gl_m, gd_m, w_hy, w_gla, w_gdn)


def _mlp_kernel(x_ref, sh_ref, sc_ref, g2_ref, ng_ref, w1_ref, w2_ref, fg_ref, o_ref, hn_ref, acc_ref, *, final_norm):
    kk = pl.program_id(2)

    @pl.when(kk == 0)
    def _():
        hn_ref[...] = _rms_modulate(x_ref[0], ng_ref[...], sh_ref[0], sc_ref[0]).astype(BF16)
        acc_ref[...] = jnp.zeros_like(acc_ref)

    hid = jnp.maximum(jnp.dot(hn_ref[...], w1_ref[...], preferred_element_type=F32), 0.0)
    acc_ref[...] += _bdot(hid * hid, w2_ref[...])

    @pl.when(kk == pl.num_programs(2) - 1)
    def _():
        y = x_ref[0] + g2_ref[0] * acc_ref[...]
        if final_norm:
            y = y * lax.rsqrt(jnp.mean(y * y, axis=-1, keepdims=True) + NORM_EPS) * fg_ref[...]
        o_ref[0] = y


def _mlp(x, shift, scale, gate, norm_g, w1, w2, final_g, final_norm, tm, th):
    b, l, d = x.shape
    dff = w1.shape[1]
    vec = pl.BlockSpec((1, 1, d), lambda i, j, k: (i, 0, 0))
    row = pl.BlockSpec((1, d), lambda i, j, k: (0, 0))
    return pl.pallas_call(
        functools.partial(_mlp_kernel, final_norm=final_norm),
        grid=(b, l // tm, dff // th),
        in_specs=[pl.BlockSpec((1, tm, d), lambda i, j, k: (i, j, 0)), vec, vec, vec, row,
                  pl.BlockSpec((d, th), lambda i, j, k: (0, k)), pl.BlockSpec((th, d), lambda i, j, k: (k, 0)), row],
        out_specs=pl.BlockSpec((1, tm, d), lambda i, j, k: (i, j, 0)),
        out_shape=jax.ShapeDtypeStruct(x.shape, F32),
        scratch_shapes=[pltpu.VMEM((tm, d), BF16), pltpu.VMEM((tm, d), F32)],
        compiler_params=_cparams(3),
        name="mlp",
    )(x, shift, scale, gate, norm_g.reshape(1, d), w1, w2, final_g.reshape(1, d))


def _pad_heads(w, n_heads, axis):
    shape = w.shape
    d = shape[axis] // n_heads
    w = w.reshape(shape[:axis] + (n_heads, d) + shape[axis + 1:])
    pad = [(0, 0)] * w.ndim
    pad[axis + 1] = (0, GDN_DP - d)
    w = jnp.pad(w, pad)
    return w.reshape(shape[:axis] + (n_heads * GDN_DP,) + shape[axis + 1:])


def kernel(x, c, ctx, c_ctx, norm1_g, norm2_g, w_mod, b_mod, w_in, w_out, hy_conv_w, hy_conv_b, hy_f_w1, hy_f_b1, hy_f_w2, hy_f_b2, hy_f_w3, hy_sin_freq, hy_d, gla_w_a2, gla_b_a, gla_norm_g, gdn_conv_w, gdn_a_log, gdn_dt_bias, gdn_norm_g, w_mlp1, w_mlp2, final_norm_g):
    batch, seq, d_model = x.shape
    ctx_len = ctx.shape[1]
    depth = w_mod.shape[0]
    hy_c = hy_d.shape[-1]
    gla_kw = gla_w_a2.shape[-1]
    gla_dv = gla_norm_g.shape[-1]
    gla_vw = GLA_H * gla_dv
    gdn_d = gdn_norm_g.shape[-1]
    gdn_w = GDN_H * gdn_d
    gdn_wp = GDN_H * GDN_DP

    n2 = LANES
    n1 = 2 * seq // n2
    fft_tabs = _fft_tables(n1, n2)
    dft_tabs = _dft_tables(ctx_len)
    emb_lat = _hy_embedding(seq, HY_EMB_PAD)
    emb_ctx = _hy_embedding(ctx_len, HY_EMB_PAD)
    hy_deltas = jnp.abs(jnp.linspace(math.log(HY_DECAY_TARGET) / HY_SLOW_PCT,
                                     math.log(HY_DECAY_TARGET) / HY_FAST_PCT, hy_c, dtype=F32))[None, :]

    pad_rows = -(batch + 1) % 8
    cc = jnp.concatenate([c, c_ctx[None, :], jnp.zeros((pad_rows, d_model), F32)], axis=0)
    mod = _modulation(cc, w_mod, b_mod)

    def mod_vecs(l, i):
        v = mod[l, :, i * d_model:(i + 1) * d_model]
        lat = v[:batch, None, :]
        cx = jnp.broadcast_to(v[batch][None, None, :], (batch, 1, d_model))
        return lat, cx

    zeros_gla = jnp.zeros((batch, gla_vw, gla_kw), F32)
    zeros_gdn = jnp.zeros((batch, GDN_H, GDN_DP, GDN_DP), F32)
    gla_bd_gain = jnp.tile(gla_norm_g, (1, GLA_H))

    for l in range(depth):
        with_ctx_out = l < depth - 1
        (sh1, csh1), (sc1, csc1), (g1, cg1), (sh2, csh2), (sc2, csc2), (g2, cg2) = [mod_vecs(l, i) for i in range(N_MOD)]

        wl = w_in[l]
        o_gla = 3 * hy_c
        o_gdn = o_gla + 2 * gla_kw + 2 * gla_vw + 2 * GLA_RANK
        o_gdn_gate = o_gdn + 3 * gdn_w
        o_gdn_small = o_gdn_gate + gdn_w
        w_small = jnp.concatenate([wl[:, o_gdn - 2 * GLA_RANK:o_gdn], wl[:, o_gdn_small:]], axis=1)
        w_small = jnp.pad(w_small, ((0, 0), (0, LANES - w_small.shape[1])))
        in_plain = [
            wl[:, o_gla:o_gla + 2 * gla_kw],
            wl[:, o_gla + 2 * gla_kw:o_gla + 2 * gla_kw + gla_vw],
            wl[:, o_gla + 2 * gla_kw + gla_vw:o_gla + 2 * gla_kw + 2 * gla_vw],
            _pad_heads(wl[:, o_gdn_gate:o_gdn_small], GDN_H, 1),
            w_small,
        ]
        in_plain = [w.astype(BF16) for w in in_plain]
        in_hy = wl[:, :o_gla].astype(BF16)
        in_gdn = _pad_heads(wl[:, o_gdn:o_gdn_gate], 3 * GDN_H, 1).astype(BF16)
        gdn_cw = _pad_heads(gdn_conv_w[l], 3 * GDN_H, 1)
        gdn_gain = _pad_heads(gdn_norm_g[l][None, :].repeat(GDN_H, 0).reshape(1, gdn_w), GDN_H, 1)
        wo = w_out[l]
        w_hy = wo[:hy_c].astype(BF16)
        w_gla = wo[hy_c:hy_c + gla_vw].astype(BF16)
        w_gdn = _pad_heads(wo[hy_c + gla_vw:], GDN_H, 0).astype(BF16)
        w1 = w_mlp1[l].astype(BF16)
        w2 = w_mlp2[l].astype(BF16)
        filt_args = (hy_f_w1[l], hy_f_b1[l], hy_f_w2[l], hy_f_b2[l], hy_f_w3[l], hy_sin_freq[l], hy_deltas)

        def mixer_parts(tokens, shift, scale, seg_len, s0_gla, s0_gdn):
            length = tokens.shape[1]
            tl = min(SCAN_ROWS, length)
            z_qk, z_v, z_gate, z_gdn_gate, z_small, u, q, k, v = _inproj(
                tokens, shift, scale, norm1_g[l], in_plain, in_hy, in_gdn, hy_conv_w[l], hy_conv_b[l], gdn_cw,
                seg_len, gdn_d ** -0.5, min(PROJ_ROWS, length))
            b_f, b_b, gc = _gates(z_small, gla_w_a2[l], gla_b_a[l], gdn_a_log[l], gdn_dt_bias[l], tl)
            gla_f, gla_b, gla_sf, gla_sb = _gla_scan(z_qk, z_v, b_f, b_b, s0_gla[0], s0_gla[1], tl)
            gdn_f, gdn_b, gdn_sf, gdn_sb = _gdn_scan(q, k, v, gc, s0_gdn[0], s0_gdn[1], tl)
            return u, (gla_f, gla_b, z_gate), (gdn_f, gdn_b, z_gdn_gate), (gla_sf, gla_sb), (gdn_sf, gdn_sb)

        def finish(tokens, gate1, hy, gla, gdn):
            return _outproj(tokens, gate1, hy, *gla, *gdn, gla_bd_gain[l][None, :], gdn_gain, w_hy, w_gla, w_gdn,
                            gla_dv, gdn_d, min(PROJ_ROWS, tokens.shape[1]))

        def mlp(tokens, shift, scale, gate2, final_norm):
            return _mlp(tokens, shift, scale, gate2, norm2_g[l], w1, w2, final_norm_g, final_norm,
                        min(MLP_ROWS, tokens.shape[1]), MLP_HIDDEN)

        u_c, gla_c, gdn_c, gla_s, gdn_s = mixer_parts(ctx, csh1, csc1, ctx_len,
                                                      (zeros_gla, zeros_gla), (zeros_gdn, zeros_gdn))
        u_l, gla_l, gdn_l, _, _ = mixer_parts(x, sh1, sc1, GRID_W, gla_s, gdn_s)
        h_re, h_im = _filter_spectrum(*_hy_filters(emb_lat, *filt_args, min(PROJ_ROWS, seq)), fft_tabs, n1, n2)
        nb = hy_c // LANES
        y1 = _fftconv_gate(u_l, 0, u_l, nb, h_re, h_im, 0, hy_d[l, 0], fft_tabs, n1, n2)
        hy_l = _fftconv_gate(y1, 0, u_l, 2 * nb, h_re, h_im, nb, hy_d[l, 1], fft_tabs, n1, n2)
        x = finish(x, g1, hy_l, gla_l, gdn_l)
        x = mlp(x, sh2, sc2, g2, not with_ctx_out)

        if with_ctx_out:
            coef = _dense_filter_spectrum(*_hy_filters(emb_ctx, *filt_args, ctx_len), dft_tabs)
            y1c = _dftconv_gate(u_c, 0, u_c, 1, *coef, 0, hy_d[l, 0], dft_tabs)
            hy_c_out = _dftconv_gate(y1c, 0, u_c, 2, *coef, 1, hy_d[l, 1], dft_tabs)
            ctx = finish(ctx, cg1, hy_c_out, gla_c, gdn_c)
            ctx = mlp(ctx, csh2, csc2, cg2, False)
    return x
```
